```python
import jax, jax.numpy as jnp
from jax import lax
import numpy as np

D_MODEL = 2048
BATCH = 4
SEQ = 2048
DEPTH = 2
DEC_BATCH = 128
DEC_SEQ = 1
PAST_LEN = 16384
PAGE_SIZE = 128

N_EVEN = (DEPTH + 1) // 2
N_ODD = DEPTH // 2
D_RNN = D_MODEL // 2
RG_HEADS = 8
RG_BLOCK = D_RNN // RG_HEADS
RG_CONV = 4
RG_C = 8.0
GLA_HEADS = 4
GLA_DK = D_MODEL // 4 // GLA_HEADS
GLA_DV = D_MODEL // 2 // GLA_HEADS
GLA_RANK = 16
GLA_TAU = 16.0
GLA_CHUNK = 64
IN0_SPLITS = (D_RNN, D_RNN, GLA_HEADS * GLA_DK, GLA_HEADS * GLA_DK, GLA_HEADS * GLA_DV, GLA_HEADS * GLA_DV, GLA_RANK)
D_IN0 = 2 * D_RNN + 2 * GLA_HEADS * GLA_DK + 2 * GLA_HEADS * GLA_DV + GLA_RANK
D_MIX0 = D_RNN + GLA_HEADS * GLA_DV
D_CONV = D_MODEL
SC_WIDTH = 3
D_FF = 5632
N_EXPERTS = 8
TOP_K = 2
EPS = 1e-6

kernel_name = 'hybrid_rglru_gla_shortconv_moe_step'


def rmsnorm(x, g):
    xf = x.astype(jnp.float32)
    y = xf * lax.rsqrt(jnp.mean(xf * xf, axis=-1, keepdims=True) + EPS)
    return (y * g.astype(jnp.float32)).astype(x.dtype)


def causal_dwconv(u, buf, w):
    T = u.shape[1]
    width = w.shape[0]
    up = jnp.concatenate([buf.astype(u.dtype), u], axis=1)
    y = up[:, 0:T] * w[0]
    for k in range(1, width):
        y = y + up[:, k:k + T] * w[k]
    return y, up[:, T:]


def linear_recurrence(a, b, h0):
    def combine(left, right):
        a_l, b_l = left
        a_r, b_r = right
        return a_r * a_l, a_r * b_l + b_r
    a_cum, h = lax.associative_scan(combine, (a, b), axis=1)
    h = h + a_cum * h0[:, None, :]
    return h, h[:, -1]


def rglru(u, h0, pos0, w_a, b_a, w_x, b_x, lam):
    B, T, _ = u.shape
    f32 = jnp.float32
    uf = u.astype(f32)
    ub = uf.reshape(B, T, RG_HEADS, RG_BLOCK)
    r = jax.nn.sigmoid(jnp.einsum('bthi,hij->bthj', ub, w_a.astype(f32)).reshape(B, T, D_RNN) + b_a.astype(f32))
    i = jax.nn.sigmoid(jnp.einsum('bthi,hij->bthj', ub, w_x.astype(f32)).reshape(B, T, D_RNN) + b_x.astype(f32))
    log_a = -RG_C * r * jax.nn.softplus(-lam.astype(f32))
    a = jnp.exp(log_a)
    mult = jnp.sqrt(-jnp.expm1(2.0 * log_a))
    reset = (jnp.arange(T) + pos0) == 0
    mult = jnp.where(reset[None, :, None], 1.0, mult)
    return linear_recurrence(a, mult * i * uf, h0.astype(f32))


def gla_chunked(q, k, v, log_alpha, S0, chunk):
    B, H, T, DK = q.shape
    DV = v.shape[-1]
    n = T // chunk
    def to_chunks(t):
        return t.reshape(B, H, n, chunk, t.shape[-1]).transpose(2, 0, 1, 3, 4)
    causal = jnp.tril(jnp.ones((chunk, chunk), dtype=bool))[None, None, :, :, None]
    def step(S, inp):
        qc, kc, vc, gc = inp
        bcum = jnp.cumsum(gc, axis=2)
        o_inter = jnp.einsum('bhcd,bhde->bhce', qc * jnp.exp(bcum), S)
        diff = bcum[:, :, :, None, :] - bcum[:, :, None, :, :]
        decay = jnp.exp(jnp.where(causal, diff, -jnp.inf))
        attn = jnp.einsum('bhid,bhjd,bhijd->bhij', qc, kc, decay)
        o_intra = jnp.einsum('bhij,bhje->bhie', attn, vc)
        b_last = bcum[:, :, -1]
        k_dec = kc * jnp.exp(b_last[:, :, None, :] - bcum)
        S_new = jnp.exp(b_last)[..., None] * S + jnp.einsum('bhcd,bhce->bhde', k_dec, vc)
        return S_new, o_inter + o_intra
    S_T, o = lax.scan(step, S0, (to_chunks(q), to_chunks(k), to_chunks(v), to_chunks(log_alpha)))
    o = o.transpose(1, 2, 0, 3, 4).reshape(B, H, T, DV)
    return o, S_T


def mixer_rglru_gla(h, st_conv, st_h, st_S, pos0, w_in, rg_conv_w, rg_conv_b, rg_w_a, rg_b_a, rg_w_x, rg_b_x,
                    rg_lambda, gla_w_gate, gla_b_gate, gla_norm, w_out):
    B, T, _ = h.shape
    f32 = jnp.float32
    z = h @ w_in
    bounds = np.cumsum(IN0_SPLITS)[:-1].tolist()
    xr, gr, q, k, v, g, lr = jnp.split(z, bounds, axis=-1)
    xc, new_conv = causal_dwconv(xr, st_conv, rg_conv_w)
    xc = xc + rg_conv_b
    hr, new_h = rglru(xc, st_h, pos0, rg_w_a, rg_b_a, rg_w_x, rg_b_x, rg_lambda)
    y_rg = hr.astype(h.dtype) * jax.nn.gelu(gr)
    def heads(t, d):
        return t.astype(f32).reshape(B, T, GLA_HEADS, d).transpose(0, 2, 1, 3)
    log_alpha = jax.nn.log_sigmoid(lr.astype(f32) @ gla_w_gate.astype(f32) + gla_b_gate.astype(f32)) / GLA_TAU
    chunk = GLA_CHUNK if T % GLA_CHUNK == 0 else T
    o, new_S = gla_chunked(heads(q, GLA_DK) * (GLA_DK ** -0.5), heads(k, GLA_DK), heads(v, GLA_DV),
                           heads(log_alpha, GLA_DK), st_S.astype(f32), chunk)
    o = rmsnorm(o.transpose(0, 2, 1, 3), gla_norm).reshape(B, T, GLA_HEADS * GLA_DV)
    y_gla = o.astype(h.dtype) * jax.nn.silu(g)
    y = jnp.concatenate([y_rg, y_gla], axis=-1) @ w_out
    return y, new_conv, new_h, new_S


def mixer_shortconv(h, st_buf, w_in, conv_w, w_out):
    gate_b, gate_c, val = jnp.split(h @ w_in, 3, axis=-1)
    u, new_buf = causal_dwconv(gate_c * val, st_buf, conv_w)
    return (gate_b * u) @ w_out, new_buf


def swiglu(h, wg, wu, wd):
    return (jax.nn.silu(h @ wg) * (h @ wu)) @ wd


def moe(h, w_router, wg, wu, wd):
    logits = (h @ w_router).astype(jnp.float32)
    top_v, top_i = lax.top_k(logits, TOP_K)
    gates = jax.nn.softmax(top_v, axis=-1)
    combine = jnp.sum(jax.nn.one_hot(top_i, N_EXPERTS, dtype=jnp.float32) * gates[..., None], axis=-2)
    out = jnp.zeros_like(h)
    for e in range(N_EXPERTS):
        out = out + combine[..., e:e + 1].astype(h.dtype) * swiglu(h, wg[e], wu[e], wd[e])
    return out


def trunk(x, st_rg_conv, st_rg_h, st_gla, st_sc, pos0, w):
    h = x
    rgc, rgh, gla, sc = [], [], [], []
    for l in range(DEPTH):
        j = l // 2
        if l % 2 == 0:
            y, c, hh, S = mixer_rglru_gla(rmsnorm(h, w['norm_mix_e'][j]), st_rg_conv[j], st_rg_h[j], st_gla[j], pos0,
                                          w['w_in_e'][j], w['rg_conv_w'][j], w['rg_conv_b'][j], w['rg_w_a'][j],
                                          w['rg_b_a'][j], w['rg_w_x'][j], w['rg_b_x'][j], w['rg_lambda'][j],
                                          w['gla_w_gate'][j], w['gla_b_gate'][j], w['gla_norm'][j], w['w_out_e'][j])
            h = h + y
            h = h + swiglu(rmsnorm(h, w['norm_ffn_e'][j]), w['ffn_w_gate'][j], w['ffn_w_up'][j], w['ffn_w_down'][j])
            rgc.append(c)
            rgh.append(hh)
            gla.append(S)
        else:
            y, c = mixer_shortconv(rmsnorm(h, w['norm_mix_o'][j]), st_sc[j], w['w_in_o'][j], w['sc_conv_w'][j],
                                   w['w_out_o'][j])
            h = h + y
            h = h + moe(rmsnorm(h, w['norm_ffn_o'][j]), w['router_w'][j], w['moe_w_gate'][j], w['moe_w_up'][j],
                        w['moe_w_down'][j])
            sc.append(c)
    y = rmsnorm(h, w['final_norm'])
    dt = x.dtype
    return (y, jnp.stack(rgc).astype(dt), jnp.stack(rgh).astype(dt), jnp.stack(gla).astype(dt),
            jnp.stack(sc).astype(dt))


def setup_inputs(seed: int = 0) -> dict:
    key = jax.random.key(seed)
    ks = jax.random.split(key, 40)
    f32 = jnp.float32
    def nrm(i, shape, scale):
        return jax.random.normal(ks[i], shape, f32) * scale
    a8 = jax.random.uniform(ks[14], (N_EVEN, D_RNN), f32, minval=0.9, maxval=0.999)
    a = a8 ** (1.0 / RG_C)
    rg_lambda = jnp.log(a) - jnp.log1p(-a)
    return {
        'x_prompt': nrm(0, (BATCH, SEQ, D_MODEL), 1.0),
        'x_sample': nrm(1, (DEC_BATCH, DEC_SEQ, D_MODEL), 1.0),
        'state_rg_conv': nrm(2, (N_EVEN, DEC_BATCH, RG_CONV - 1, D_RNN), 1.0),
        'state_rg_h': nrm(3, (N_EVEN, DEC_BATCH, D_RNN), 0.5),
        'state_gla': nrm(4, (N_EVEN, DEC_BATCH, GLA_HEADS, GLA_DK, GLA_DV), 0.3),
        'state_sc_conv': nrm(5, (N_ODD, DEC_BATCH, SC_WIDTH - 1, D_CONV), 1.0),
        'norm_mix_e': 1.0 + nrm(6, (N_EVEN, D_MODEL), 0.02),
        'w_in_e': nrm(7, (N_EVEN, D_MODEL, D_IN0), D_MODEL ** -0.5),
        'rg_conv_w': nrm(8, (N_EVEN, RG_CONV, D_RNN), RG_CONV ** -0.5),
        'rg_conv_b': nrm(9, (N_EVEN, D_RNN), 0.02),
        'rg_w_a': nrm(10, (N_EVEN, RG_HEADS, RG_BLOCK, RG_BLOCK), RG_BLOCK ** -0.5),
        'rg_b_a': nrm(11, (N_EVEN, D_RNN), 0.1),
        'rg_w_x': nrm(12, (N_EVEN, RG_HEADS, RG_BLOCK, RG_BLOCK), RG_BLOCK ** -0.5),
        'rg_b_x': nrm(13, (N_EVEN, D_RNN), 0.1),
        'rg_lambda': rg_lambda,
        'gla_w_gate': nrm(15, (N_EVEN, GLA_RANK, GLA_HEADS * GLA_DK), GLA_RANK ** -0.5),
        'gla_b_gate': nrm(16, (N_EVEN, GLA_HEADS * GLA_DK), 0.1),
        'gla_norm': 1.0 + nrm(17, (N_EVEN, GLA_DV), 0.02),
        'w_out_e': nrm(18, (N_EVEN, D_MIX0, D_MODEL), D_MIX0 ** -0.5),
        'norm_ffn_e': 1.0 + nrm(19, (N_EVEN, D_MODEL), 0.02),
        'ffn_w_gate': nrm(20, (N_EVEN, D_MODEL, D_FF), D_MODEL ** -0.5),
        'ffn_w_up': nrm(21, (N_EVEN, D_MODEL, D_FF), D_MODEL ** -0.5),
        'ffn_w_down': nrm(22, (N_EVEN, D_FF, D_MODEL), D_FF ** -0.5),
        'norm_mix_o': 1.0 + nrm(23, (N_ODD, D_MODEL), 0.02),
        'w_in_o': nrm(24, (N_ODD, D_MODEL, 3 * D_CONV), D_MODEL ** -0.5),
        'sc_conv_w': nrm(25, (N_ODD, SC_WIDTH, D_CONV), SC_WIDTH ** -0.5),
        'w_out_o': nrm(26, (N_ODD, D_CONV, D_MODEL), D_CONV ** -0.5),
        'norm_ffn_o': 1.0 + nrm(27, (N_ODD, D_MODEL), 0.02),
        'router_w': nrm(28, (N_ODD, D_MODEL, N_EXPERTS), D_MODEL ** -0.5),
        'moe_w_gate': nrm(29, (N_ODD, N_EXPERTS, D_MODEL, D_FF), D_MODEL ** -0.5),
        'moe_w_up': nrm(30, (N_ODD, N_EXPERTS, D_MODEL, D_FF), D_MODEL ** -0.5),
        'moe_w_down': nrm(31, (N_ODD, N_EXPERTS, D_FF, D_MODEL), D_FF ** -0.5),
        'final_norm': 1.0 + nrm(32, (D_MODEL,), 0.02),
    }


def reference(x_prompt, x_sample, state_rg_conv, state_rg_h, state_gla, state_sc_conv, norm_mix_e, w_in_e,
              rg_conv_w, rg_conv_b, rg_w_a, rg_b_a, rg_w_x, rg_b_x, rg_lambda, gla_w_gate, gla_b_gate, gla_norm,
              w_out_e, norm_ffn_e, ffn_w_gate, ffn_w_up, ffn_w_down, norm_mix_o, w_in_o, sc_conv_w, w_out_o,
              norm_ffn_o, router_w, moe_w_gate, moe_w_up, moe_w_down, final_norm):
    w = dict(norm_mix_e=norm_mix_e, w_in_e=w_in_e, rg_conv_w=rg_conv_w, rg_conv_b=rg_conv_b, rg_w_a=rg_w_a,
             rg_b_a=rg_b_a, rg_w_x=rg_w_x, rg_b_x=rg_b_x, rg_lambda=rg_lambda, gla_w_gate=gla_w_gate,
             gla_b_gate=gla_b_gate, gla_norm=gla_norm, w_out_e=w_out_e, norm_ffn_e=norm_ffn_e,
             ffn_w_gate=ffn_w_gate, ffn_w_up=ffn_w_up, ffn_w_down=ffn_w_down, norm_mix_o=norm_mix_o,
             w_in_o=w_in_o, sc_conv_w=sc_conv_w, w_out_o=w_out_o, norm_ffn_o=norm_ffn_o, router_w=router_w,
             moe_w_gate=moe_w_gate, moe_w_up=moe_w_up, moe_w_down=moe_w_down, final_norm=final_norm)
    bp = x_prompt.shape[0]
    f32 = jnp.float32
    p_rg_conv = jnp.zeros((N_EVEN, bp, RG_CONV - 1, D_RNN), x_prompt.dtype)
    p_rg_h = jnp.zeros((N_EVEN, bp, D_RNN), f32)
    p_gla = jnp.zeros((N_EVEN, bp, GLA_HEADS, GLA_DK, GLA_DV), f32)
    p_sc = jnp.zeros((N_ODD, bp, SC_WIDTH - 1, D_CONV), x_prompt.dtype)
    y_prompt, rgc_p, rgh_p, gla_p, sc_p = trunk(x_prompt, p_rg_conv, p_rg_h, p_gla, p_sc, 0, w)
    y_sample, rgc_s, rgh_s, gla_s, sc_s = trunk(x_sample, state_rg_conv, state_rg_h, state_gla, state_sc_conv,
                                                PAST_LEN, w)
    return (y_prompt, y_sample, rgc_p, rgc_s, rgh_p, rgh_s, gla_p, gla_s, sc_p, sc_s)
```

```python
import functools

import jax
import jax.numpy as jnp
from jax import lax
from jax.experimental import pallas as pl
from jax.experimental.pallas import tpu as pltpu

F32 = jnp.float32
BF16 = jnp.bfloat16
U32 = jnp.uint32
I32 = jnp.int32
SDS = jax.ShapeDtypeStruct

EPS = 1e-6
RG_C = 8.0
GLA_TAU = 16.0
GLA_CHUNK = 64
TOP_K = 2
LANE = 128
VMEM_LIMIT = 56 * 1024 * 1024
ARB = "arbitrary"
TM_TARGET = 640
TP_TARGET = 512
GLA_TB_TARGET = 512
MOE_SUB = 256
MOE_SPR = 10


def _cparams(n_axes, vmem=VMEM_LIMIT):
    return pltpu.CompilerParams(dimension_semantics=(ARB,) * n_axes, vmem_limit_bytes=vmem)


def _tile(n, target, align):
    best = None
    for t in range(align, min(n, target) + 1, align):
        if n % t == 0:
            best = t
    assert best is not None, (n, target, align)
    return best


def _rms(xf, g):
    ms = jnp.mean(xf * xf, axis=-1, keepdims=True)
    return xf * lax.rsqrt(ms + EPS) * g


def _sigmoid(x):
    return 1.0 / (1.0 + jnp.exp(-x))


def _softplus(x):
    return jnp.maximum(x, 0.0) + jnp.log1p(jnp.exp(-jnp.abs(x)))


def _gelu_tanh(x):
    c = 0.7978845608028654
    return x * (0.5 * (1.0 + jnp.tanh(c * (x + 0.044715 * (x * x * x)))))


def _dot(a, b):
    return jnp.dot(a, b, preferred_element_type=F32)


def _shift_rows(x, s, row, fill):
    return jnp.where(row >= s, pltpu.roll(x, s, 0), fill)


def _norm_in(xp, xs, g, tp):
    mp, d = xp.shape
    ms = xs.shape[0]
    n_p = mp // tp
    assert mp % tp == 0 and ms <= tp and ms % 16 == 0

    def kern(xp_ref, xs_ref, g_ref, o_ref):
        i = pl.program_id(0)

        @pl.when(i < n_p)
        def _():
            o_ref[...] = _rms(xp_ref[...], g_ref[...]).astype(BF16)

        @pl.when(i == n_p)
        def _():
            o_ref[0:ms, :] = _rms(xs_ref[...], g_ref[...]).astype(BF16)

    return pl.pallas_call(
        kern, grid=(n_p + 1,),
        in_specs=[pl.BlockSpec((tp, d), lambda i: (jnp.minimum(i, n_p - 1), 0)),
                  pl.BlockSpec((ms, d), lambda i: (0, 0)),
                  pl.BlockSpec((1, d), lambda i: (0, 0))],
        out_specs=pl.BlockSpec((tp, d), lambda i: (i, 0)),
        out_shape=SDS((mp + ms, d), BF16),
        compiler_params=_cparams(1), name="norm_in")(xp, xs, g)


def _in_proj0(x, w3, n_cols, tm, tn):
    m, k = x.shape

    def kern(x_ref, w_ref, o_ref, wb_ref):
        @pl.when(pl.program_id(1) == 0)
        def _():
            wb_ref[...] = w_ref[...].astype(BF16)

        o_ref[...] = _dot(x_ref[...], wb_ref[...])

    return pl.pallas_call(
        kern, grid=(n_cols // tn, m // tm),
        in_specs=[pl.BlockSpec((tm, k), lambda n, i: (i, 0)),
                  pl.BlockSpec((None, k, tn), lambda n, i: (0, 0, n))],
        out_specs=pl.BlockSpec((tm, tn), lambda n, i: (i, n)),
        out_shape=SDS((m, n_cols), F32),
        scratch_shapes=[pltpu.VMEM((k, tn), BF16)],
        compiler_params=_cparams(2), name="in_proj0")(x, w3)


def _gla_gate(x, w_lr, w_gate, b_gate, tm):
    m, k = x.shape
    r = w_lr.shape[1]
    n = w_gate.shape[1]

    def kern(x_ref, wl_ref, wg_ref, bg_ref, o_ref):
        lr = _dot(x_ref[...], wl_ref[...].astype(BF16))
        pre = jnp.dot(lr, wg_ref[...], preferred_element_type=F32,
                      precision=lax.Precision.HIGHEST) + bg_ref[...]
        o_ref[...] = -_softplus(-pre) * (1.0 / GLA_TAU)

    return pl.pallas_call(
        kern, grid=(m // tm,),
        in_specs=[pl.BlockSpec((tm, k), lambda i: (i, 0)),
                  pl.BlockSpec((k, r), lambda i: (0, 0)),
                  pl.BlockSpec((r, n), lambda i: (0, 0)),
                  pl.BlockSpec((1, n), lambda i: (0, 0))],
        out_specs=pl.BlockSpec((tm, n), lambda i: (i, 0)),
        out_shape=SDS((m, n), F32),
        compiler_params=_cparams(1), name="gla_gate")(x, w_lr, w_gate, b_gate)


def _rg_gates(xc, wa_ref, wx_ref, ba_ref, bx_ref, lam_ref):
    w2 = jnp.concatenate([wa_ref[...], wx_ref[...]], axis=1).astype(BF16)
    pre = _dot(xc.astype(BF16), w2)
    r = _sigmoid(pre[:, :LANE] + ba_ref[...])
    i = _sigmoid(pre[:, LANE:] + bx_ref[...])
    log_a = (-RG_C) * r * _softplus(-lam_ref[...])
    a = jnp.exp(log_a)
    mult = jnp.sqrt(jnp.tanh(-log_a) * (1.0 + a * a))
    return a, mult, i


def _rglru_prompt(z, cw, cb, wa, ba, wx, bx, lam, bsz, t, d_rnn, d_out):
    nh = d_rnn // LANE
    m = z.shape[0]

    def kern(xr_ref, gr_ref, cw_ref, cb_ref, wa_ref, ba_ref, wx_ref, bx_ref, lam_ref,
             y_ref, conv_ref, h_ref):
        xr = xr_ref[...]
        row = lax.broadcasted_iota(I32, (t, LANE), 0)
        w = cw_ref[...]
        nw = w.shape[0]
        xc = w[nw - 1:nw, :] * xr
        for s in range(1, nw):
            xc = xc + w[nw - 1 - s:nw - s, :] * _shift_rows(xr, s, row, 0.0)
        xc = xc + cb_ref[...]
        a, mult, i = _rg_gates(xc, wa_ref, wx_ref, ba_ref, bx_ref, lam_ref)
        mult = jnp.where(row == 0, 1.0, mult)
        b = mult * i * xc
        s = 1
        while s < t:
            b = a * _shift_rows(b, s, row, 0.0) + b
            a = a * _shift_rows(a, s, row, 1.0)
            s *= 2
        y_ref[...] = (b * _gelu_tanh(gr_ref[...])).astype(BF16)
        conv_ref[...] = xr[t - (nw - 1):t, :]
        h_ref[...] = b[t - 1:t, :]

    vec = pl.BlockSpec((1, LANE), lambda b_, h: (0, h))
    blk = pl.BlockSpec((None, LANE, LANE), lambda b_, h: (h, 0, 0))
    nw = cw.shape[0]
    return pl.pallas_call(
        kern, grid=(bsz, nh),
        in_specs=[pl.BlockSpec((t, LANE), lambda b_, h: (b_, h)),
                  pl.BlockSpec((t, LANE), lambda b_, h: (b_, nh + h)),
                  pl.BlockSpec((nw, LANE), lambda b_, h: (0, h)),
                  vec, blk, vec, blk, vec, vec],
        out_specs=[pl.BlockSpec((t, LANE), lambda b_, h: (b_, h)),
                   pl.BlockSpec((None, nw - 1, LANE), lambda b_, h: (b_, 0, h)),
                   pl.BlockSpec((None, 1, LANE), lambda b_, h: (b_, 0, h))],
        out_shape=[SDS((m, d_out), BF16), SDS((bsz, nw - 1, d_rnn), F32), SDS((bsz, 1, d_rnn), F32)],
        compiler_params=_cparams(2), name="rglru_prompt")(z, z, cw, cb, wa, ba, wx, bx, lam)


def _rglru_sample(z, y_mix, st_conv, st_h, cw, cb, wa, ba, wx, bx, lam, mp, ms, d_rnn):
    nh = d_rnn // LANE
    nw = cw.shape[0]
    rb = mp // ms
    assert mp % ms == 0

    def kern(xr_ref, gr_ref, y_in_ref, sc_ref, sh_ref, cw_ref, cb_ref, wa_ref, ba_ref, wx_ref,
             bx_ref, lam_ref, y_ref, conv_ref, h_ref):
        del y_in_ref
        xr = xr_ref[...]
        w = cw_ref[...]
        xc = w[nw - 1:nw, :] * xr
        for s in range(nw - 1):
            xc = xc + w[s:s + 1, :] * sc_ref[:, s, :]
        xc = xc + cb_ref[...]
        a, mult, i = _rg_gates(xc, wa_ref, wx_ref, ba_ref, bx_ref, lam_ref)
        h = a * sh_ref[...] + mult * i * xc
        y_ref[...] = (h * _gelu_tanh(gr_ref[...])).astype(BF16)
        for s in range(nw - 2):
            conv_ref[:, s, :] = sc_ref[:, s + 1, :]
        conv_ref[:, nw - 2, :] = xr
        h_ref[...] = h

    vec = pl.BlockSpec((1, LANE), lambda h: (0, h))
    blk = pl.BlockSpec((None, LANE, LANE), lambda h: (h, 0, 0))
    return pl.pallas_call(
        kern, grid=(nh,),
        in_specs=[pl.BlockSpec((ms, LANE), lambda h: (rb, h)),
                  pl.BlockSpec((ms, LANE), lambda h: (rb, nh + h)),
                  pl.BlockSpec(memory_space=pl.ANY),
                  pl.BlockSpec((ms, nw - 1, LANE), lambda h: (0, 0, h)),
                  pl.BlockSpec((ms, LANE), lambda h: (0, h)),
                  pl.BlockSpec((nw, LANE), lambda h: (0, h)),
                  vec, blk, vec, blk, vec, vec],
        out_specs=[pl.BlockSpec((ms, LANE), lambda h: (rb, h)),
                   pl.BlockSpec((ms, nw - 1, LANE), lambda h: (0, 0, h)),
                   pl.BlockSpec((ms, LANE), lambda h: (0, h))],
        out_shape=[SDS(y_mix.shape, BF16), SDS((ms, nw - 1, d_rnn), F32), SDS((ms, d_rnn), F32)],
        input_output_aliases={2: 0},
        compiler_params=_cparams(1), name="rglru_sample")(
            z, z, y_mix, st_conv, st_h, cw, cb, wa, ba, wx, bx, lam)


def _gla_out(o, gn, g):
    return _rms(o, gn) * (g * _sigmoid(g))


def _gla_prompt(z, la, y_mix, gn, bsz, t, nh, dk, dv, col_q, col_y):
    tb = _tile(t, GLA_TB_TARGET, GLA_CHUNK)
    nt = t // tb
    nc = tb // GLA_CHUNK
    c = GLA_CHUNK
    hk, hv = nh * dk, nh * dv
    assert col_q % hk == 0 and (col_q + 2 * hk) % hv == 0 and col_y % hv == 0
    cq, ck = col_q // hk, col_q // hk + 1
    cv, cg = (col_q + 2 * hk) // hv, (col_q + 2 * hk) // hv + 1
    scale = dk ** -0.5

    def kern(q_ref, k_ref, v_ref, g_ref, la_ref, gn_ref, y_in_ref, y_ref, s_ref, st_ref):
        del y_in_ref
        tbi = pl.program_id(1)

        @pl.when(tbi == 0)
        def _():
            st_ref[...] = jnp.zeros_like(st_ref)

        row = lax.broadcasted_iota(I32, (c, dk), 0)
        causal = (lax.broadcasted_iota(I32, (c, c), 0) >= lax.broadcasted_iota(I32, (c, c), 1))

        def chunk(ci, carry):
            rows = pl.ds(pl.multiple_of(ci * c, c), c)
            for hd in range(nh):
                ks = slice(hd * dk, (hd + 1) * dk)
                vs = slice(hd * dv, (hd + 1) * dv)
                q = q_ref[rows, ks] * scale
                k = k_ref[rows, ks]
                v = v_ref[rows, vs].astype(BF16)
                bc = la_ref[rows, ks]
                s = 1
                while s < c:
                    bc = bc + _shift_rows(bc, s, row, 0.0)
                    s *= 2
                b_last = bc[c - 1:c, :]
                qe = (q * jnp.exp(bc)).astype(BF16)
                ke = (k * jnp.exp(-bc)).astype(BF16)
                kd = (k * jnp.exp(b_last - bc)).astype(BF16)
                st = st_ref[hd]
                o = lax.dot_general(qe, st.astype(BF16), (((1,), (1,)), ((), ())),
                                    preferred_element_type=F32)
                attn = lax.dot_general(qe, ke, (((1,), (1,)), ((), ())),
                                       preferred_element_type=F32)
                attn = jnp.where(causal, attn, 0.0).astype(BF16)
                o = o + _dot(attn, v)
                st_ref[hd] = st * jnp.exp(b_last) + lax.dot_general(
                    v, kd, (((0,), (0,)), ((), ())), preferred_element_type=F32)
                y_ref[rows, vs] = _gla_out(o, gn_ref[...], g_ref[rows, vs]).astype(BF16)
            return carry

        lax.fori_loop(0, nc, chunk, 0)

        @pl.when(tbi == nt - 1)
        def _():
            for hd in range(nh):
                s_ref[hd] = st_ref[hd].T

    m = z.shape[0]
    return pl.pallas_call(
        kern, grid=(bsz, nt),
        in_specs=[pl.BlockSpec((tb, hk), lambda b_, i: (b_ * nt + i, cq)),
                  pl.BlockSpec((tb, hk), lambda b_, i: (b_ * nt + i, ck)),
                  pl.BlockSpec((tb, hv), lambda b_, i: (b_ * nt + i, cv)),
                  pl.BlockSpec((tb, hv), lambda b_, i: (b_ * nt + i, cg)),
                  pl.BlockSpec((tb, hk), lambda b_, i: (b_ * nt + i, 0)),
                  pl.BlockSpec((1, dv), lambda b_, i: (0, 0)),
                  pl.BlockSpec(memory_space=pl.ANY)],
        out_specs=[pl.BlockSpec((tb, hv), lambda b_, i: (b_ * nt + i, col_y // hv)),
                   pl.BlockSpec((None, nh, dk, dv), lambda b_, i: (b_, 0, 0, 0))],
        out_shape=[SDS(y_mix.shape, BF16), SDS((bsz, nh, dk, dv), F32)],
        scratch_shapes=[pltpu.VMEM((nh, dv, dk), F32)],
        input_output_aliases={6: 0},
        compiler_params=_cparams(2), name="gla_prompt")(z, z, z, z, la, gn, y_mix)


def _gla_sample(z, la, y_mix, st, gn, mp, ms, nh, dk, dv, col_q, col_y):
    bb = 16
    assert ms % bb == 0 and mp % ms == 0 and dk == LANE
    ns = ms // bb
    hk, hv = nh * dk, nh * dv
    cq, ck = col_q // hk, col_q // hk + 1
    cv, cg = (col_q + 2 * hk) // hv, (col_q + 2 * hk) // hv + 1
    scale = dk ** -0.5

    def kern(q_ref, k_ref, la_ref, v_ref, g_ref, gn_ref, st_ref, y_in_ref, y_ref, so_ref,
             qt_ref, kt_ref, at_ref):
        del y_in_ref
        i = pl.program_id(0)

        @pl.when(i == 0)
        def _():
            for hd in range(nh):
                ks = slice(hd * dk, (hd + 1) * dk)
                qt = (q_ref[:, ks] * scale).T
                kt = k_ref[:, ks].T
                at = jnp.exp(la_ref[:, ks]).T
                for s in range(ns):
                    qt_ref[s, hd] = qt[:, s * bb:(s + 1) * bb]
                    kt_ref[s, hd] = kt[:, s * bb:(s + 1) * bb]
                    at_ref[s, hd] = at[:, s * bb:(s + 1) * bb]

        for hd in range(nh):
            vs = slice(hd * dv, (hd + 1) * dv)
            qt = qt_ref[i, hd]
            kt = kt_ref[i, hd]
            at = at_ref[i, hd]
            outs = []
            for j in range(bb):
                v = v_ref[j:j + 1, vs]
                s_new = at[:, j:j + 1] * st_ref[j, hd] + kt[:, j:j + 1] * v
                so_ref[j, hd] = s_new
                outs.append(jnp.sum(qt[:, j:j + 1] * s_new, axis=0, keepdims=True))
            o = jnp.concatenate(outs, axis=0)
            y_ref[:, vs] = _gla_out(o, gn_ref[...], g_ref[:, vs]).astype(BF16)

    rb = mp // ms
    rbb = mp // bb
    return pl.pallas_call(
        kern, grid=(ns,),
        in_specs=[pl.BlockSpec((ms, hk), lambda i: (rb, cq)),
                  pl.BlockSpec((ms, hk), lambda i: (rb, ck)),
                  pl.BlockSpec((ms, hk), lambda i: (rb, 0)),
                  pl.BlockSpec((bb, hv), lambda i: (rbb + i, cv)),
                  pl.BlockSpec((bb, hv), lambda i: (rbb + i, cg)),
                  pl.BlockSpec((1, dv), lambda i: (0, 0)),
                  pl.BlockSpec((bb, nh, dk, dv), lambda i: (i, 0, 0, 0)),
                  pl.BlockSpec(memory_space=pl.ANY)],
        out_specs=[pl.BlockSpec((bb, hv), lambda i: (rbb + i, col_y // hv)),
                   pl.BlockSpec((bb, nh, dk, dv), lambda i: (i, 0, 0, 0))],
        out_shape=[SDS(y_mix.shape, BF16), SDS((ms, nh, dk, dv), F32)],
        scratch_shapes=[pltpu.VMEM((ns, nh, dk, bb), F32)] * 3,
        input_output_aliases={7: 0},
        compiler_params=_cparams(1), name="gla_sample")(z, z, la, z, z, gn, st, y_mix)


def _cast_bf16(w3):
    _, k, n = w3.shape
    tk = _tile(k, 512, 16)

    def kern(w_ref, o_ref):
        o_ref[...] = w_ref[...].astype(BF16)

    return pl.pallas_call(
        kern, grid=(k // tk,),
        in_specs=[pl.BlockSpec((None, tk, n), lambda i: (0, i, 0))],
        out_specs=pl.BlockSpec((tk, n), lambda i: (i, 0)),
        out_shape=SDS((k, n), BF16),
        compiler_params=_cparams(1), name="cast_bf16")(w3)


def _pack_bf16_pairs(hn):
    half = hn.shape[1] // 2
    r = hn.astype(BF16).astype(F32)
    lo = pltpu.bitcast(r[:, :half], U32)
    hi = pltpu.bitcast(r[:, half:], U32)
    return hi | (lo >> 16)


def _unpack_bf16_pairs(xu):
    lo = pltpu.bitcast(xu << 16, F32).astype(BF16)
    hi = pltpu.bitcast(xu & jnp.uint32(0xFFFF0000), F32).astype(BF16)
    return lo, hi


def _route(hn, rw):
    logits = _dot(hn.astype(BF16), rw.astype(BF16))
    ne = float(logits.shape[1])
    lane = lax.broadcasted_iota(I32, logits.shape, 1).astype(F32)
    m1 = jnp.max(logits, axis=1, keepdims=True)
    i1 = jnp.min(jnp.where(logits == m1, lane, ne), axis=1, keepdims=True)
    sel1 = lane == i1
    rest = jnp.where(sel1, -jnp.inf, logits)
    m2 = jnp.max(rest, axis=1, keepdims=True)
    i2 = jnp.min(jnp.where(rest == m2, lane, ne), axis=1, keepdims=True)
    sel2 = lane == i2
    e2 = jnp.exp(m2 - m1)
    g1 = 1.0 / (1.0 + e2)
    g2 = e2 / (1.0 + e2)
    mh = jnp.where(sel1 | sel2, 1.0, 0.0)
    gd = jnp.where(sel1, g1, 0.0) + jnp.where(sel2, g2, 0.0)
    return mh, gd


def _out_proj(y, wb, res, g, tm, res_tail=None, router_w=None):
    m, k = y.shape
    d = wb.shape[1]
    n_m = m // tm
    split = res_tail is not None
    if split:
        mp, ms = res.shape[0], res_tail.shape[0]
        assert mp + ms == m and ms <= tm
        head = tm - ms
        n_rb = pl.cdiv(mp, tm)
    route = router_w is not None

    def kern(*refs):
        it = iter(refs)
        y_ref, w_ref, r_ref = next(it), next(it), next(it)
        rt_ref = next(it) if split else None
        g_ref = next(it)
        rw_ref = next(it) if route else None
        h_ref, n_ref = next(it), next(it)
        mh_ref, gd_ref = (next(it), next(it)) if route else (None, None)
        acc = _dot(y_ref[...], w_ref[...])

        def emit(rows, h):
            h_ref[rows, :] = h
            hn = _rms(h, g_ref[...])
            if route:
                n_ref[rows, :] = _pack_bf16_pairs(hn)
                mh, gd = _route(hn, rw_ref[...])
                mh_ref[rows, :] = mh
                gd_ref[rows, :] = gd
            else:
                n_ref[rows, :] = hn.astype(BF16)

        if not split:
            emit(slice(0, tm), acc + r_ref[...])
        else:
            i = pl.program_id(0)

            @pl.when(i < n_m - 1)
            def _():
                emit(slice(0, tm), acc + r_ref[...])

            @pl.when(i == n_m - 1)
            def _():
                if head > 0:
                    emit(slice(0, head), acc[:head] + r_ref[0:head, :])
                emit(slice(head, tm), acc[head:] + rt_ref[...])

    in_specs = [pl.BlockSpec((tm, k), lambda i: (i, 0)),
                pl.BlockSpec((k, d), lambda i: (0, 0))]
    args = [y, wb]
    if split:
        in_specs += [pl.BlockSpec((tm, d), lambda i: (jnp.minimum(i, n_rb - 1), 0)),
                     pl.BlockSpec((ms, d), lambda i: (0, 0))]
        args += [res, res_tail]
    else:
        in_specs += [pl.BlockSpec((tm, d), lambda i: (i, 0))]
        args += [res]
    in_specs += [pl.BlockSpec((1, d), lambda i: (0, 0))]
    args += [g]
    out_specs = [pl.BlockSpec((tm, d), lambda i: (i, 0))]
    out_shape = [SDS((m, d), F32)]
    if route:
        ne = router_w.shape[1]
        in_specs += [pl.BlockSpec((d, ne), lambda i: (0, 0))]
        args += [router_w]
        out_specs += [pl.BlockSpec((tm, d // 2), lambda i: (i, 0)),
                      pl.BlockSpec((tm, ne), lambda i: (i, 0)),
                      pl.BlockSpec((tm, ne), lambda i: (i, 0))]
        out_shape += [SDS((m, d // 2), U32), SDS((m, ne), F32), SDS((m, ne), F32)]
    else:
        out_specs += [pl.BlockSpec((tm, d), lambda i: (i, 0))]
        out_shape += [SDS((m, d), BF16)]
    return pl.pallas_call(
        kern, grid=(n_m,), in_specs=in_specs, out_specs=out_specs, out_shape=out_shape,
        compiler_params=_cparams(1), name="out_proj_route" if route else "out_proj")(*args)


def _ffn(x, wg3, wu3, wd3, tm, tf):
    m, k = x.shape
    f = wg3.shape[2]
    d = wd3.shape[2]

    def kern(x_ref, wg_ref, wu_ref, wd_ref, o_ref):
        @pl.when(pl.program_id(1) == 0)
        def _():
            o_ref[...] = jnp.zeros_like(o_ref)

        xb = x_ref[...]
        gg = _dot(xb, wg_ref[...].astype(BF16))
        uu = _dot(xb, wu_ref[...].astype(BF16))
        a = (gg * _sigmoid(gg) * uu).astype(BF16)
        o_ref[...] += _dot(a, wd_ref[...].astype(BF16))

    return pl.pallas_call(
        kern, grid=(m // tm, f // tf),
        in_specs=[pl.BlockSpec((tm, k), lambda i, j: (i, 0)),
                  pl.BlockSpec((None, k, tf), lambda i, j: (0, 0, j)),
                  pl.BlockSpec((None, k, tf), lambda i, j: (0, 0, j)),
                  pl.BlockSpec((None, tf, d), lambda i, j: (0, j, 0))],
        out_specs=pl.BlockSpec((tm, d), lambda i, j: (i, 0)),
        out_shape=SDS((m, d), F32),
        compiler_params=_cparams(2), name="ffn")(x, wg3, wu3, wd3)


def _add_norm(a, b, g, tm):
    m, d = a.shape

    def kern(a_ref, b_ref, g_ref, h_ref, n_ref):
        h = a_ref[...] + b_ref[...]
        h_ref[...] = h
        n_ref[...] = _rms(h, g_ref[...]).astype(BF16)

    row = pl.BlockSpec((tm, d), lambda i: (i, 0))
    return pl.pallas_call(
        kern, grid=(m // tm,),
        in_specs=[row, row, pl.BlockSpec((1, d), lambda i: (0, 0))],
        out_specs=[row, row],
        out_shape=[SDS((m, d), F32), SDS((m, d), BF16)],
        compiler_params=_cparams(1), name="add_norm")(a, b, g)


def _in_proj1(x, w3, tm, tn):
    m, k = x.shape
    dc = w3.shape[2] // 3
    nb = dc // tn

    def kern(x_ref, wb_ref, wc_ref, wv_ref, gb_ref, cv_ref, sb_ref, sc_ref, sv_ref):
        @pl.when(pl.program_id(1) == 0)
        def _():
            sb_ref[...] = wb_ref[...].astype(BF16)
            sc_ref[...] = wc_ref[...].astype(BF16)
            sv_ref[...] = wv_ref[...].astype(BF16)

        xb = x_ref[...]
        gb_ref[...] = _dot(xb, sb_ref[...]).astype(BF16)
        cv_ref[...] = _dot(xb, sc_ref[...]) * _dot(xb, sv_ref[...])

    def wspec(g):
        return pl.BlockSpec((None, k, tn), lambda n, i: (0, 0, g * nb + n))

    return pl.pallas_call(
        kern, grid=(nb, m // tm),
        in_specs=[pl.BlockSpec((tm, k), lambda n, i: (i, 0)), wspec(0), wspec(1), wspec(2)],
        out_specs=[pl.BlockSpec((tm, tn), lambda n, i: (i, n))] * 2,
        out_shape=[SDS((m, dc), BF16), SDS((m, dc), F32)],
        scratch_shapes=[pltpu.VMEM((k, tn), BF16)] * 3,
        compiler_params=_cparams(2), name="in_proj1")(x, w3, w3, w3)


def _shortconv_prompt(gb, cv, cw, bsz, t):
    m, dc = cv.shape
    tc = _tile(dc, 512, LANE)
    nw = cw.shape[0]

    def kern(gb_ref, cv_ref, cw_ref, u_ref, buf_ref):
        x = cv_ref[...]
        row = lax.broadcasted_iota(I32, x.shape, 0)
        w = cw_ref[...]
        u = w[nw - 1:nw, :] * x
        for s in range(1, nw):
            u = u + w[nw - 1 - s:nw - s, :] * _shift_rows(x, s, row, 0.0)
        u_ref[...] = (gb_ref[...].astype(F32) * u).astype(BF16)
        buf_ref[...] = x[t - (nw - 1):t, :]

    blk = pl.BlockSpec((t, tc), lambda b_, c: (b_, c))
    return pl.pallas_call(
        kern, grid=(bsz, dc // tc),
        in_specs=[blk, blk, pl.BlockSpec((nw, tc), lambda b_, c: (0, c))],
        out_specs=[blk, pl.BlockSpec((None, nw - 1, tc), lambda b_, c: (b_, 0, c))],
        out_shape=[SDS((m, dc), BF16), SDS((bsz, nw - 1, dc), F32)],
        compiler_params=_cparams(2), name="shortconv_prompt")(gb, cv, cw)


def _shortconv_sample(gb, cv, u_all, st, cw, mp, ms):
    dc = cv.shape[1]
    tc = _tile(dc, 512, LANE)
    nw = cw.shape[0]
    rb = mp // ms

    def kern(gb_ref, cv_ref, u_in_ref, st_ref, cw_ref, u_ref, buf_ref):
        del u_in_ref
        x = cv_ref[...]
        w = cw_ref[...]
        u = w[nw - 1:nw, :] * x
        for s in range(nw - 1):
            u = u + w[s:s + 1, :] * st_ref[:, s, :]
        u_ref[...] = (gb_ref[...].astype(F32) * u).astype(BF16)
        for s in range(nw - 2):
            buf_ref[:, s, :] = st_ref[:, s + 1, :]
        buf_ref[:, nw - 2, :] = x

    blk = pl.BlockSpec((ms, tc), lambda c: (rb, c))
    stb = pl.BlockSpec((ms, nw - 1, tc), lambda c: (0, 0, c))
    return pl.pallas_call(
        kern, grid=(dc // tc,),
        in_specs=[blk, blk, pl.BlockSpec(memory_space=pl.ANY), stb,
                  pl.BlockSpec((nw, tc), lambda c: (0, c))],
        out_specs=[blk, stb],
        out_shape=[SDS(u_all.shape, BF16), SDS((ms, nw - 1, dc), F32)],
        input_output_aliases={2: 0},
        compiler_params=_cparams(1), name="shortconv_sample")(gb, cv, u_all, st, cw)


def _moe_rank(mh, tm):
    m, ne = mh.shape

    def kern(mh_ref, ex_ref, cnt_ref, carry_ref):
        @pl.when(pl.program_id(0) == 0)
        def _():
            carry_ref[...] = jnp.zeros_like(carry_ref)

        x = mh_ref[...]
        tri = (lax.broadcasted_iota(I32, (tm, tm), 0) > lax.broadcasted_iota(I32, (tm, tm), 1))
        ex = _dot(jnp.where(tri, 1.0, 0.0).astype(BF16), x.astype(BF16)) + carry_ref[...]
        ex_ref[...] = ex
        tot = ex[tm - 1:tm, :] + x[tm - 1:tm, :]
        carry_ref[...] = tot
        cnt_ref[...] = tot

    return pl.pallas_call(
        kern, grid=(m // tm,),
        in_specs=[pl.BlockSpec((tm, ne), lambda i: (i, 0))],
        out_specs=[pl.BlockSpec((tm, ne), lambda i: (i, 0)), pl.BlockSpec((1, ne), lambda i: (0, 0))],
        out_shape=[SDS((m, ne), F32), SDS((1, ne), F32)],
        scratch_shapes=[pltpu.VMEM((1, ne), F32)],
        compiler_params=_cparams(1), name="moe_rank")(mh)


def _moe_pos(mh, gd, ex, off, tm):
    m, ne = mh.shape

    def kern(mh_ref, gd_ref, ex_ref, off_ref, pos_ref, gate_ref):
        sel = mh_ref[...] > 0.5
        pd = ex_ref[...] + off_ref[...]
        big = jnp.float32(3e38)
        p_lo = jnp.min(jnp.where(sel, pd, big), axis=1, keepdims=True)
        p_hi = jnp.max(jnp.where(sel, pd, -big), axis=1, keepdims=True)
        gdv = gd_ref[...]
        g_lo = jnp.sum(jnp.where(sel & (pd == p_lo), gdv, 0.0), axis=1, keepdims=True)
        g_hi = jnp.sum(jnp.where(sel & (pd == p_hi), gdv, 0.0), axis=1, keepdims=True)
        pos_ref[:, 0:1] = p_lo.astype(I32)
        pos_ref[:, 1:2] = p_hi.astype(I32)
        gate_ref[:, 0:1] = g_lo
        gate_ref[:, 1:2] = g_hi

    blk = pl.BlockSpec((tm, ne), lambda i: (i, 0))
    two = pl.BlockSpec((tm, TOP_K), lambda i: (i, 0))
    return pl.pallas_call(
        kern, grid=(m // tm,),
        in_specs=[blk, blk, blk, pl.BlockSpec((1, ne), lambda i: (0, 0))],
        out_specs=[two, two],
        out_shape=[SDS((m, TOP_K), I32), SDS((m, TOP_K), F32)],
        compiler_params=_cparams(1), name="moe_pos")(mh, gd, ex, off)


def _moe_scatter(pos_flat, pad_rows, n_pad, xp, n_rows, tm):
    m, w = xp.shape
    zr = 8

    def kern(pos_ref, pad_ref, npad_ref, x_ref, xs_ref, zero_ref, sem, zsem):
        i = pl.program_id(0)

        def row_copy(r, p):
            return pltpu.make_async_copy(x_ref.at[pl.ds(r, 1)], xs_ref.at[pl.ds(p, 1)], sem)

        def issue(r, c):
            t = i * tm + r
            row_copy(r, pos_ref[TOP_K * t]).start()
            row_copy(r, pos_ref[TOP_K * t + 1]).start()
            return c

        lax.fori_loop(0, tm, issue, 0, unroll=8)

        @pl.when(i == 0)
        def _():
            zero_ref[...] = jnp.zeros_like(zero_ref)
            npad = npad_ref[0]

            def zcopy(j):
                return pltpu.make_async_copy(zero_ref.at[pl.ds(0, 1)],
                                             xs_ref.at[pl.ds(pad_ref[j], 1)], zsem)

            def zissue(j, c):
                zcopy(j).start()
                return c

            def zwait(j, c):
                zcopy(j).wait()
                return c

            lax.fori_loop(0, npad, zissue, 0)
            lax.fori_loop(0, npad, zwait, 0)

        def drain(r, c):
            row_copy(0, 0).wait()
            row_copy(0, 0).wait()
            return c

        lax.fori_loop(0, tm, drain, 0, unroll=8)

    return pl.pallas_call(
        kern,
        grid_spec=pltpu.PrefetchScalarGridSpec(
            num_scalar_prefetch=3, grid=(m // tm,),
            in_specs=[pl.BlockSpec((tm, w), lambda i, *_: (i, 0))],
            out_specs=pl.BlockSpec(memory_space=pl.ANY),
            scratch_shapes=[pltpu.VMEM((zr, w), U32), pltpu.SemaphoreType.DMA(()),
                            pltpu.SemaphoreType.DMA(())]),
        out_shape=SDS((n_rows, w), U32),
        compiler_params=_cparams(1), name="moe_scatter")(pos_flat, pad_rows, n_pad, xp)


def _moe_ffn(st_e, st_row, st_nsub, xs, wg4, wu4, wd4, rs, sub, tf):
    n_rows, half = xs.shape
    k = 2 * half
    f = wg4.shape[3]
    d = wd4.shape[3]
    ng = st_e.shape[0]
    nf = f // tf

    def kern(se_ref, sr_ref, sn_ref, xs_ref, wg_ref, wu_ref, wd_ref, ys_ref,
             x_ref, acc_ref, wgb_ref, wub_ref, wdb_ref, sem_in, sem_out):
        g = pl.program_id(0)
        j = pl.program_id(1)
        nsub = sn_ref[g]
        row0 = sr_ref[g]

        def in_copy(s):
            return pltpu.make_async_copy(
                xs_ref.at[pl.ds(pl.multiple_of(row0 + s * sub, sub), sub)],
                x_ref.at[pl.ds(pl.multiple_of(s * sub, sub), sub)], sem_in)

        def out_copy(s):
            return pltpu.make_async_copy(
                acc_ref.at[pl.ds(pl.multiple_of(s * sub, sub), sub)],
                ys_ref.at[pl.ds(pl.multiple_of(row0 + s * sub, sub), sub)], sem_out)

        def each(fn):
            def body(s, c):
                fn(s)
                return c
            lax.fori_loop(0, nsub, body, 0)

        @pl.when(nsub > 0)
        def _():
            @pl.when(j == 0)
            def _():
                each(lambda s: in_copy(s).start())
                each(lambda s: in_copy(s).wait())

            wgb_ref[...] = wg_ref[...].astype(BF16)
            wub_ref[...] = wu_ref[...].astype(BF16)
            wdb_ref[...] = wd_ref[...].astype(BF16)

            def step(s, first):
                rows = pl.ds(pl.multiple_of(s * sub, sub), sub)
                lo, hi = _unpack_bf16_pairs(x_ref[rows, :])
                gg = _dot(lo, wgb_ref[0:half, :]) + _dot(hi, wgb_ref[half:k, :])
                uu = _dot(lo, wub_ref[0:half, :]) + _dot(hi, wub_ref[half:k, :])
                a = (gg * _sigmoid(gg) * uu).astype(BF16)
                dd = _dot(a, wdb_ref[...])
                if first:
                    acc_ref[rows, :] = dd
                else:
                    acc_ref[rows, :] += dd

            @pl.when(j == 0)
            def _():
                each(lambda s: step(s, True))

            @pl.when(j > 0)
            def _():
                each(lambda s: step(s, False))

            @pl.when(j == nf - 1)
            def _():
                each(lambda s: out_copy(s).start())
                each(lambda s: out_copy(s).wait())

    def widx(g, j, se, sr, sn):
        return (0, se[g], 0, jnp.where(sn[g] > 0, j, nf - 1))

    def didx(g, j, se, sr, sn):
        return (0, se[g], jnp.where(sn[g] > 0, j, nf - 1), 0)

    return pl.pallas_call(
        kern,
        grid_spec=pltpu.PrefetchScalarGridSpec(
            num_scalar_prefetch=3, grid=(ng, nf),
            in_specs=[pl.BlockSpec(memory_space=pl.ANY),
                      pl.BlockSpec((None, None, k, tf), widx),
                      pl.BlockSpec((None, None, k, tf), widx),
                      pl.BlockSpec((None, None, tf, d), didx)],
            out_specs=pl.BlockSpec(memory_space=pl.ANY),
            scratch_shapes=[pltpu.VMEM((rs, half), U32), pltpu.VMEM((rs, d), F32),
                            pltpu.VMEM((k, tf), BF16), pltpu.VMEM((k, tf), BF16),
                            pltpu.VMEM((tf, d), BF16),
                            pltpu.SemaphoreType.DMA(()), pltpu.SemaphoreType.DMA(())]),
        out_shape=SDS((n_rows, d), F32),
        compiler_params=_cparams(2), name="moe_ffn")(st_e, st_row, st_nsub, xs, wg4, wu4, wd4)


def _moe_combine(pos_flat, h, gates, ys, g, mp, ms, tp):
    m, d = h.shape
    n_p = mp // tp
    assert mp % tp == 0 and ms <= tp

    def kern(pos_ref, h_ref, gate_ref, ys_ref, g_ref, yp_ref, ysm_ref, a_ref, b_ref, sem):
        i = pl.program_id(0)
        nrow = jnp.where(i < n_p, tp, ms)

        def copies(r):
            t = i * tp + r
            return (pltpu.make_async_copy(ys_ref.at[pl.ds(pos_ref[TOP_K * t], 1)],
                                          a_ref.at[pl.ds(r, 1)], sem),
                    pltpu.make_async_copy(ys_ref.at[pl.ds(pos_ref[TOP_K * t + 1], 1)],
                                          b_ref.at[pl.ds(r, 1)], sem))

        def issue(r, c):
            ca, cb = copies(r)
            ca.start()
            cb.start()
            return c

        def drain(r, c):
            ca, cb = copies(r)
            ca.wait()
            cb.wait()
            return c

        lax.fori_loop(0, nrow, issue, 0)
        lax.fori_loop(0, nrow, drain, 0)

        def out(rows):
            gt = gate_ref[rows, :]
            hh = h_ref[rows, :] + gt[:, 0:1] * a_ref[rows, :] + gt[:, 1:2] * b_ref[rows, :]
            return _rms(hh, g_ref[...])

        @pl.when(i < n_p)
        def _():
            yp_ref[...] = out(slice(0, tp))

        @pl.when(i == n_p)
        def _():
            ysm_ref[...] = out(slice(0, ms))

    return pl.pallas_call(
        kern,
        grid_spec=pltpu.PrefetchScalarGridSpec(
            num_scalar_prefetch=1, grid=(n_p + 1,),
            in_specs=[pl.BlockSpec((tp, d), lambda i, *_: (i, 0)),
                      pl.BlockSpec((tp, TOP_K), lambda i, *_: (i, 0)),
                      pl.BlockSpec(memory_space=pl.ANY),
                      pl.BlockSpec((1, d), lambda i, *_: (0, 0))],
            out_specs=[pl.BlockSpec((tp, d), lambda i, *_: (jnp.minimum(i, n_p - 1), 0)),
                       pl.BlockSpec((ms, d), lambda i, *_: (0, 0))],
            scratch_shapes=[pltpu.VMEM((tp, d), F32), pltpu.VMEM((tp, d), F32),
                            pltpu.SemaphoreType.DMA(())]),
        out_shape=[SDS((mp, d), F32), SDS((ms, d), F32)],
        compiler_params=_cparams(1), name="moe_combine")(pos_flat, h, gates, ys, g)


def _moe_tables(cnt, sub, rs, ng):
    ne = cnt.shape[0]
    nsub_e = (cnt + sub - 1) // sub
    size_e = nsub_e * sub
    off = jnp.cumsum(size_e) - size_e
    spr = rs // sub
    nst_e = (nsub_e + spr - 1) // spr
    st_start = jnp.cumsum(nst_e) - nst_e
    n_act = jnp.sum(nst_e)
    gidx = jnp.arange(ng, dtype=I32)
    e_of = jnp.sum((gidx[:, None] >= st_start[None, :]).astype(I32), axis=1) - 1
    e_of = jnp.clip(e_of, 0, ne - 1)
    kth = gidx - st_start[e_of]
    active = gidx < n_act
    nsub = jnp.where(active, jnp.clip(nsub_e[e_of] - kth * spr, 0, spr), 0)
    row = off[e_of] + kth * rs
    last = jnp.maximum(n_act - 1, 0)
    st_e = jnp.where(active, e_of, e_of[last]).astype(I32)
    st_row = jnp.where(active, row, 0).astype(I32)
    n_pad_e = size_e - cnt
    pad_start = jnp.cumsum(n_pad_e) - n_pad_e
    n_pad = jnp.sum(n_pad_e)
    pidx = jnp.arange(ne * (sub - 1), dtype=I32)
    pe = jnp.clip(jnp.sum((pidx[:, None] >= pad_start[None, :]).astype(I32), axis=1) - 1, 0, ne - 1)
    pad_rows = jnp.where(pidx < n_pad, off[pe] + cnt[pe] + (pidx - pad_start[pe]), 0).astype(I32)
    return off, st_e, st_row, nsub.astype(I32), pad_rows, n_pad.astype(I32).reshape(1)


def kernel(x_prompt, x_sample, state_rg_conv, state_rg_h, state_gla, state_sc_conv, norm_mix_e, w_in_e, rg_conv_w, rg_conv_b, rg_w_a, rg_b_a, rg_w_x, rg_b_x, rg_lambda, gla_w_gate, gla_b_gate, gla_norm, w_out_e, norm_ffn_e, ffn_w_gate, ffn_w_up, ffn_w_down, norm_mix_o, w_in_o, sc_conv_w, w_out_o, norm_ffn_o, router_w, moe_w_gate, moe_w_up, moe_w_down, final_norm):
    bsz, t, d = x_prompt.shape
    ms = x_sample.shape[0]
    assert x_sample.shape[1] == 1 and w_in_e.shape[0] == 1 and w_in_o.shape[0] == 1
    mp = bsz * t
    m = mp + ms
    d_rnn = rg_lambda.shape[1]
    nh, dk, dv = state_gla.shape[2:]
    hk, hv = nh * dk, nh * dv
    n_main = 2 * d_rnn + 2 * hk + 2 * hv
    d_mix = d_rnn + hv
    ne = router_w.shape[2]
    xp = x_prompt.reshape(mp, d)
    xs = x_sample.reshape(ms, d)
    row = lambda v: v.reshape(1, -1)

    tp = _tile(mp, TP_TARGET, 16)
    tm = _tile(m, TM_TARGET, 16)

    hn0 = _norm_in(xp, xs, norm_mix_e, tp)
    z0 = _in_proj0(hn0, w_in_e, n_main, tm, _tile(n_main, 1024, LANE))
    la = _gla_gate(hn0, w_in_e[0, :, n_main:], gla_w_gate[0], gla_b_gate, tm)
    y_mix, rgc_p, rgh_p = _rglru_prompt(z0, rg_conv_w[0], rg_conv_b, rg_w_a[0], rg_b_a, rg_w_x[0],
                                        rg_b_x, rg_lambda, bsz, t, d_rnn, d_mix)
    y_mix, rgc_s, rgh_s = _rglru_sample(z0, y_mix, state_rg_conv[0], state_rg_h[0], rg_conv_w[0],
                                        rg_conv_b, rg_w_a[0], rg_b_a, rg_w_x[0], rg_b_x, rg_lambda,
                                        mp, ms, d_rnn)
    y_mix, gla_p = _gla_prompt(z0, la, y_mix, gla_norm, bsz, t, nh, dk, dv, 2 * d_rnn, d_rnn)
    y_mix, gla_s = _gla_sample(z0, la, y_mix, state_gla[0], gla_norm, mp, ms, nh, dk, dv,
                               2 * d_rnn, d_rnn)
    h1, hn1 = _out_proj(y_mix, _cast_bf16(w_out_e), xp, norm_ffn_e, tm, res_tail=xs)

    f0 = _ffn(hn1, ffn_w_gate, ffn_w_up, ffn_w_down, tm, _tile(ffn_w_gate.shape[2], 512, LANE))
    h2, hn2 = _add_norm(h1, f0, norm_mix_o, tm)

    gb, cv = _in_proj1(hn2, w_in_o, tm, _tile(w_in_o.shape[2] // 3, 512, LANE))
    u, sc_p = _shortconv_prompt(gb, cv, sc_conv_w[0], bsz, t)
    u, sc_s = _shortconv_sample(gb, cv, u, state_sc_conv[0], sc_conv_w[0], mp, ms)
    h3, hn3p, mh, gd = _out_proj(u, _cast_bf16(w_out_o), h2, norm_ffn_o, tm, router_w=router_w[0])

    sub = MOE_SUB
    rs = MOE_SPR * sub
    ng = (TOP_K * m) // rs + ne
    n_rows = TOP_K * m + ne * (sub - 1)
    n_rows = ((n_rows + sub - 1) // sub) * sub
    ex, cnt = _moe_rank(mh, tm)
    off, st_e, st_row, st_nsub, pad_rows, n_pad = _moe_tables(cnt[0].astype(I32), sub, rs, ng)
    pos, gates = _moe_pos(mh, gd, ex, off.astype(F32).reshape(1, ne), tm)
    pos_flat = pos.reshape(TOP_K * m)
    xsort = _moe_scatter(pos_flat, pad_rows, n_pad, hn3p, n_rows, tm)
    ys = _moe_ffn(st_e, st_row, st_nsub, xsort, moe_w_gate, moe_w_up, moe_w_down, rs, sub,
                  _tile(moe_w_gate.shape[3], 256, LANE))
    y_p, y_s = _moe_combine(pos_flat, h3, gates, ys, row(final_norm), mp, ms, tp)

    return (y_p.reshape(bsz, t, d), y_s.reshape(ms, 1, d),
            rgc_p[None], rgc_s[None], rgh_p.reshape(1, bsz, d_rnn), rgh_s[None],
            gla_p[None], gla_s[None], sc_p[None], sc_s[None])
```

```python
import functools

import jax
import jax.numpy as jnp
from jax import lax
from jax.experimental import pallas as pl
from jax.experimental.pallas import tpu as pltpu

F32 = jnp.float32
BF16 = jnp.bfloat16
I32 = jnp.int32
SDS = jax.ShapeDtypeStruct

EPS = 1e-6
RG_C = 8.0
GLA_TAU = 16.0
GLA_CHUNK = 64
TOP_K = 2
LANE = 128
SUB8 = 8
VMEM_LIMIT = 56 * 1024 * 1024
ARB = "arbitrary"
TM_TARGET = 640
TP_TARGET = 512
GLA_TB_TARGET = 512
MOE_SUB = 256
MOE_SPR = 10


def _cparams(n_axes, vmem=VMEM_LIMIT):
    return pltpu.CompilerParams(dimension_semantics=(ARB,) * n_axes, vmem_limit_bytes=vmem)


def _tile(n, target, align):
    best = None
    for t in range(align, min(n, target) + 1, align):
        if n % t == 0:
            best = t
    assert best is not None, (n, target, align)
    return best


def _rms(xf, g):
    ms = jnp.mean(xf * xf, axis=-1, keepdims=True)
    return xf * lax.rsqrt(ms + EPS) * g


def _sigmoid(x):
    return 1.0 / (1.0 + jnp.exp(-x))


def _softplus(x):
    return jnp.maximum(x, 0.0) + jnp.log1p(jnp.exp(-jnp.abs(x)))


def _gelu_tanh(x):
    c = 0.7978845608028654
    return x * (0.5 * (1.0 + jnp.tanh(c * (x + 0.044715 * (x * x * x)))))


def _dot(a, b):
    return jnp.dot(a, b, preferred_element_type=F32)


def _shift_rows(x, s, row, fill):
    return jnp.where(row >= s, pltpu.roll(x, s, 0), fill)


def _norm_in(xp, xs, g, tp):
    mp, d = xp.shape
    ms = xs.shape[0]
    n_p = mp // tp
    assert mp % tp == 0 and ms <= tp and ms % 16 == 0

    def kern(xp_ref, xs_ref, g_ref, o_ref):
        i = pl.program_id(0)

        @pl.when(i < n_p)
        def _():
            o_ref[...] = _rms(xp_ref[...], g_ref[...]).astype(BF16)

        @pl.when(i == n_p)
        def _():
            o_ref[0:ms, :] = _rms(xs_ref[...], g_ref[...]).astype(BF16)

    return pl.pallas_call(
        kern, grid=(n_p + 1,),
        in_specs=[pl.BlockSpec((tp, d), lambda i: (jnp.minimum(i, n_p - 1), 0)),
                  pl.BlockSpec((ms, d), lambda i: (0, 0)),
                  pl.BlockSpec((1, d), lambda i: (0, 0))],
        out_specs=pl.BlockSpec((tp, d), lambda i: (i, 0)),
        out_shape=SDS((mp + ms, d), BF16),
        compiler_params=_cparams(1), name="norm_in")(xp, xs, g)


def _in_proj0(x, w3, n_cols, tm, tn):
    m, k = x.shape

    def kern(x_ref, w_ref, o_ref, wb_ref):
        @pl.when(pl.program_id(1) == 0)
        def _():
            wb_ref[...] = w_ref[...].astype(BF16)

        o_ref[...] = _dot(x_ref[...], wb_ref[...])

    return pl.pallas_call(
        kern, grid=(n_cols // tn, m // tm),
        in_specs=[pl.BlockSpec((tm, k), lambda n, i: (i, 0)),
                  pl.BlockSpec((None, k, tn), lambda n, i: (0, 0, n))],
        out_specs=pl.BlockSpec((tm, tn), lambda n, i: (i, n)),
        out_shape=SDS((m, n_cols), F32),
        scratch_shapes=[pltpu.VMEM((k, tn), BF16)],
        compiler_params=_cparams(2), name="in_proj0")(x, w3)


def _gla_gate(x, w_lr, w_gate, b_gate, tm):
    m, k = x.shape
    r = w_lr.shape[1]
    n = w_gate.shape[1]

    def kern(x_ref, wl_ref, wg_ref, bg_ref, o_ref):
        lr = _dot(x_ref[...], wl_ref[...].astype(BF16))
        pre = jnp.dot(lr, wg_ref[...], preferred_element_type=F32,
                      precision=lax.Precision.HIGHEST) + bg_ref[...]
        o_ref[...] = -_softplus(-pre) * (1.0 / GLA_TAU)

    return pl.pallas_call(
        kern, grid=(m // tm,),
        in_specs=[pl.BlockSpec((tm, k), lambda i: (i, 0)),
                  pl.BlockSpec((k, r), lambda i: (0, 0)),
                  pl.BlockSpec((r, n), lambda i: (0, 0)),
                  pl.BlockSpec((1, n), lambda i: (0, 0))],
        out_specs=pl.BlockSpec((tm, n), lambda i: (i, 0)),
        out_shape=SDS((m, n), F32),
        compiler_params=_cparams(1), name="gla_gate")(x, w_lr, w_gate, b_gate)


def _rg_gates(xc, wa_ref, wx_ref, ba_ref, bx_ref, lam_ref):
    w2 = jnp.concatenate([wa_ref[...], wx_ref[...]], axis=1).astype(BF16)
    pre = _dot(xc.astype(BF16), w2)
    r = _sigmoid(pre[:, :LANE] + ba_ref[...])
    i = _sigmoid(pre[:, LANE:] + bx_ref[...])
    log_a = (-RG_C) * r * _softplus(-lam_ref[...])
    a = jnp.exp(log_a)
    mult = jnp.sqrt(jnp.tanh(-log_a) * (1.0 + a * a))
    return a, mult, i


def _rglru_prompt(z, cw, cb, wa, ba, wx, bx, lam, bsz, t, d_rnn, d_out):
    nh = d_rnn // LANE
    m = z.shape[0]

    def kern(xr_ref, gr_ref, cw_ref, cb_ref, wa_ref, ba_ref, wx_ref, bx_ref, lam_ref,
             y_ref, conv_ref, h_ref, a_s, b_s, c_s):
        xr = xr_ref[...]
        row = lax.broadcasted_iota(I32, (t, LANE), 0)
        w = cw_ref[...]
        nw = w.shape[0]
        xc = w[nw - 1:nw, :] * xr
        for s in range(1, nw):
            xc = xc + w[nw - 1 - s:nw - s, :] * _shift_rows(xr, s, row, 0.0)
        xc = xc + cb_ref[...]
        a, mult, i = _rg_gates(xc, wa_ref, wx_ref, ba_ref, bx_ref, lam_ref)
        mult = jnp.where(row == 0, 1.0, mult)
        b = mult * i * xc
        ng = t // SUB8
        a3, b3 = a.reshape(ng, SUB8, LANE), b.reshape(ng, SUB8, LANE)
        r8 = lax.broadcasted_iota(I32, (ng, SUB8, LANE), 1)
        s = 1
        while s < SUB8:
            b3 = a3 * jnp.where(r8 >= s, pltpu.roll(b3, s, 1), 0.0) + b3
            a3 = a3 * jnp.where(r8 >= s, pltpu.roll(a3, s, 1), 1.0)
            s *= 2
        a_s[...] = a3.reshape(t, LANE)
        b_s[...] = b3.reshape(t, LANE)
        ag = a_s[pl.ds(SUB8 - 1, ng, stride=SUB8), :]
        bg = b_s[pl.ds(SUB8 - 1, ng, stride=SUB8), :]
        rowg = lax.broadcasted_iota(I32, (ng, LANE), 0)
        s = 1
        while s < ng:
            bg = ag * _shift_rows(bg, s, rowg, 0.0) + bg
            ag = ag * _shift_rows(ag, s, rowg, 1.0)
            s *= 2
        carry = _shift_rows(bg, 1, rowg, 0.0)
        for k in range(SUB8):
            c_s[pl.ds(k, ng, stride=SUB8), :] = carry
        h = b_s[...] + a_s[...] * c_s[...]
        y_ref[...] = (h * _gelu_tanh(gr_ref[...])).astype(BF16)
        conv_ref[...] = xr[t - (nw - 1):t, :]
        h_ref[...] = h[t - 1:t, :]

    vec = pl.BlockSpec((1, LANE), lambda b_, h: (0, h))
    blk = pl.BlockSpec((None, LANE, LANE), lambda b_, h: (h, 0, 0))
    nw = cw.shape[0]
    return pl.pallas_call(
        kern, grid=(bsz, nh),
        in_specs=[pl.BlockSpec((t, LANE), lambda b_, h: (b_, h)),
                  pl.BlockSpec((t, LANE), lambda b_, h: (b_, nh + h)),
                  pl.BlockSpec((nw, LANE), lambda b_, h: (0, h)),
                  vec, blk, vec, blk, vec, vec],
        out_specs=[pl.BlockSpec((t, LANE), lambda b_, h: (b_, h)),
                   pl.BlockSpec((None, nw - 1, LANE), lambda b_, h: (b_, 0, h)),
                   pl.BlockSpec((None, 1, LANE), lambda b_, h: (b_, 0, h))],
        out_shape=[SDS((m, d_out), BF16), SDS((bsz, nw - 1, d_rnn), F32), SDS((bsz, 1, d_rnn), F32)],
        scratch_shapes=[pltpu.VMEM((t, LANE), F32)] * 3,
        compiler_params=_cparams(2), name="rglru_prompt")(z, z, cw, cb, wa, ba, wx, bx, lam)


def _rglru_sample(z, y_mix, st_conv, st_h, cw, cb, wa, ba, wx, bx, lam, mp, ms, d_rnn):
    nh = d_rnn // LANE
    nw = cw.shape[0]
    rb = mp // ms
    assert mp % ms == 0

    def kern(xr_ref, gr_ref, y_in_ref, sc_ref, sh_ref, cw_ref, cb_ref, wa_ref, ba_ref, wx_ref,
             bx_ref, lam_ref, y_ref, conv_ref, h_ref):
        del y_in_ref
        xr = xr_ref[...]
        w = cw_ref[...]
        xc = w[nw - 1:nw, :] * xr
        for s in range(nw - 1):
            xc = xc + w[s:s + 1, :] * sc_ref[:, s, :]
        xc = xc + cb_ref[...]
        a, mult, i = _rg_gates(xc, wa_ref, wx_ref, ba_ref, bx_ref, lam_ref)
        h = a * sh_ref[...] + mult * i * xc
        y_ref[...] = (h * _gelu_tanh(gr_ref[...])).astype(BF16)
        for s in range(nw - 2):
            conv_ref[:, s, :] = sc_ref[:, s + 1, :]
        conv_ref[:, nw - 2, :] = xr
        h_ref[...] = h

    vec = pl.BlockSpec((1, LANE), lambda h: (0, h))
    blk = pl.BlockSpec((None, LANE, LANE), lambda h: (h, 0, 0))
    return pl.pallas_call(
        kern, grid=(nh,),
        in_specs=[pl.BlockSpec((ms, LANE), lambda h: (rb, h)),
                  pl.BlockSpec((ms, LANE), lambda h: (rb, nh + h)),
                  pl.BlockSpec(memory_space=pl.ANY),
                  pl.BlockSpec((ms, nw - 1, LANE), lambda h: (0, 0, h)),
                  pl.BlockSpec((ms, LANE), lambda h: (0, h)),
                  pl.BlockSpec((nw, LANE), lambda h: (0, h)),
                  vec, blk, vec, blk, vec, vec],
        out_specs=[pl.BlockSpec((ms, LANE), lambda h: (rb, h)),
                   pl.BlockSpec((ms, nw - 1, LANE), lambda h: (0, 0, h)),
                   pl.BlockSpec((ms, LANE), lambda h: (0, h))],
        out_shape=[SDS(y_mix.shape, BF16), SDS((ms, nw - 1, d_rnn), F32), SDS((ms, d_rnn), F32)],
        input_output_aliases={2: 0},
        compiler_params=_cparams(1), name="rglru_sample")(
            z, z, y_mix, st_conv, st_h, cw, cb, wa, ba, wx, bx, lam)


def _gla_out(o, gn, g):
    return _rms(o, gn) * (g * _sigmoid(g))


def _gla_prompt(z, la, y_mix, gn, bsz, t, nh, dk, dv, col_q, col_y):
    tb = _tile(t, GLA_TB_TARGET, GLA_CHUNK)
    nt = t // tb
    nc = tb // GLA_CHUNK
    c = GLA_CHUNK
    hk, hv = nh * dk, nh * dv
    assert col_q % hk == 0 and (col_q + 2 * hk) % hv == 0 and col_y % hv == 0
    cq, ck = col_q // hk, col_q // hk + 1
    cv, cg = (col_q + 2 * hk) // hv, (col_q + 2 * hk) // hv + 1
    scale = dk ** -0.5

    def kern(q_ref, k_ref, v_ref, g_ref, la_ref, gn_ref, y_in_ref, y_ref, s_ref, st_ref):
        del y_in_ref
        tbi = pl.program_id(1)

        @pl.when(tbi == 0)
        def _():
            st_ref[...] = jnp.zeros_like(st_ref)

        row = lax.broadcasted_iota(I32, (c, dk), 0)
        causal = (lax.broadcasted_iota(I32, (c, c), 0) >= lax.broadcasted_iota(I32, (c, c), 1))

        def chunk(ci, carry):
            rows = pl.ds(pl.multiple_of(ci * c, c), c)
            for hd in range(nh):
                ks = slice(hd * dk, (hd + 1) * dk)
                vs = slice(hd * dv, (hd + 1) * dv)
                q = q_ref[rows, ks] * scale
                k = k_ref[rows, ks]
                v = v_ref[rows, vs].astype(BF16)
                bc = la_ref[rows, ks]
                s = 1
                while s < c:
                    bc = bc + _shift_rows(bc, s, row, 0.0)
                    s *= 2
                b_last = bc[c - 1:c, :]
                qe = (q * jnp.exp(bc)).astype(BF16)
                ke = (k * jnp.exp(-bc)).astype(BF16)
                kd = (k * jnp.exp(b_last - bc)).astype(BF16)
                st = st_ref[hd]
                o = lax.dot_general(qe, st.astype(BF16), (((1,), (1,)), ((), ())),
                                    preferred_element_type=F32)
                attn = lax.dot_general(qe, ke, (((1,), (1,)), ((), ())),
                                       preferred_element_type=F32)
                attn = jnp.where(causal, attn, 0.0).astype(BF16)
                o = o + _dot(attn, v)
                st_ref[hd] = st * jnp.exp(b_last) + lax.dot_general(
                    v, kd, (((0,), (0,)), ((), ())), preferred_element_type=F32)
                y_ref[rows, vs] = _gla_out(o, gn_ref[...], g_ref[rows, vs]).astype(BF16)
            return carry

        lax.fori_loop(0, nc, chunk, 0)

        @pl.when(tbi == nt - 1)
        def _():
            for hd in range(nh):
                s_ref[hd] = st_ref[hd].T

    m = z.shape[0]
    return pl.pallas_call(
        kern, grid=(bsz, nt),
        in_specs=[pl.BlockSpec((tb, hk), lambda b_, i: (b_ * nt + i, cq)),
                  pl.BlockSpec((tb, hk), lambda b_, i: (b_ * nt + i, ck)),
                  pl.BlockSpec((tb, hv), lambda b_, i: (b_ * nt + i, cv)),
                  pl.BlockSpec((tb, hv), lambda b_, i: (b_ * nt + i, cg)),
                  pl.BlockSpec((tb, hk), lambda b_, i: (b_ * nt + i, 0)),
                  pl.BlockSpec((1, dv), lambda b_, i: (0, 0)),
                  pl.BlockSpec(memory_space=pl.ANY)],
        out_specs=[pl.BlockSpec((tb, hv), lambda b_, i: (b_ * nt + i, col_y // hv)),
                   pl.BlockSpec((None, nh, dk, dv), lambda b_, i: (b_, 0, 0, 0))],
        out_shape=[SDS(y_mix.shape, BF16), SDS((bsz, nh, dk, dv), F32)],
        scratch_shapes=[pltpu.VMEM((nh, dv, dk), F32)],
        input_output_aliases={6: 0},
        compiler_params=_cparams(2), name="gla_prompt")(z, z, z, z, la, gn, y_mix)


def _gla_sample(z, la, y_mix, st, gn, mp, ms, nh, dk, dv, col_q, col_y):
    bb = 16
    assert ms % bb == 0 and mp % ms == 0 and dk == LANE
    ns = ms // bb
    hk, hv = nh * dk, nh * dv
    cq, ck = col_q // hk, col_q // hk + 1
    cv, cg = (col_q + 2 * hk) // hv, (col_q + 2 * hk) // hv + 1
    scale = dk ** -0.5

    def kern(q_ref, k_ref, la_ref, v_ref, g_ref, gn_ref, st_ref, y_in_ref, y_ref, so_ref,
             qt_ref, kt_ref, at_ref):
        del y_in_ref
        i = pl.program_id(0)

        @pl.when(i == 0)
        def _():
            for hd in range(nh):
                ks = slice(hd * dk, (hd + 1) * dk)
                qt = (q_ref[:, ks] * scale).T
                kt = k_ref[:, ks].T
                at = jnp.exp(la_ref[:, ks]).T
                for s in range(ns):
                    qt_ref[s, hd] = qt[:, s * bb:(s + 1) * bb]
                    kt_ref[s, hd] = kt[:, s * bb:(s + 1) * bb]
                    at_ref[s, hd] = at[:, s * bb:(s + 1) * bb]

        for hd in range(nh):
            vs = slice(hd * dv, (hd + 1) * dv)
            qt = qt_ref[i, hd]
            kt = kt_ref[i, hd]
            at = at_ref[i, hd]
            outs = []
            for j in range(bb):
                v = v_ref[j:j + 1, vs]
                s_new = at[:, j:j + 1] * st_ref[j, hd] + kt[:, j:j + 1] * v
                so_ref[j, hd] = s_new
                outs.append(jnp.sum(qt[:, j:j + 1] * s_new, axis=0, keepdims=True))
            o = jnp.concatenate(outs, axis=0)
            y_ref[:, vs] = _gla_out(o, gn_ref[...], g_ref[:, vs]).astype(BF16)

    rb = mp // ms
    rbb = mp // bb
    return pl.pallas_call(
        kern, grid=(ns,),
        in_specs=[pl.BlockSpec((ms, hk), lambda i: (rb, cq)),
                  pl.BlockSpec((ms, hk), lambda i: (rb, ck)),
                  pl.BlockSpec((ms, hk), lambda i: (rb, 0)),
                  pl.BlockSpec((bb, hv), lambda i: (rbb + i, cv)),
                  pl.BlockSpec((bb, hv), lambda i: (rbb + i, cg)),
                  pl.BlockSpec((1, dv), lambda i: (0, 0)),
                  pl.BlockSpec((None, bb, nh, dk, dv), lambda i: (0, i, 0, 0, 0)),
                  pl.BlockSpec(memory_space=pl.ANY)],
        out_specs=[pl.BlockSpec((bb, hv), lambda i: (rbb + i, col_y // hv)),
                   pl.BlockSpec((None, bb, nh, dk, dv), lambda i: (0, i, 0, 0, 0))],
        out_shape=[SDS(y_mix.shape, BF16), SDS((1, ms, nh, dk, dv), F32)],
        scratch_shapes=[pltpu.VMEM((ns, nh, dk, bb), F32)] * 3,
        input_output_aliases={7: 0},
        compiler_params=_cparams(1), name="gla_sample")(z, z, la, z, z, gn, st, y_mix)


def _cast_bf16(w3):
    _, k, n = w3.shape
    tk = _tile(k, 512, 16)

    def kern(w_ref, o_ref):
        o_ref[...] = w_ref[...].astype(BF16)

    return pl.pallas_call(
        kern, grid=(k // tk,),
        in_specs=[pl.BlockSpec((None, tk, n), lambda i: (0, i, 0))],
        out_specs=pl.BlockSpec((tk, n), lambda i: (i, 0)),
        out_shape=SDS((k, n), BF16),
        compiler_params=_cparams(1), name="cast_bf16")(w3)


def _route(hn, rw):
    logits = _dot(hn.astype(BF16), rw.astype(BF16))
    ne = float(logits.shape[1])
    lane = lax.broadcasted_iota(I32, logits.shape, 1).astype(F32)
    m1 = jnp.max(logits, axis=1, keepdims=True)
    i1 = jnp.min(jnp.where(logits == m1, lane, ne), axis=1, keepdims=True)
    sel1 = lane == i1
    rest = jnp.where(sel1, -jnp.inf, logits)
    m2 = jnp.max(rest, axis=1, keepdims=True)
    i2 = jnp.min(jnp.where(rest == m2, lane, ne), axis=1, keepdims=True)
    sel2 = lane == i2
    e2 = jnp.exp(m2 - m1)
    g1 = 1.0 / (1.0 + e2)
    g2 = e2 / (1.0 + e2)
    mh = jnp.where(sel1 | sel2, 1.0, 0.0)
    gd = jnp.where(sel1, g1, 0.0) + jnp.where(sel2, g2, 0.0)
    return mh, gd


def _out_proj(y, wb, res, g, tm, res_tail=None, router_w=None):
    m, k = y.shape
    d = wb.shape[1]
    n_m = m // tm
    split = res_tail is not None
    if split:
        mp, ms = res.shape[0], res_tail.shape[0]
        assert mp + ms == m and ms <= tm
        head = tm - ms
        n_rb = pl.cdiv(mp, tm)
    route = router_w is not None

    def kern(*refs):
        it = iter(refs)
        y_ref, w_ref, r_ref = next(it), next(it), next(it)
        rt_ref = next(it) if split else None
        g_ref = next(it)
        rw_ref = next(it) if route else None
        h_ref, n_ref = next(it), next(it)
        mh_ref, gd_ref = (next(it), next(it)) if route else (None, None)
        acc = _dot(y_ref[...], w_ref[...])

        def emit(rows, h):
            h_ref[rows, :] = h
            hn = _rms(h, g_ref[...])
            if route:
                n_ref[rows, :] = hn
                mh, gd = _route(hn, rw_ref[...])
                mh_ref[rows, :] = mh
                gd_ref[rows, :] = gd
            else:
                n_ref[rows, :] = hn.astype(BF16)

        if not split:
            emit(slice(0, tm), acc + r_ref[...])
        else:
            i = pl.program_id(0)

            @pl.when(i < n_m - 1)
            def _():
                emit(slice(0, tm), acc + r_ref[...])

            @pl.when(i == n_m - 1)
            def _():
                if head > 0:
                    emit(slice(0, head), acc[:head] + r_ref[0:head, :])
                emit(slice(head, tm), acc[head:] + rt_ref[...])

    in_specs = [pl.BlockSpec((tm, k), lambda i: (i, 0)),
                pl.BlockSpec((k, d), lambda i: (0, 0))]
    args = [y, wb]
    if split:
        in_specs += [pl.BlockSpec((tm, d), lambda i: (jnp.minimum(i, n_rb - 1), 0)),
                     pl.BlockSpec((ms, d), lambda i: (0, 0))]
        args += [res, res_tail]
    else:
        in_specs += [pl.BlockSpec((tm, d), lambda i: (i, 0))]
        args += [res]
    in_specs += [pl.BlockSpec((1, d), lambda i: (0, 0))]
    args += [g]
    out_specs = [pl.BlockSpec((tm, d), lambda i: (i, 0))]
    out_shape = [SDS((m, d), F32)]
    if route:
        ne = router_w.shape[1]
        in_specs += [pl.BlockSpec((d, ne), lambda i: (0, 0))]
        args += [router_w]
        out_specs += [pl.BlockSpec((tm, d), lambda i: (i, 0)),
                      pl.BlockSpec((tm, ne), lambda i: (i, 0)),
                      pl.BlockSpec((tm, ne), lambda i: (i, 0))]
        out_shape += [SDS((m, d), F32), SDS((m, ne), F32), SDS((m, ne), F32)]
    else:
        out_specs += [pl.BlockSpec((tm, d), lambda i: (i, 0))]
        out_shape += [SDS((m, d), BF16)]
    return pl.pallas_call(
        kern, grid=(n_m,), in_specs=in_specs, out_specs=out_specs, out_shape=out_shape,
        compiler_params=_cparams(1), name="out_proj_route" if route else "out_proj")(*args)


def _ffn(x, wg3, wu3, wd3, tm, tf):
    m, k = x.shape
    f = wg3.shape[2]
    d = wd3.shape[2]

    def kern(x_ref, wg_ref, wu_ref, wd_ref, o_ref):
        @pl.when(pl.program_id(1) == 0)
        def _():
            o_ref[...] = jnp.zeros_like(o_ref)

        xb = x_ref[...]
        gg = _dot(xb, wg_ref[...].astype(BF16))
        uu = _dot(xb, wu_ref[...].astype(BF16))
        a = (gg * _sigmoid(gg) * uu).astype(BF16)
        o_ref[...] += _dot(a, wd_ref[...].astype(BF16))

    return pl.pallas_call(
        kern, grid=(m // tm, f // tf),
        in_specs=[pl.BlockSpec((tm, k), lambda i, j: (i, 0)),
                  pl.BlockSpec((None, k, tf), lambda i, j: (0, 0, j)),
                  pl.BlockSpec((None, k, tf), lambda i, j: (0, 0, j)),
                  pl.BlockSpec((None, tf, d), lambda i, j: (0, j, 0))],
        out_specs=pl.BlockSpec((tm, d), lambda i, j: (i, 0)),
        out_shape=SDS((m, d), F32),
        compiler_params=_cparams(2), name="ffn")(x, wg3, wu3, wd3)


def _add_norm(a, b, g, tm):
    m, d = a.shape

    def kern(a_ref, b_ref, g_ref, h_ref, n_ref):
        h = a_ref[...] + b_ref[...]
        h_ref[...] = h
        n_ref[...] = _rms(h, g_ref[...]).astype(BF16)

    row = pl.BlockSpec((tm, d), lambda i: (i, 0))
    return pl.pallas_call(
        kern, grid=(m // tm,),
        in_specs=[row, row, pl.BlockSpec((1, d), lambda i: (0, 0))],
        out_specs=[row, row],
        out_shape=[SDS((m, d), F32), SDS((m, d), BF16)],
        compiler_params=_cparams(1), name="add_norm")(a, b, g)


def _in_proj1(x, w3, tm, tn):
    m, k = x.shape
    dc = w3.shape[2] // 3
    nb = dc // tn

    def kern(x_ref, wb_ref, wc_ref, wv_ref, gb_ref, cv_ref, sb_ref, sc_ref, sv_ref):
        @pl.when(pl.program_id(1) == 0)
        def _():
            sb_ref[...] = wb_ref[...].astype(BF16)
            sc_ref[...] = wc_ref[...].astype(BF16)
            sv_ref[...] = wv_ref[...].astype(BF16)

        xb = x_ref[...]
        gb_ref[...] = _dot(xb, sb_ref[...]).astype(BF16)
        cv_ref[...] = _dot(xb, sc_ref[...]) * _dot(xb, sv_ref[...])

    def wspec(g):
        return pl.BlockSpec((None, k, tn), lambda n, i: (0, 0, g * nb + n))

    return pl.pallas_call(
        kern, grid=(nb, m // tm),
        in_specs=[pl.BlockSpec((tm, k), lambda n, i: (i, 0)), wspec(0), wspec(1), wspec(2)],
        out_specs=[pl.BlockSpec((tm, tn), lambda n, i: (i, n))] * 2,
        out_shape=[SDS((m, dc), BF16), SDS((m, dc), F32)],
        scratch_shapes=[pltpu.VMEM((k, tn), BF16)] * 3,
        compiler_params=_cparams(2), name="in_proj1")(x, w3, w3, w3)


def _shortconv_prompt(gb, cv, cw, bsz, t):
    m, dc = cv.shape
    tc = _tile(dc, 512, LANE)
    nw = cw.shape[0]

    def kern(gb_ref, cv_ref, cw_ref, u_ref, buf_ref):
        x = cv_ref[...]
        row = lax.broadcasted_iota(I32, x.shape, 0)
        w = cw_ref[...]
        u = w[nw - 1:nw, :] * x
        for s in range(1, nw):
            u = u + w[nw - 1 - s:nw - s, :] * _shift_rows(x, s, row, 0.0)
        u_ref[...] = (gb_ref[...].astype(F32) * u).astype(BF16)
        buf_ref[...] = x[t - (nw - 1):t, :]

    blk = pl.BlockSpec((t, tc), lambda b_, c: (b_, c))
    return pl.pallas_call(
        kern, grid=(bsz, dc // tc),
        in_specs=[blk, blk, pl.BlockSpec((nw, tc), lambda b_, c: (0, c))],
        out_specs=[blk, pl.BlockSpec((None, nw - 1, tc), lambda b_, c: (b_, 0, c))],
        out_shape=[SDS((m, dc), BF16), SDS((bsz, nw - 1, dc), F32)],
        compiler_params=_cparams(2), name="shortconv_prompt")(gb, cv, cw)


def _shortconv_sample(gb, cv, u_all, st, cw, mp, ms):
    dc = cv.shape[1]
    tc = _tile(dc, 512, LANE)
    nw = cw.shape[0]
    rb = mp // ms

    def kern(gb_ref, cv_ref, u_in_ref, st_ref, cw_ref, u_ref, buf_ref):
        del u_in_ref
        x = cv_ref[...]
        w = cw_ref[...]
        u = w[nw - 1:nw, :] * x
        for s in range(nw - 1):
            u = u + w[s:s + 1, :] * st_ref[:, s, :]
        u_ref[...] = (gb_ref[...].astype(F32) * u).astype(BF16)
        for s in range(nw - 2):
            buf_ref[:, s, :] = st_ref[:, s + 1, :]
        buf_ref[:, nw - 2, :] = x

    blk = pl.BlockSpec((ms, tc), lambda c: (rb, c))
    stb = pl.BlockSpec((ms, nw - 1, tc), lambda c: (0, 0, c))
    return pl.pallas_call(
        kern, grid=(dc // tc,),
        in_specs=[blk, blk, pl.BlockSpec(memory_space=pl.ANY), stb,
                  pl.BlockSpec((nw, tc), lambda c: (0, c))],
        out_specs=[blk, stb],
        out_shape=[SDS(u_all.shape, BF16), SDS((ms, nw - 1, dc), F32)],
        input_output_aliases={2: 0},
        compiler_params=_cparams(1), name="shortconv_sample")(gb, cv, u_all, st, cw)


def _moe_rank(mh, tm):
    m, ne = mh.shape

    def kern(mh_ref, ex_ref, cnt_ref, carry_ref):
        @pl.when(pl.program_id(0) == 0)
        def _():
            carry_ref[...] = jnp.zeros_like(carry_ref)

        x = mh_ref[...]
        tri = (lax.broadcasted_iota(I32, (tm, tm), 0) > lax.broadcasted_iota(I32, (tm, tm), 1))
        ex = _dot(jnp.where(tri, 1.0, 0.0).astype(BF16), x.astype(BF16)) + carry_ref[...]
        ex_ref[...] = ex
        tot = ex[tm - 1:tm, :] + x[tm - 1:tm, :]
        carry_ref[...] = tot
        cnt_ref[...] = tot

    return pl.pallas_call(
        kern, grid=(m // tm,),
        in_specs=[pl.BlockSpec((tm, ne), lambda i: (i, 0))],
        out_specs=[pl.BlockSpec((tm, ne), lambda i: (i, 0)), pl.BlockSpec((1, ne), lambda i: (0, 0))],
        out_shape=[SDS((m, ne), F32), SDS((1, ne), F32)],
        scratch_shapes=[pltpu.VMEM((1, ne), F32)],
        compiler_params=_cparams(1), name="moe_rank")(mh)


def _moe_pos(mh, gd, ex, off, tm):
    m, ne = mh.shape

    def kern(mh_ref, gd_ref, ex_ref, off_ref, pos_ref, gate_ref):
        sel = mh_ref[...] > 0.5
        pd = ex_ref[...] + off_ref[...]
        big = jnp.float32(3e38)
        p_lo = jnp.min(jnp.where(sel, pd, big), axis=1, keepdims=True)
        p_hi = jnp.max(jnp.where(sel, pd, -big), axis=1, keepdims=True)
        gdv = gd_ref[...]
        g_lo = jnp.sum(jnp.where(sel & (pd == p_lo), gdv, 0.0), axis=1, keepdims=True)
        g_hi = jnp.sum(jnp.where(sel & (pd == p_hi), gdv, 0.0), axis=1, keepdims=True)
        pos_ref[:, 0:1] = p_lo.astype(I32)
        pos_ref[:, 1:2] = p_hi.astype(I32)
        gate_ref[:, 0:1] = g_lo
        gate_ref[:, 1:2] = g_hi

    blk = pl.BlockSpec((tm, ne), lambda i: (i, 0))
    two = pl.BlockSpec((tm, TOP_K), lambda i: (i, 0))
    return pl.pallas_call(
        kern, grid=(m // tm,),
        in_specs=[blk, blk, blk, pl.BlockSpec((1, ne), lambda i: (0, 0))],
        out_specs=[two, two],
        out_shape=[SDS((m, TOP_K), I32), SDS((m, TOP_K), F32)],
        compiler_params=_cparams(1), name="moe_pos")(mh, gd, ex, off)


def _moe_scatter(pos_flat, pad_rows, n_pad, x, n_rows, tm):
    m, w = x.shape

    def kern(pos_ref, pad_ref, npad_ref, x_ref, xs_ref, zero_ref, sem, zsem):
        i = pl.program_id(0)

        def row_copy(r, p):
            return pltpu.make_async_copy(x_ref.at[pl.ds(r, 1)], xs_ref.at[pl.ds(p, 1)], sem)

        def issue(r, c):
            t = i * tm + r
            row_copy(r, pos_ref[TOP_K * t]).start()
            row_copy(r, pos_ref[TOP_K * t + 1]).start()
            return c

        lax.fori_loop(0, tm, issue, 0, unroll=8)

        @pl.when(i == 0)
        def _():
            zero_ref[...] = jnp.zeros_like(zero_ref)
            npad = npad_ref[0]

            def zcopy(j):
                return pltpu.make_async_copy(zero_ref.at[pl.ds(0, 1)],
                                             xs_ref.at[pl.ds(pad_ref[j], 1)], zsem)

            def zissue(j, c):
                zcopy(j).start()
                return c

            def zwait(j, c):
                zcopy(j).wait()
                return c

            lax.fori_loop(0, npad, zissue, 0)
            lax.fori_loop(0, npad, zwait, 0)

        def drain(r, c):
            row_copy(0, 0).wait()
            row_copy(0, 0).wait()
            return c

        lax.fori_loop(0, tm, drain, 0, unroll=8)

    return pl.pallas_call(
        kern,
        grid_spec=pltpu.PrefetchScalarGridSpec(
            num_scalar_prefetch=3, grid=(m // tm,),
            in_specs=[pl.BlockSpec((tm, w), lambda i, *_: (i, 0))],
            out_specs=pl.BlockSpec(memory_space=pl.ANY),
            scratch_shapes=[pltpu.VMEM((8, w), F32), pltpu.SemaphoreType.DMA(()),
                            pltpu.SemaphoreType.DMA(())]),
        out_shape=SDS((n_rows, w), F32),
        compiler_params=_cparams(1), name="moe_scatter")(pos_flat, pad_rows, n_pad, x)


def _moe_ffn(st_e, st_row, st_nsub, xs, wg4, wu4, wd4, rs, sub, tf):
    n_rows, k = xs.shape
    f = wg4.shape[3]
    d = wd4.shape[3]
    ng = st_e.shape[0]
    nf = f // tf
    assert k == d and nf >= 2
    kc = _tile(k, 512, LANE)

    def kern(se_ref, sr_ref, sn_ref, xs_ref, wg_ref, wu_ref, wd_ref, ys_ref,
             x_ref, acc_ref, a_ref, wgb_ref, wub_ref, wdb_ref, sem_in, sem_out):
        g = pl.program_id(0)
        j = pl.program_id(1)
        nsub = sn_ref[g]
        row0 = sr_ref[g]

        def rows_of(s):
            return pl.ds(pl.multiple_of(s * sub, sub), sub)

        def hbm_rows(s):
            return pl.ds(pl.multiple_of(row0 + s * sub, sub), sub)

        def in_copy(s):
            return pltpu.make_async_copy(xs_ref.at[hbm_rows(s)], acc_ref.at[rows_of(s)], sem_in)

        def out_copy(s):
            return pltpu.make_async_copy(acc_ref.at[rows_of(s)], ys_ref.at[hbm_rows(s)], sem_out)

        def each(fn):
            def body(s, c):
                fn(s)
                return c
            lax.fori_loop(0, nsub, body, 0)

        def cast_weights():
            for c in range(k // kc):
                ks = slice(c * kc, (c + 1) * kc)
                wgb_ref[ks, :] = wg_ref[ks, :].astype(BF16)
                wub_ref[ks, :] = wu_ref[ks, :].astype(BF16)
            wdb_ref[...] = wd_ref[...].astype(BF16)

        def up(s):
            xb = x_ref[rows_of(s), :]
            gg = _dot(xb, wgb_ref[...])
            uu = _dot(xb, wub_ref[...])
            a_ref[s & 1] = (gg * _sigmoid(gg) * uu).astype(BF16)

        def down(s, first, last):
            dd = _dot(a_ref[s & 1], wdb_ref[...])
            if first:
                acc_ref[rows_of(s), :] = dd
            else:
                acc_ref[rows_of(s), :] += dd
            if last:
                out_copy(s).start()

        def pipeline(first, last):
            up(0)

            def body(s, c):
                up(s)
                down(s - 1, first, last)
                return c

            lax.fori_loop(1, nsub, body, 0)
            down(nsub - 1, first, last)

        @pl.when(nsub > 0)
        def _():
            @pl.when(j == 0)
            def _():
                each(lambda s: in_copy(s).start())
                cast_weights()

                def to_bf16(s):
                    x_ref[rows_of(s), :] = acc_ref[rows_of(s), :].astype(BF16)

                each(lambda s: in_copy(s).wait())
                each(to_bf16)
                pipeline(True, False)

            @pl.when((j > 0) & (j < nf - 1))
            def _():
                cast_weights()
                pipeline(False, False)

            @pl.when(j == nf - 1)
            def _():
                cast_weights()
                pipeline(False, True)
                each(lambda s: out_copy(s).wait())

    def widx(g, j, se, sr, sn):
        return (0, se[g], 0, jnp.where(sn[g] > 0, j, nf - 1))

    def didx(g, j, se, sr, sn):
        return (0, se[g], jnp.where(sn[g] > 0, j, nf - 1), 0)

    return pl.pallas_call(
        kern,
        grid_spec=pltpu.PrefetchScalarGridSpec(
            num_scalar_prefetch=3, grid=(ng, nf),
            in_specs=[pl.BlockSpec(memory_space=pl.ANY),
                      pl.BlockSpec((None, None, k, tf), widx),
                      pl.BlockSpec((None, None, k, tf), widx),
                      pl.BlockSpec((None, None, tf, d), didx)],
            out_specs=pl.BlockSpec(memory_space=pl.ANY),
            scratch_shapes=[pltpu.VMEM((rs, k), BF16), pltpu.VMEM((rs, d), F32),
                            pltpu.VMEM((2, sub, tf), BF16),
                            pltpu.VMEM((k, tf), BF16), pltpu.VMEM((k, tf), BF16),
                            pltpu.VMEM((tf, d), BF16),
                            pltpu.SemaphoreType.DMA(()), pltpu.SemaphoreType.DMA(())]),
        out_shape=SDS((n_rows, d), F32),
        compiler_params=_cparams(2), name="moe_ffn")(st_e, st_row, st_nsub, xs, wg4, wu4, wd4)


def _moe_combine(pos_flat, h, gates, ys, g, mp, ms, tp):
    m, d = h.shape
    n_p = mp // tp
    assert mp % tp == 0 and ms <= tp and ms % 8 == 0

    def kern(pos_ref, h_ref, gate_ref, ys_ref, g_ref, yp_ref, ysm_ref, a_ref, b_ref, sem):
        i = pl.program_id(0)

        def copies(r):
            t = i * tp + r
            return (pltpu.make_async_copy(ys_ref.at[pl.ds(pos_ref[TOP_K * t], 1)],
                                          a_ref.at[pl.ds(r, 1)], sem),
                    pltpu.make_async_copy(ys_ref.at[pl.ds(pos_ref[TOP_K * t + 1], 1)],
                                          b_ref.at[pl.ds(r, 1)], sem))

        def issue(r, c):
            ca, cb = copies(r)
            ca.start()
            cb.start()
            return c

        def drain(r, c):
            ca, cb = copies(0)
            ca.wait()
            cb.wait()
            return c

        def combined(n):
            lax.fori_loop(0, n, issue, 0, unroll=8)
            lax.fori_loop(0, n, drain, 0, unroll=8)
            gt = gate_ref[0:n, :]
            hh = h_ref[0:n, :] + gt[:, 0:1] * a_ref[0:n, :] + gt[:, 1:2] * b_ref[0:n, :]
            return _rms(hh, g_ref[...])

        @pl.when(i < n_p)
        def _():
            yp_ref[...] = combined(tp)

        @pl.when(i == n_p)
        def _():
            ysm_ref[...] = combined(ms)

    return pl.pallas_call(
        kern,
        grid_spec=pltpu.PrefetchScalarGridSpec(
            num_scalar_prefetch=1, grid=(n_p + 1,),
            in_specs=[pl.BlockSpec((tp, d), lambda i, *_: (i, 0)),
                      pl.BlockSpec((tp, TOP_K), lambda i, *_: (i, 0)),
                      pl.BlockSpec(memory_space=pl.ANY),
                      pl.BlockSpec((1, d), lambda i, *_: (0, 0))],
            out_specs=[pl.BlockSpec((tp, d), lambda i, *_: (jnp.minimum(i, n_p - 1), 0)),
                       pl.BlockSpec((ms, d), lambda i, *_: (0, 0))],
            scratch_shapes=[pltpu.VMEM((tp, d), F32), pltpu.VMEM((tp, d), F32),
                            pltpu.SemaphoreType.DMA(())]),
        out_shape=[SDS((mp, d), F32), SDS((ms, d), F32)],
        compiler_params=_cparams(1), name="moe_combine")(pos_flat, h, gates, ys, g)


def _moe_tables(cnt, sub, rs, ng):
    ne = cnt.shape[0]
    nsub_e = (cnt + sub - 1) // sub
    size_e = nsub_e * sub
    off = jnp.cumsum(size_e) - size_e
    spr = rs // sub
    nst_e = (nsub_e + spr - 1) // spr
    st_start = jnp.cumsum(nst_e) - nst_e
    n_act = jnp.sum(nst_e)
    gidx = jnp.arange(ng, dtype=I32)
    e_of = jnp.sum((gidx[:, None] >= st_start[None, :]).astype(I32), axis=1) - 1
    e_of = jnp.clip(e_of, 0, ne - 1)
    kth = gidx - st_start[e_of]
    active = gidx < n_act
    nsub = jnp.where(active, jnp.clip(nsub_e[e_of] - kth * spr, 0, spr), 0)
    row = off[e_of] + kth * rs
    last = jnp.maximum(n_act - 1, 0)
    st_e = jnp.where(active, e_of, e_of[last]).astype(I32)
    st_row = jnp.where(active, row, 0).astype(I32)
    n_pad_e = size_e - cnt
    pad_start = jnp.cumsum(n_pad_e) - n_pad_e
    n_pad = jnp.sum(n_pad_e)
    pidx = jnp.arange(ne * (sub - 1), dtype=I32)
    pe = jnp.clip(jnp.sum((pidx[:, None] >= pad_start[None, :]).astype(I32), axis=1) - 1, 0, ne - 1)
    pad_rows = jnp.where(pidx < n_pad, off[pe] + cnt[pe] + (pidx - pad_start[pe]), 0).astype(I32)
    return off, st_e, st_row, nsub.astype(I32), pad_rows, n_pad.astype(I32).reshape(1)


def kernel(x_prompt, x_sample, state_rg_conv, state_rg_h, state_gla, state_sc_conv, norm_mix_e, w_in_e, rg_conv_w, rg_conv_b, rg_w_a, rg_b_a, rg_w_x, rg_b_x, rg_lambda, gla_w_gate, gla_b_gate, gla_norm, w_out_e, norm_ffn_e, ffn_w_gate, ffn_w_up, ffn_w_down, norm_mix_o, w_in_o, sc_conv_w, w_out_o, norm_ffn_o, router_w, moe_w_gate, moe_w_up, moe_w_down, final_norm):
    bsz, t, d = x_prompt.shape
    ms = x_sample.shape[0]
    assert x_sample.shape[1] == 1 and w_in_e.shape[0] == 1 and w_in_o.shape[0] == 1
    mp = bsz * t
    m = mp + ms
    d_rnn = rg_lambda.shape[1]
    nh, dk, dv = state_gla.shape[2:]
    hk, hv = nh * dk, nh * dv
    n_main = 2 * d_rnn + 2 * hk + 2 * hv
    d_mix = d_rnn + hv
    ne = router_w.shape[2]
    xp = x_prompt.reshape(mp, d)
    xs = x_sample.reshape(ms, d)
    row = lambda v: v.reshape(1, -1)

    tp = _tile(mp, TP_TARGET, 16)
    tm = _tile(m, TM_TARGET, 16)

    hn0 = _norm_in(xp, xs, norm_mix_e, tp)
    z0 = _in_proj0(hn0, w_in_e, n_main, tm, _tile(n_main, 1024, LANE))
    la = _gla_gate(hn0, w_in_e[0, :, n_main:], gla_w_gate[0], gla_b_gate, tm)
    y_mix, rgc_p, rgh_p = _rglru_prompt(z0, rg_conv_w[0], rg_conv_b, rg_w_a[0], rg_b_a, rg_w_x[0],
                                        rg_b_x, rg_lambda, bsz, t, d_rnn, d_mix)
    y_mix, rgc_s, rgh_s = _rglru_sample(z0, y_mix, state_rg_conv[0], state_rg_h[0], rg_conv_w[0],
                                        rg_conv_b, rg_w_a[0], rg_b_a, rg_w_x[0], rg_b_x, rg_lambda,
                                        mp, ms, d_rnn)
    y_mix, gla_p = _gla_prompt(z0, la, y_mix, gla_norm, bsz, t, nh, dk, dv, 2 * d_rnn, d_rnn)
    y_mix, gla_s = _gla_sample(z0, la, y_mix, state_gla, gla_norm, mp, ms, nh, dk, dv,
                               2 * d_rnn, d_rnn)
    h1, hn1 = _out_proj(y_mix, _cast_bf16(w_out_e), xp, norm_ffn_e, tm, res_tail=xs)

    f0 = _ffn(hn1, ffn_w_gate, ffn_w_up, ffn_w_down, tm, _tile(ffn_w_gate.shape[2], 512, LANE))
    h2, hn2 = _add_norm(h1, f0, norm_mix_o, tm)

    gb, cv = _in_proj1(hn2, w_in_o, tm, _tile(w_in_o.shape[2] // 3, 512, LANE))
    u, sc_p = _shortconv_prompt(gb, cv, sc_conv_w[0], bsz, t)
    u, sc_s = _shortconv_sample(gb, cv, u, state_sc_conv[0], sc_conv_w[0], mp, ms)
    h3, hn3, mh, gd = _out_proj(u, _cast_bf16(w_out_o), h2, norm_ffn_o, tm, router_w=router_w[0])

    sub = MOE_SUB
    rs = MOE_SPR * sub
    ng = (TOP_K * m) // rs + ne
    n_rows = TOP_K * m + ne * (sub - 1)
    n_rows = ((n_rows + sub - 1) // sub) * sub
    ex, cnt = _moe_rank(mh, tm)
    off, st_e, st_row, st_nsub, pad_rows, n_pad = _moe_tables(cnt[0].astype(I32), sub, rs, ng)
    pos, gates = _moe_pos(mh, gd, ex, off.astype(F32).reshape(1, ne), tm)
    pos_flat = pos.reshape(TOP_K * m)
    xsort = _moe_scatter(pos_flat, pad_rows, n_pad, hn3, n_rows, tm)
    ys = _moe_ffn(st_e, st_row, st_nsub, xsort, moe_w_gate, moe_w_up, moe_w_down, rs, sub,
                  _tile(moe_w_gate.shape[3], 256, LANE))
    y_p, y_s = _moe_combine(pos_flat, h3, gates, ys, row(final_norm), mp, ms, tp)

    return (y_p.reshape(bsz, t, d), y_s.reshape(ms, 1, d),
            rgc_p[None], rgc_s[None], rgh_p.reshape(1, bsz, d_rnn), rgh_s[None],
            gla_p[None], gla_s, sc_p[None], sc_s[None])
```

```python
import functools

import jax
import jax.numpy as jnp
from jax import lax
from jax.experimental import pallas as pl
from jax.experimental.pallas import tpu as pltpu

F32 = jnp.float32
BF16 = jnp.bfloat16
I32 = jnp.int32
SDS = jax.ShapeDtypeStruct

EPS = 1e-6
RG_C = 8.0
GLA_TAU = 16.0
GLA_CHUNK = 64
TOP_K = 2
LANE = 128
SUB8 = 8
VMEM_LIMIT = 56 * 1024 * 1024
ARB = "arbitrary"
TM_TARGET = 640
TP_TARGET = 512
GLA_TB_TARGET = 512
MOE_SUB = 128
MOE_SPR = 20


def _cparams(n_axes, vmem=VMEM_LIMIT):
    return pltpu.CompilerParams(dimension_semantics=(ARB,) * n_axes, vmem_limit_bytes=vmem)


def _tile(n, target, align):
    best = None
    for t in range(align, min(n, target) + 1, align):
        if n % t == 0:
            best = t
    assert best is not None, (n, target, align)
    return best


def _rms(xf, g):
    ms = jnp.mean(xf * xf, axis=-1, keepdims=True)
    return xf * lax.rsqrt(ms + EPS) * g


def _sigmoid(x):
    return 1.0 / (1.0 + jnp.exp(-x))


def _softplus(x):
    return jnp.maximum(x, 0.0) + jnp.log1p(jnp.exp(-jnp.abs(x)))


def _gelu_tanh(x):
    c = 0.7978845608028654
    return x * (0.5 * (1.0 + jnp.tanh(c * (x + 0.044715 * (x * x * x)))))


def _dot(a, b):
    return jnp.dot(a, b, preferred_element_type=F32)


def _shift_rows(x, s, row, fill):
    return jnp.where(row >= s, pltpu.roll(x, s, 0), fill)


def _norm_in(xp, xs, g, tp):
    mp, d = xp.shape
    ms = xs.shape[0]
    n_p = mp // tp
    assert mp % tp == 0 and ms <= tp and ms % 16 == 0

    def kern(xp_ref, xs_ref, g_ref, o_ref):
        i = pl.program_id(0)

        @pl.when(i < n_p)
        def _():
            o_ref[...] = _rms(xp_ref[...], g_ref[...]).astype(BF16)

        @pl.when(i == n_p)
        def _():
            o_ref[0:ms, :] = _rms(xs_ref[...], g_ref[...]).astype(BF16)

    return pl.pallas_call(
        kern, grid=(n_p + 1,),
        in_specs=[pl.BlockSpec((tp, d), lambda i: (jnp.minimum(i, n_p - 1), 0)),
                  pl.BlockSpec((ms, d), lambda i: (0, 0)),
                  pl.BlockSpec((1, d), lambda i: (0, 0))],
        out_specs=pl.BlockSpec((tp, d), lambda i: (i, 0)),
        out_shape=SDS((mp + ms, d), BF16),
        compiler_params=_cparams(1), name="norm_in")(xp, xs, g)


def _in_proj0(x, wt3, n_cols, tm, tn):
    m, k = x.shape

    def kern(x_ref, w_ref, o_ref, wb_ref):
        @pl.when(pl.program_id(1) == 0)
        def _():
            wb_ref[...] = w_ref[...].T.astype(BF16)

        o_ref[...] = _dot(x_ref[...], wb_ref[...])

    return pl.pallas_call(
        kern, grid=(n_cols // tn, m // tm),
        in_specs=[pl.BlockSpec((tm, k), lambda n, i: (i, 0)),
                  pl.BlockSpec((None, tn, k), lambda n, i: (0, n, 0))],
        out_specs=pl.BlockSpec((tm, tn), lambda n, i: (i, n)),
        out_shape=SDS((m, n_cols), F32),
        scratch_shapes=[pltpu.VMEM((k, tn), BF16)],
        compiler_params=_cparams(2), name="in_proj0")(x, wt3)


def _gla_gate(x, wt3, col0, w_gate, b_gate, tm):
    m, k = x.shape
    r, n = w_gate.shape
    assert col0 % r == 0

    def kern(x_ref, wl_ref, wg_ref, bg_ref, o_ref):
        lr = lax.dot_general(x_ref[...], wl_ref[...].astype(BF16), (((1,), (1,)), ((), ())),
                             preferred_element_type=F32)
        pre = jnp.dot(lr, wg_ref[...], preferred_element_type=F32,
                      precision=lax.Precision.HIGHEST) + bg_ref[...]
        o_ref[...] = -_softplus(-pre) * (1.0 / GLA_TAU)

    return pl.pallas_call(
        kern, grid=(m // tm,),
        in_specs=[pl.BlockSpec((tm, k), lambda i: (i, 0)),
                  pl.BlockSpec((None, r, k), lambda i: (0, col0 // r, 0)),
                  pl.BlockSpec((r, n), lambda i: (0, 0)),
                  pl.BlockSpec((1, n), lambda i: (0, 0))],
        out_specs=pl.BlockSpec((tm, n), lambda i: (i, 0)),
        out_shape=SDS((m, n), F32),
        compiler_params=_cparams(1), name="gla_gate")(x, wt3, w_gate, b_gate)


def _rg_gates(xc, wa_ref, wx_ref, ba_ref, bx_ref, lam_ref):
    w2 = jnp.concatenate([wa_ref[...], wx_ref[...]], axis=1).astype(BF16)
    pre = _dot(xc.astype(BF16), w2)
    r = _sigmoid(pre[:, :LANE] + ba_ref[...])
    i = _sigmoid(pre[:, LANE:] + bx_ref[...])
    log_a = (-RG_C) * r * _softplus(-lam_ref[...])
    a = jnp.exp(log_a)
    mult = jnp.sqrt(jnp.tanh(-log_a) * (1.0 + a * a))
    return a, mult, i


def _rglru_prompt(z, cw, cb, wa, ba, wx, bx, lam, bsz, t, d_rnn, d_out):
    nh = d_rnn // LANE
    m = z.shape[0]

    def kern(xr_ref, gr_ref, cw_ref, cb_ref, wa_ref, ba_ref, wx_ref, bx_ref, lam_ref,
             y_ref, conv_ref, h_ref, a_s, b_s, c_s):
        xr = xr_ref[...]
        row = lax.broadcasted_iota(I32, (t, LANE), 0)
        w = cw_ref[...]
        nw = w.shape[0]
        xc = w[nw - 1:nw, :] * xr
        for s in range(1, nw):
            xc = xc + w[nw - 1 - s:nw - s, :] * _shift_rows(xr, s, row, 0.0)
        xc = xc + cb_ref[...]
        a, mult, i = _rg_gates(xc, wa_ref, wx_ref, ba_ref, bx_ref, lam_ref)
        mult = jnp.where(row == 0, 1.0, mult)
        b = mult * i * xc
        ng = t // SUB8
        a3, b3 = a.reshape(ng, SUB8, LANE), b.reshape(ng, SUB8, LANE)
        r8 = lax.broadcasted_iota(I32, (ng, SUB8, LANE), 1)
        s = 1
        while s < SUB8:
            b3 = a3 * jnp.where(r8 >= s, pltpu.roll(b3, s, 1), 0.0) + b3
            a3 = a3 * jnp.where(r8 >= s, pltpu.roll(a3, s, 1), 1.0)
            s *= 2
        a_s[...] = a3.reshape(t, LANE)
        b_s[...] = b3.reshape(t, LANE)
        ag = a_s[pl.ds(SUB8 - 1, ng, stride=SUB8), :]
        bg = b_s[pl.ds(SUB8 - 1, ng, stride=SUB8), :]
        rowg = lax.broadcasted_iota(I32, (ng, LANE), 0)
        s = 1
        while s < ng:
            bg = ag * _shift_rows(bg, s, rowg, 0.0) + bg
            ag = ag * _shift_rows(ag, s, rowg, 1.0)
            s *= 2
        carry = _shift_rows(bg, 1, rowg, 0.0)
        for k in range(SUB8):
            c_s[pl.ds(k, ng, stride=SUB8), :] = carry
        h = b_s[...] + a_s[...] * c_s[...]
        y_ref[...] = (h * _gelu_tanh(gr_ref[...])).astype(BF16)
        conv_ref[...] = xr[t - (nw - 1):t, :]
        h_ref[...] = h[t - 1:t, :]

    vec = pl.BlockSpec((1, LANE), lambda b_, h: (0, h))
    blk = pl.BlockSpec((None, LANE, LANE), lambda b_, h: (h, 0, 0))
    nw = cw.shape[0]
    return pl.pallas_call(
        kern, grid=(bsz, nh),
        in_specs=[pl.BlockSpec((t, LANE), lambda b_, h: (b_, h)),
                  pl.BlockSpec((t, LANE), lambda b_, h: (b_, nh + h)),
                  pl.BlockSpec((nw, LANE), lambda b_, h: (0, h)),
                  vec, blk, vec, blk, vec, vec],
        out_specs=[pl.BlockSpec((t, LANE), lambda b_, h: (b_, h)),
                   pl.BlockSpec((None, nw - 1, LANE), lambda b_, h: (b_, 0, h)),
                   pl.BlockSpec((None, 1, LANE), lambda b_, h: (b_, 0, h))],
        out_shape=[SDS((m, d_out), BF16), SDS((bsz, nw - 1, d_rnn), F32), SDS((bsz, 1, d_rnn), F32)],
        scratch_shapes=[pltpu.VMEM((t, LANE), F32)] * 3,
        compiler_params=_cparams(2), name="rglru_prompt")(z, z, cw, cb, wa, ba, wx, bx, lam)


def _rglru_sample(z, y_mix, st_conv, st_h, cw, cb, wa, ba, wx, bx, lam, mp, ms, d_rnn):
    nh = d_rnn // LANE
    nw = cw.shape[0]
    rb = mp // ms
    assert mp % ms == 0

    def kern(xr_ref, gr_ref, y_in_ref, sc_ref, sh_ref, cw_ref, cb_ref, wa_ref, ba_ref, wx_ref,
             bx_ref, lam_ref, y_ref, conv_ref, h_ref):
        del y_in_ref
        xr = xr_ref[...]
        w = cw_ref[...]
        xc = w[nw - 1:nw, :] * xr
        for s in range(nw - 1):
            xc = xc + w[s:s + 1, :] * sc_ref[:, s, :]
        xc = xc + cb_ref[...]
        a, mult, i = _rg_gates(xc, wa_ref, wx_ref, ba_ref, bx_ref, lam_ref)
        h = a * sh_ref[...] + mult * i * xc
        y_ref[...] = (h * _gelu_tanh(gr_ref[...])).astype(BF16)
        for s in range(nw - 2):
            conv_ref[:, s, :] = sc_ref[:, s + 1, :]
        conv_ref[:, nw - 2, :] = xr
        h_ref[...] = h

    vec = pl.BlockSpec((1, LANE), lambda h: (0, h))
    blk = pl.BlockSpec((None, LANE, LANE), lambda h: (h, 0, 0))
    return pl.pallas_call(
        kern, grid=(nh,),
        in_specs=[pl.BlockSpec((ms, LANE), lambda h: (rb, h)),
                  pl.BlockSpec((ms, LANE), lambda h: (rb, nh + h)),
                  pl.BlockSpec(memory_space=pl.ANY),
                  pl.BlockSpec((ms, nw - 1, LANE), lambda h: (0, 0, h)),
                  pl.BlockSpec((ms, LANE), lambda h: (0, h)),
                  pl.BlockSpec((nw, LANE), lambda h: (0, h)),
                  vec, blk, vec, blk, vec, vec],
        out_specs=[pl.BlockSpec((ms, LANE), lambda h: (rb, h)),
                   pl.BlockSpec((ms, nw - 1, LANE), lambda h: (0, 0, h)),
                   pl.BlockSpec((ms, LANE), lambda h: (0, h))],
        out_shape=[SDS(y_mix.shape, BF16), SDS((ms, nw - 1, d_rnn), F32), SDS((ms, d_rnn), F32)],
        input_output_aliases={2: 0},
        compiler_params=_cparams(1), name="rglru_sample")(
            z, z, y_mix, st_conv, st_h, cw, cb, wa, ba, wx, bx, lam)


def _gla_out(o, gn, g):
    return _rms(o, gn) * (g * _sigmoid(g))


def _gla_prompt(z, la, y_mix, gn, bsz, t, nh, dk, dv, col_q, col_y):
    tb = _tile(t, GLA_TB_TARGET, GLA_CHUNK)
    nt = t // tb
    nc = tb // GLA_CHUNK
    c = GLA_CHUNK
    hk, hv = nh * dk, nh * dv
    assert col_q % hk == 0 and (col_q + 2 * hk) % hv == 0 and col_y % hv == 0
    cq, ck = col_q // hk, col_q // hk + 1
    cv, cg = (col_q + 2 * hk) // hv, (col_q + 2 * hk) // hv + 1
    scale = dk ** -0.5

    def kern(q_ref, k_ref, v_ref, g_ref, la_ref, gn_ref, y_in_ref, y_ref, s_ref, st_ref):
        del y_in_ref
        tbi = pl.program_id(1)

        @pl.when(tbi == 0)
        def _():
            st_ref[...] = jnp.zeros_like(st_ref)

        row = lax.broadcasted_iota(I32, (c, dk), 0)
        causal = (lax.broadcasted_iota(I32, (c, c), 0) >= lax.broadcasted_iota(I32, (c, c), 1))

        def chunk(ci, carry):
            rows = pl.ds(pl.multiple_of(ci * c, c), c)
            for hd in range(nh):
                ks = slice(hd * dk, (hd + 1) * dk)
                vs = slice(hd * dv, (hd + 1) * dv)
                q = q_ref[rows, ks] * scale
                k = k_ref[rows, ks]
                v = v_ref[rows, vs].astype(BF16)
                bc = la_ref[rows, ks]
                s = 1
                while s < c:
                    bc = bc + _shift_rows(bc, s, row, 0.0)
                    s *= 2
                b_last = bc[c - 1:c, :]
                qe = (q * jnp.exp(bc)).astype(BF16)
                ke = (k * jnp.exp(-bc)).astype(BF16)
                kd = (k * jnp.exp(b_last - bc)).astype(BF16)
                st = st_ref[hd]
                o = lax.dot_general(qe, st.astype(BF16), (((1,), (1,)), ((), ())),
                                    preferred_element_type=F32)
                attn = lax.dot_general(qe, ke, (((1,), (1,)), ((), ())),
                                       preferred_element_type=F32)
                attn = jnp.where(causal, attn, 0.0).astype(BF16)
                o = o + _dot(attn, v)
                st_ref[hd] = st * jnp.exp(b_last) + lax.dot_general(
                    v, kd, (((0,), (0,)), ((), ())), preferred_element_type=F32)
                y_ref[rows, vs] = _gla_out(o, gn_ref[...], g_ref[rows, vs]).astype(BF16)
            return carry

        lax.fori_loop(0, nc, chunk, 0)

        @pl.when(tbi == nt - 1)
        def _():
            for hd in range(nh):
                s_ref[hd] = st_ref[hd].T

    m = z.shape[0]
    return pl.pallas_call(
        kern, grid=(bsz, nt),
        in_specs=[pl.BlockSpec((tb, hk), lambda b_, i: (b_ * nt + i, cq)),
                  pl.BlockSpec((tb, hk), lambda b_, i: (b_ * nt + i, ck)),
                  pl.BlockSpec((tb, hv), lambda b_, i: (b_ * nt + i, cv)),
                  pl.BlockSpec((tb, hv), lambda b_, i: (b_ * nt + i, cg)),
                  pl.BlockSpec((tb, hk), lambda b_, i: (b_ * nt + i, 0)),
                  pl.BlockSpec((1, dv), lambda b_, i: (0, 0)),
                  pl.BlockSpec(memory_space=pl.ANY)],
        out_specs=[pl.BlockSpec((tb, hv), lambda b_, i: (b_ * nt + i, col_y // hv)),
                   pl.BlockSpec((None, nh, dk, dv), lambda b_, i: (b_, 0, 0, 0))],
        out_shape=[SDS(y_mix.shape, BF16), SDS((bsz, nh, dk, dv), F32)],
        scratch_shapes=[pltpu.VMEM((nh, dv, dk), F32)],
        input_output_aliases={6: 0},
        compiler_params=_cparams(2), name="gla_prompt")(z, z, z, z, la, gn, y_mix)


def _gla_sample(z, la, y_mix, st, gn, mp, ms, nh, dk, dv, col_q, col_y):
    bb = 16
    assert ms % bb == 0 and mp % ms == 0 and dk == LANE
    ns = ms // bb
    hk, hv = nh * dk, nh * dv
    cq, ck = col_q // hk, col_q // hk + 1
    cv, cg = (col_q + 2 * hk) // hv, (col_q + 2 * hk) // hv + 1
    scale = dk ** -0.5

    def kern(q_ref, k_ref, la_ref, v_ref, g_ref, gn_ref, st_ref, y_in_ref, y_ref, so_ref,
             qt_ref, kt_ref, at_ref):
        del y_in_ref
        i = pl.program_id(0)

        @pl.when(i == 0)
        def _():
            for hd in range(nh):
                ks = slice(hd * dk, (hd + 1) * dk)
                qt = (q_ref[:, ks] * scale).T
                kt = k_ref[:, ks].T
                at = jnp.exp(la_ref[:, ks]).T
                for s in range(ns):
                    qt_ref[s, hd] = qt[:, s * bb:(s + 1) * bb]
                    kt_ref[s, hd] = kt[:, s * bb:(s + 1) * bb]
                    at_ref[s, hd] = at[:, s * bb:(s + 1) * bb]

        for hd in range(nh):
            vs = slice(hd * dv, (hd + 1) * dv)
            qt = qt_ref[i, hd]
            kt = kt_ref[i, hd]
            at = at_ref[i, hd]
            outs = []
            for j in range(bb):
                v = v_ref[j:j + 1, vs]
                s_new = at[:, j:j + 1] * st_ref[j, hd] + kt[:, j:j + 1] * v
                so_ref[j, hd] = s_new
                outs.append(jnp.sum(qt[:, j:j + 1] * s_new, axis=0, keepdims=True))
            o = jnp.concatenate(outs, axis=0)
            y_ref[:, vs] = _gla_out(o, gn_ref[...], g_ref[:, vs]).astype(BF16)

    rb = mp // ms
    rbb = mp // bb
    return pl.pallas_call(
        kern, grid=(ns,),
        in_specs=[pl.BlockSpec((ms, hk), lambda i: (rb, cq)),
                  pl.BlockSpec((ms, hk), lambda i: (rb, ck)),
                  pl.BlockSpec((ms, hk), lambda i: (rb, 0)),
                  pl.BlockSpec((bb, hv), lambda i: (rbb + i, cv)),
                  pl.BlockSpec((bb, hv), lambda i: (rbb + i, cg)),
                  pl.BlockSpec((1, dv), lambda i: (0, 0)),
                  pl.BlockSpec((None, bb, nh, dk, dv), lambda i: (0, i, 0, 0, 0)),
                  pl.BlockSpec(memory_space=pl.ANY)],
        out_specs=[pl.BlockSpec((bb, hv), lambda i: (rbb + i, col_y // hv)),
                   pl.BlockSpec((None, bb, nh, dk, dv), lambda i: (0, i, 0, 0, 0))],
        out_shape=[SDS(y_mix.shape, BF16), SDS((1, ms, nh, dk, dv), F32)],
        scratch_shapes=[pltpu.VMEM((ns, nh, dk, bb), F32)] * 3,
        input_output_aliases={7: 0},
        compiler_params=_cparams(1), name="gla_sample")(z, z, la, z, z, gn, st, y_mix)


def _cast_bf16(w3):
    _, k, n = w3.shape
    tk = _tile(k, 512, 16)

    def kern(w_ref, o_ref):
        o_ref[...] = w_ref[...].astype(BF16)

    return pl.pallas_call(
        kern, grid=(k // tk,),
        in_specs=[pl.BlockSpec((None, tk, n), lambda i: (0, i, 0))],
        out_specs=pl.BlockSpec((tk, n), lambda i: (i, 0)),
        out_shape=SDS((k, n), BF16),
        compiler_params=_cparams(1), name="cast_bf16")(w3)


def _route(hn, rw):
    logits = _dot(hn.astype(BF16), rw.astype(BF16))
    ne = float(logits.shape[1])
    lane = lax.broadcasted_iota(I32, logits.shape, 1).astype(F32)
    m1 = jnp.max(logits, axis=1, keepdims=True)
    i1 = jnp.min(jnp.where(logits == m1, lane, ne), axis=1, keepdims=True)
    sel1 = lane == i1
    rest = jnp.where(sel1, -jnp.inf, logits)
    m2 = jnp.max(rest, axis=1, keepdims=True)
    i2 = jnp.min(jnp.where(rest == m2, lane, ne), axis=1, keepdims=True)
    sel2 = lane == i2
    e2 = jnp.exp(m2 - m1)
    g1 = 1.0 / (1.0 + e2)
    g2 = e2 / (1.0 + e2)
    mh = jnp.where(sel1 | sel2, 1.0, 0.0)
    gd = jnp.where(sel1, g1, 0.0) + jnp.where(sel2, g2, 0.0)
    return mh, gd


def _out_proj(y, wb, res, g, tm, res_tail=None, router_w=None):
    m, k = y.shape
    d = wb.shape[1]
    n_m = m // tm
    split = res_tail is not None
    if split:
        mp, ms = res.shape[0], res_tail.shape[0]
        assert mp + ms == m and ms <= tm
        head = tm - ms
        n_rb = pl.cdiv(mp, tm)
    route = router_w is not None

    def kern(*refs):
        it = iter(refs)
        y_ref, w_ref, r_ref = next(it), next(it), next(it)
        rt_ref = next(it) if split else None
        g_ref = next(it)
        rw_ref = next(it) if route else None
        h_ref, n_ref = next(it), next(it)
        mh_ref, gd_ref = (next(it), next(it)) if route else (None, None)
        acc = _dot(y_ref[...], w_ref[...])

        def emit(rows, h):
            h_ref[rows, :] = h
            hn = _rms(h, g_ref[...])
            if route:
                n_ref[rows, :] = hn
                mh, gd = _route(hn, rw_ref[...])
                mh_ref[rows, :] = mh
                gd_ref[rows, :] = gd
            else:
                n_ref[rows, :] = hn.astype(BF16)

        if not split:
            emit(slice(0, tm), acc + r_ref[...])
        else:
            i = pl.program_id(0)

            @pl.when(i < n_m - 1)
            def _():
                emit(slice(0, tm), acc + r_ref[...])

            @pl.when(i == n_m - 1)
            def _():
                if head > 0:
                    emit(slice(0, head), acc[:head] + r_ref[0:head, :])
                emit(slice(head, tm), acc[head:] + rt_ref[...])

    in_specs = [pl.BlockSpec((tm, k), lambda i: (i, 0)),
                pl.BlockSpec((k, d), lambda i: (0, 0))]
    args = [y, wb]
    if split:
        in_specs += [pl.BlockSpec((tm, d), lambda i: (jnp.minimum(i, n_rb - 1), 0)),
                     pl.BlockSpec((ms, d), lambda i: (0, 0))]
        args += [res, res_tail]
    else:
        in_specs += [pl.BlockSpec((tm, d), lambda i: (i, 0))]
        args += [res]
    in_specs += [pl.BlockSpec((1, d), lambda i: (0, 0))]
    args += [g]
    out_specs = [pl.BlockSpec((tm, d), lambda i: (i, 0))]
    out_shape = [SDS((m, d), F32)]
    if route:
        ne = router_w.shape[1]
        in_specs += [pl.BlockSpec((d, ne), lambda i: (0, 0))]
        args += [router_w]
        out_specs += [pl.BlockSpec((tm, d), lambda i: (i, 0)),
                      pl.BlockSpec((tm, ne), lambda i: (i, 0)),
                      pl.BlockSpec((tm, ne), lambda i: (i, 0))]
        out_shape += [SDS((m, d), F32), SDS((m, ne), F32), SDS((m, ne), F32)]
    else:
        out_specs += [pl.BlockSpec((tm, d), lambda i: (i, 0))]
        out_shape += [SDS((m, d), BF16)]
    return pl.pallas_call(
        kern, grid=(n_m,), in_specs=in_specs, out_specs=out_specs, out_shape=out_shape,
        compiler_params=_cparams(1), name="out_proj_route" if route else "out_proj")(*args)


def _ffn(x, wg3, wu3, wd3, tm, tf):
    m, k = x.shape
    f = wg3.shape[2]
    d = wd3.shape[2]

    def kern(x_ref, wg_ref, wu_ref, wd_ref, o_ref):
        @pl.when(pl.program_id(1) == 0)
        def _():
            o_ref[...] = jnp.zeros_like(o_ref)

        xb = x_ref[...]
        gg = _dot(xb, wg_ref[...].astype(BF16))
        uu = _dot(xb, wu_ref[...].astype(BF16))
        a = (gg * _sigmoid(gg) * uu).astype(BF16)
        o_ref[...] += _dot(a, wd_ref[...].astype(BF16))

    return pl.pallas_call(
        kern, grid=(m // tm, f // tf),
        in_specs=[pl.BlockSpec((tm, k), lambda i, j: (i, 0)),
                  pl.BlockSpec((None, k, tf), lambda i, j: (0, 0, j)),
                  pl.BlockSpec((None, k, tf), lambda i, j: (0, 0, j)),
                  pl.BlockSpec((None, tf, d), lambda i, j: (0, j, 0))],
        out_specs=pl.BlockSpec((tm, d), lambda i, j: (i, 0)),
        out_shape=SDS((m, d), F32),
        compiler_params=_cparams(2), name="ffn")(x, wg3, wu3, wd3)


def _add_norm(a, b, g, tm):
    m, d = a.shape

    def kern(a_ref, b_ref, g_ref, h_ref, n_ref):
        h = a_ref[...] + b_ref[...]
        h_ref[...] = h
        n_ref[...] = _rms(h, g_ref[...]).astype(BF16)

    row = pl.BlockSpec((tm, d), lambda i: (i, 0))
    return pl.pallas_call(
        kern, grid=(m // tm,),
        in_specs=[row, row, pl.BlockSpec((1, d), lambda i: (0, 0))],
        out_specs=[row, row],
        out_shape=[SDS((m, d), F32), SDS((m, d), BF16)],
        compiler_params=_cparams(1), name="add_norm")(a, b, g)


def _in_proj1(x, w3, tm, tn):
    m, k = x.shape
    dc = w3.shape[2] // 3
    nb = dc // tn

    def kern(x_ref, wb_ref, wc_ref, wv_ref, gb_ref, cv_ref, sb_ref, sc_ref, sv_ref):
        @pl.when(pl.program_id(1) == 0)
        def _():
            sb_ref[...] = wb_ref[...].astype(BF16)
            sc_ref[...] = wc_ref[...].astype(BF16)
            sv_ref[...] = wv_ref[...].astype(BF16)

        xb = x_ref[...]
        gb_ref[...] = _dot(xb, sb_ref[...]).astype(BF16)
        cv_ref[...] = _dot(xb, sc_ref[...]) * _dot(xb, sv_ref[...])

    def wspec(g):
        return pl.BlockSpec((None, k, tn), lambda n, i: (0, 0, g * nb + n))

    return pl.pallas_call(
        kern, grid=(nb, m // tm),
        in_specs=[pl.BlockSpec((tm, k), lambda n, i: (i, 0)), wspec(0), wspec(1), wspec(2)],
        out_specs=[pl.BlockSpec((tm, tn), lambda n, i: (i, n))] * 2,
        out_shape=[SDS((m, dc), BF16), SDS((m, dc), F32)],
        scratch_shapes=[pltpu.VMEM((k, tn), BF16)] * 3,
        compiler_params=_cparams(2), name="in_proj1")(x, w3, w3, w3)


def _shortconv_prompt(gb, cv, cw, bsz, t):
    m, dc = cv.shape
    tc = _tile(dc, 512, LANE)
    nw = cw.shape[0]

    def kern(gb_ref, cv_ref, cw_ref, u_ref, buf_ref):
        x = cv_ref[...]
        row = lax.broadcasted_iota(I32, x.shape, 0)
        w = cw_ref[...]
        u = w[nw - 1:nw, :] * x
        for s in range(1, nw):
            u = u + w[nw - 1 - s:nw - s, :] * _shift_rows(x, s, row, 0.0)
        u_ref[...] = (gb_ref[...].astype(F32) * u).astype(BF16)
        buf_ref[...] = x[t - (nw - 1):t, :]

    blk = pl.BlockSpec((t, tc), lambda b_, c: (b_, c))
    return pl.pallas_call(
        kern, grid=(bsz, dc // tc),
        in_specs=[blk, blk, pl.BlockSpec((nw, tc), lambda b_, c: (0, c))],
        out_specs=[blk, pl.BlockSpec((None, nw - 1, tc), lambda b_, c: (b_, 0, c))],
        out_shape=[SDS((m, dc), BF16), SDS((bsz, nw - 1, dc), F32)],
        compiler_params=_cparams(2), name="shortconv_prompt")(gb, cv, cw)


def _shortconv_sample(gb, cv, u_all, st, cw, mp, ms):
    dc = cv.shape[1]
    tc = _tile(dc, 512, LANE)
    nw = cw.shape[0]
    rb = mp // ms

    def kern(gb_ref, cv_ref, u_in_ref, st_ref, cw_ref, u_ref, buf_ref):
        del u_in_ref
        x = cv_ref[...]
        w = cw_ref[...]
        u = w[nw - 1:nw, :] * x
        for s in range(nw - 1):
            u = u + w[s:s + 1, :] * st_ref[:, s, :]
        u_ref[...] = (gb_ref[...].astype(F32) * u).astype(BF16)
        for s in range(nw - 2):
            buf_ref[:, s, :] = st_ref[:, s + 1, :]
        buf_ref[:, nw - 2, :] = x

    blk = pl.BlockSpec((ms, tc), lambda c: (rb, c))
    stb = pl.BlockSpec((ms, nw - 1, tc), lambda c: (0, 0, c))
    return pl.pallas_call(
        kern, grid=(dc // tc,),
        in_specs=[blk, blk, pl.BlockSpec(memory_space=pl.ANY), stb,
                  pl.BlockSpec((nw, tc), lambda c: (0, c))],
        out_specs=[blk, stb],
        out_shape=[SDS(u_all.shape, BF16), SDS((ms, nw - 1, dc), F32)],
        input_output_aliases={2: 0},
        compiler_params=_cparams(1), name="shortconv_sample")(gb, cv, u_all, st, cw)


def _moe_rank(mh, tm):
    m, ne = mh.shape

    def kern(mh_ref, ex_ref, cnt_ref, carry_ref):
        @pl.when(pl.program_id(0) == 0)
        def _():
            carry_ref[...] = jnp.zeros_like(carry_ref)

        x = mh_ref[...]
        tri = (lax.broadcasted_iota(I32, (tm, tm), 0) > lax.broadcasted_iota(I32, (tm, tm), 1))
        ex = _dot(jnp.where(tri, 1.0, 0.0).astype(BF16), x.astype(BF16)) + carry_ref[...]
        ex_ref[...] = ex
        tot = ex[tm - 1:tm, :] + x[tm - 1:tm, :]
        carry_ref[...] = tot
        cnt_ref[...] = tot

    return pl.pallas_call(
        kern, grid=(m // tm,),
        in_specs=[pl.BlockSpec((tm, ne), lambda i: (i, 0))],
        out_specs=[pl.BlockSpec((tm, ne), lambda i: (i, 0)), pl.BlockSpec((1, ne), lambda i: (0, 0))],
        out_shape=[SDS((m, ne), F32), SDS((1, ne), F32)],
        scratch_shapes=[pltpu.VMEM((1, ne), F32)],
        compiler_params=_cparams(1), name="moe_rank")(mh)


def _moe_pos(mh, gd, ex, off, tm):
    m, ne = mh.shape

    def kern(mh_ref, gd_ref, ex_ref, off_ref, pos_ref, gate_ref):
        sel = mh_ref[...] > 0.5
        pd = ex_ref[...] + off_ref[...]
        big = jnp.float32(3e38)
        p_lo = jnp.min(jnp.where(sel, pd, big), axis=1, keepdims=True)
        p_hi = jnp.max(jnp.where(sel, pd, -big), axis=1, keepdims=True)
        gdv = gd_ref[...]
        g_lo = jnp.sum(jnp.where(sel & (pd == p_lo), gdv, 0.0), axis=1, keepdims=True)
        g_hi = jnp.sum(jnp.where(sel & (pd == p_hi), gdv, 0.0), axis=1, keepdims=True)
        pos_ref[:, 0:1] = p_lo.astype(I32)
        pos_ref[:, 1:2] = p_hi.astype(I32)
        gate_ref[:, 0:1] = g_lo
        gate_ref[:, 1:2] = g_hi

    blk = pl.BlockSpec((tm, ne), lambda i: (i, 0))
    two = pl.BlockSpec((tm, TOP_K), lambda i: (i, 0))
    return pl.pallas_call(
        kern, grid=(m // tm,),
        in_specs=[blk, blk, blk, pl.BlockSpec((1, ne), lambda i: (0, 0))],
        out_specs=[two, two],
        out_shape=[SDS((m, TOP_K), I32), SDS((m, TOP_K), F32)],
        compiler_params=_cparams(1), name="moe_pos")(mh, gd, ex, off)


def _moe_scatter(pos_flat, pad_rows, n_pad, x, n_rows, tm):
    m, w = x.shape

    def kern(pos_ref, pad_ref, npad_ref, x_ref, xs_ref, zero_ref, sem, zsem):
        i = pl.program_id(0)

        def row_copy(r, p):
            return pltpu.make_async_copy(x_ref.at[pl.ds(r, 1)], xs_ref.at[pl.ds(p, 1)], sem)

        def issue(r, c):
            t = i * tm + r
            row_copy(r, pos_ref[TOP_K * t]).start()
            row_copy(r, pos_ref[TOP_K * t + 1]).start()
            return c

        lax.fori_loop(0, tm, issue, 0, unroll=8)

        @pl.when(i == 0)
        def _():
            zero_ref[...] = jnp.zeros_like(zero_ref)
            npad = npad_ref[0]

            def zcopy(j):
                return pltpu.make_async_copy(zero_ref.at[pl.ds(0, 1)],
                                             xs_ref.at[pl.ds(pad_ref[j], 1)], zsem)

            def zissue(j, c):
                zcopy(j).start()
                return c

            def zwait(j, c):
                zcopy(j).wait()
                return c

            lax.fori_loop(0, npad, zissue, 0)
            lax.fori_loop(0, npad, zwait, 0)

        def drain(r, c):
            row_copy(0, 0).wait()
            row_copy(0, 0).wait()
            return c

        lax.fori_loop(0, tm, drain, 0, unroll=8)

    return pl.pallas_call(
        kern,
        grid_spec=pltpu.PrefetchScalarGridSpec(
            num_scalar_prefetch=3, grid=(m // tm,),
            in_specs=[pl.BlockSpec((tm, w), lambda i, *_: (i, 0))],
            out_specs=pl.BlockSpec(memory_space=pl.ANY),
            scratch_shapes=[pltpu.VMEM((8, w), F32), pltpu.SemaphoreType.DMA(()),
                            pltpu.SemaphoreType.DMA(())]),
        out_shape=SDS((n_rows, w), F32),
        compiler_params=_cparams(1), name="moe_scatter")(pos_flat, pad_rows, n_pad, x)


def _moe_ffn(st_e, st_row, st_nsub, xs, wg4, wu4, wd4, rs, sub, tf):
    n_rows, k = xs.shape
    f = wg4.shape[3]
    d = wd4.shape[3]
    ng = st_e.shape[0]
    nf = f // tf
    assert k == d and nf >= 2
    kc = _tile(k, 512, LANE)
    big, mid = 4 * sub, 2 * sub

    def kern(se_ref, sr_ref, sn_ref, xs_ref, wg_ref, wu_ref, wd_ref, ys_ref,
             x_ref, acc_ref, a_ref, wgb_ref, wub_ref, wdb_ref, sem_in, sem_out):
        g = pl.program_id(0)
        j = pl.program_id(1)
        nsub = sn_ref[g]
        row0 = sr_ref[g]
        nbig = lax.shift_right_logical(nsub, 2)
        has_mid = (nsub & 2) != 0
        has_small = (nsub & 1) != 0
        start_mid = nbig * big
        start_small = start_mid + jnp.where(has_mid, mid, 0)

        def rows(start, size):
            return pl.ds(pl.multiple_of(start, sub), size)

        def in_copy(s):
            return pltpu.make_async_copy(xs_ref.at[rows(row0 + s * sub, sub)],
                                         acc_ref.at[rows(s * sub, sub)], sem_in)

        def out_copy(start, size):
            return pltpu.make_async_copy(acc_ref.at[rows(start, size)],
                                         ys_ref.at[rows(row0 + start, size)], sem_out)

        def each_sub(fn):
            def body(s, c):
                fn(s)
                return c
            lax.fori_loop(0, nsub, body, 0)

        def cast_weights():
            for c in range(k // kc):
                ks = slice(c * kc, (c + 1) * kc)
                wgb_ref[ks, :] = wg_ref[ks, :].astype(BF16)
                wub_ref[ks, :] = wu_ref[ks, :].astype(BF16)
            wdb_ref[...] = wd_ref[...].astype(BF16)

        def up(start, size, slot):
            xb = x_ref[rows(start, size), :]
            gg = _dot(xb, wgb_ref[...])
            uu = _dot(xb, wub_ref[...])
            a_ref[slot, 0:size, :] = (gg * _sigmoid(gg) * uu).astype(BF16)

        def down(start, size, slot, first, last):
            dd = _dot(a_ref[slot, 0:size, :], wdb_ref[...])
            if first:
                acc_ref[rows(start, size), :] = dd
            else:
                acc_ref[rows(start, size), :] += dd
            if last:
                out_copy(start, size).start()

        def compute(first, last, cast):
            @pl.when(nbig > 0)
            def _():
                if cast:
                    cast_weights()
                up(0, big, 0)

                def body(s, c):
                    up(s * big, big, s & 1)
                    down((s - 1) * big, big, (s - 1) & 1, first, last)
                    return c

                lax.fori_loop(1, nbig, body, 0)
                down((nbig - 1) * big, big, (nbig - 1) & 1, first, last)

            if cast:
                @pl.when(nbig == 0)
                def _():
                    cast_weights()

            @pl.when(has_mid)
            def _():
                up(start_mid, mid, 0)
                down(start_mid, mid, 0, first, last)

            @pl.when(has_small)
            def _():
                up(start_small, sub, 0)
                down(start_small, sub, 0, first, last)

        @pl.when(nsub > 0)
        def _():
            @pl.when(j == 0)
            def _():
                each_sub(lambda s: in_copy(s).start())
                cast_weights()

                def to_bf16(s):
                    x_ref[rows(s * sub, sub), :] = acc_ref[rows(s * sub, sub), :].astype(BF16)

                each_sub(lambda s: in_copy(s).wait())
                each_sub(to_bf16)
                compute(True, False, False)

            @pl.when((j > 0) & (j < nf - 1))
            def _():
                compute(False, False, True)

            @pl.when(j == nf - 1)
            def _():
                compute(False, True, True)

                def wait_big(s, c):
                    out_copy(0, big).wait()
                    return c

                lax.fori_loop(0, nbig, wait_big, 0)

                @pl.when(has_mid)
                def _():
                    out_copy(0, mid).wait()

                @pl.when(has_small)
                def _():
                    out_copy(0, sub).wait()

    def widx(g, j, se, sr, sn):
        return (0, se[g], 0, jnp.where(sn[g] > 0, j, nf - 1))

    def didx(g, j, se, sr, sn):
        return (0, se[g], jnp.where(sn[g] > 0, j, nf - 1), 0)

    return pl.pallas_call(
        kern,
        grid_spec=pltpu.PrefetchScalarGridSpec(
            num_scalar_prefetch=3, grid=(ng, nf),
            in_specs=[pl.BlockSpec(memory_space=pl.ANY),
                      pl.BlockSpec((None, None, k, tf), widx),
                      pl.BlockSpec((None, None, k, tf), widx),
                      pl.BlockSpec((None, None, tf, d), didx)],
            out_specs=pl.BlockSpec(memory_space=pl.ANY),
            scratch_shapes=[pltpu.VMEM((rs, k), BF16), pltpu.VMEM((rs, d), F32),
                            pltpu.VMEM((2, 4 * sub, tf), BF16),
                            pltpu.VMEM((k, tf), BF16), pltpu.VMEM((k, tf), BF16),
                            pltpu.VMEM((tf, d), BF16),
                            pltpu.SemaphoreType.DMA(()), pltpu.SemaphoreType.DMA(())]),
        out_shape=SDS((n_rows, d), F32),
        compiler_params=_cparams(2), name="moe_ffn")(st_e, st_row, st_nsub, xs, wg4, wu4, wd4)


def _moe_combine(pos_flat, h, gates, ys, g, mp, ms, tp):
    m, d = h.shape
    n_p = mp // tp
    assert mp % tp == 0 and ms <= tp and ms % 8 == 0

    def kern(pos_ref, h_ref, gate_ref, ys_ref, g_ref, yp_ref, ysm_ref, a_ref, b_ref, sem):
        i = pl.program_id(0)

        def copies(r):
            t = i * tp + r
            return (pltpu.make_async_copy(ys_ref.at[pl.ds(pos_ref[TOP_K * t], 1)],
                                          a_ref.at[pl.ds(r, 1)], sem),
                    pltpu.make_async_copy(ys_ref.at[pl.ds(pos_ref[TOP_K * t + 1], 1)],
                                          b_ref.at[pl.ds(r, 1)], sem))

        def issue(r, c):
            ca, cb = copies(r)
            ca.start()
            cb.start()
            return c

        def drain(r, c):
            ca, cb = copies(0)
            ca.wait()
            cb.wait()
            return c

        def combined(n):
            lax.fori_loop(0, n, issue, 0, unroll=8)
            lax.fori_loop(0, n, drain, 0, unroll=8)
            gt = gate_ref[0:n, :]
            hh = h_ref[0:n, :] + gt[:, 0:1] * a_ref[0:n, :] + gt[:, 1:2] * b_ref[0:n, :]
            return _rms(hh, g_ref[...])

        @pl.when(i < n_p)
        def _():
            yp_ref[...] = combined(tp)

        @pl.when(i == n_p)
        def _():
            ysm_ref[...] = combined(ms)

    return pl.pallas_call(
        kern,
        grid_spec=pltpu.PrefetchScalarGridSpec(
            num_scalar_prefetch=1, grid=(n_p + 1,),
            in_specs=[pl.BlockSpec((tp, d), lambda i, *_: (i, 0)),
                      pl.BlockSpec((tp, TOP_K), lambda i, *_: (i, 0)),
                      pl.BlockSpec(memory_space=pl.ANY),
                      pl.BlockSpec((1, d), lambda i, *_: (0, 0))],
            out_specs=[pl.BlockSpec((tp, d), lambda i, *_: (jnp.minimum(i, n_p - 1), 0)),
                       pl.BlockSpec((ms, d), lambda i, *_: (0, 0))],
            scratch_shapes=[pltpu.VMEM((tp, d), F32), pltpu.VMEM((tp, d), F32),
                            pltpu.SemaphoreType.DMA(())]),
        out_shape=[SDS((mp, d), F32), SDS((ms, d), F32)],
        compiler_params=_cparams(1), name="moe_combine")(pos_flat, h, gates, ys, g)


def _moe_tables(cnt, sub, rs, ng):
    ne = cnt.shape[0]
    nsub_e = (cnt + sub - 1) // sub
    size_e = nsub_e * sub
    off = jnp.cumsum(size_e) - size_e
    spr = rs // sub
    nst_e = (nsub_e + spr - 1) // spr
    st_start = jnp.cumsum(nst_e) - nst_e
    n_act = jnp.sum(nst_e)
    gidx = jnp.arange(ng, dtype=I32)
    e_of = jnp.sum((gidx[:, None] >= st_start[None, :]).astype(I32), axis=1) - 1
    e_of = jnp.clip(e_of, 0, ne - 1)
    kth = gidx - st_start[e_of]
    active = gidx < n_act
    nsub = jnp.where(active, jnp.clip(nsub_e[e_of] - kth * spr, 0, spr), 0)
    row = off[e_of] + kth * rs
    last = jnp.maximum(n_act - 1, 0)
    st_e = jnp.where(active, e_of, e_of[last]).astype(I32)
    st_row = jnp.where(active, row, 0).astype(I32)
    n_pad_e = size_e - cnt
    pad_start = jnp.cumsum(n_pad_e) - n_pad_e
    n_pad = jnp.sum(n_pad_e)
    pidx = jnp.arange(ne * (sub - 1), dtype=I32)
    pe = jnp.clip(jnp.sum((pidx[:, None] >= pad_start[None, :]).astype(I32), axis=1) - 1, 0, ne - 1)
    pad_rows = jnp.where(pidx < n_pad, off[pe] + cnt[pe] + (pidx - pad_start[pe]), 0).astype(I32)
    return off, st_e, st_row, nsub.astype(I32), pad_rows, n_pad.astype(I32).reshape(1)


def kernel(x_prompt, x_sample, state_rg_conv, state_rg_h, state_gla, state_sc_conv, norm_mix_e, w_in_e, rg_conv_w, rg_conv_b, rg_w_a, rg_b_a, rg_w_x, rg_b_x, rg_lambda, gla_w_gate, gla_b_gate, gla_norm, w_out_e, norm_ffn_e, ffn_w_gate, ffn_w_up, ffn_w_down, norm_mix_o, w_in_o, sc_conv_w, w_out_o, norm_ffn_o, router_w, moe_w_gate, moe_w_up, moe_w_down, final_norm):
    bsz, t, d = x_prompt.shape
    ms = x_sample.shape[0]
    assert x_sample.shape[1] == 1 and w_in_e.shape[0] == 1 and w_in_o.shape[0] == 1
    mp = bsz * t
    m = mp + ms
    d_rnn = rg_lambda.shape[1]
    nh, dk, dv = state_gla.shape[2:]
    hk, hv = nh * dk, nh * dv
    n_main = 2 * d_rnn + 2 * hk + 2 * hv
    d_mix = d_rnn + hv
    ne = router_w.shape[2]
    xp = x_prompt.reshape(mp, d)
    xs = x_sample.reshape(ms, d)
    row = lambda v: v.reshape(1, -1)

    tp = _tile(mp, TP_TARGET, 16)
    tm = _tile(m, TM_TARGET, 16)

    hn0 = _norm_in(xp, xs, norm_mix_e, tp)
    w_in_t = jnp.swapaxes(w_in_e, 1, 2)
    z0 = _in_proj0(hn0, w_in_t, n_main, tm, _tile(n_main, 1024, LANE))
    la = _gla_gate(hn0, w_in_t, n_main, gla_w_gate[0], gla_b_gate, tm)
    y_mix, rgc_p, rgh_p = _rglru_prompt(z0, rg_conv_w[0], rg_conv_b, rg_w_a[0], rg_b_a, rg_w_x[0],
                                        rg_b_x, rg_lambda, bsz, t, d_rnn, d_mix)
    y_mix, rgc_s, rgh_s = _rglru_sample(z0, y_mix, state_rg_conv[0], state_rg_h[0], rg_conv_w[0],
                                        rg_conv_b, rg_w_a[0], rg_b_a, rg_w_x[0], rg_b_x, rg_lambda,
                                        mp, ms, d_rnn)
    y_mix, gla_p = _gla_prompt(z0, la, y_mix, gla_norm, bsz, t, nh, dk, dv, 2 * d_rnn, d_rnn)
    y_mix, gla_s = _gla_sample(z0, la, y_mix, state_gla, gla_norm, mp, ms, nh, dk, dv,
                               2 * d_rnn, d_rnn)
    h1, hn1 = _out_proj(y_mix, _cast_bf16(w_out_e), xp, norm_ffn_e, tm, res_tail=xs)

    f0 = _ffn(hn1, ffn_w_gate, ffn_w_up, ffn_w_down, tm, _tile(ffn_w_gate.shape[2], 512, LANE))
    h2, hn2 = _add_norm(h1, f0, norm_mix_o, tm)

    gb, cv = _in_proj1(hn2, w_in_o, tm, _tile(w_in_o.shape[2] // 3, 512, LANE))
    u, sc_p = _shortconv_prompt(gb, cv, sc_conv_w[0], bsz, t)
    u, sc_s = _shortconv_sample(gb, cv, u, state_sc_conv[0], sc_conv_w[0], mp, ms)
    h3, hn3, mh, gd = _out_proj(u, _cast_bf16(w_out_o), h2, norm_ffn_o, tm, router_w=router_w[0])

    sub = MOE_SUB
    rs = MOE_SPR * sub
    ng = (TOP_K * m) // rs + ne
    n_rows = TOP_K * m + ne * (sub - 1)
    n_rows = ((n_rows + sub - 1) // sub) * sub
    ex, cnt = _moe_rank(mh, tm)
    off, st_e, st_row, st_nsub, pad_rows, n_pad = _moe_tables(cnt[0].astype(I32), sub, rs, ng)
    pos, gates = _moe_pos(mh, gd, ex, off.astype(F32).reshape(1, ne), tm)
    pos_flat = pos.reshape(TOP_K * m)
    xsort = _moe_scatter(pos_flat, pad_rows, n_pad, hn3, n_rows, tm)
    ys = _moe_ffn(st_e, st_row, st_nsub, xsort, moe_w_gate, moe_w_up, moe_w_down, rs, sub,
                  _tile(moe_w_gate.shape[3], 256, LANE))
    y_p, y_s = _moe_combine(pos_flat, h3, gates, ys, row(final_norm), mp, ms, tp)

    return (y_p.reshape(bsz, t, d), y_s.reshape(ms, 1, d),
            rgc_p[None], rgc_s[None], rgh_p.reshape(1, bsz, d_rnn), rgh_s[None],
            gla_p[None], gla_s, sc_p[None], sc_s[None])
```

```python
import functools

import jax
import jax.numpy as jnp
from jax import lax
from jax.experimental import pallas as pl
from jax.experimental.pallas import tpu as pltpu

F32 = jnp.float32
BF16 = jnp.bfloat16
I32 = jnp.int32
SDS = jax.ShapeDtypeStruct

EPS = 1e-6
RG_C = 8.0
GLA_TAU = 16.0
GLA_CHUNK = 64
TOP_K = 2
LANE = 128
SUB8 = 8
VMEM_LIMIT = 56 * 1024 * 1024
ARB = "arbitrary"
TM_TARGET = 640
TM_IN_TARGET = 1664
TM_OUT_TARGET = 320
TP_TARGET = 512
GLA_TB_TARGET = 512
MOE_SUB = 128
MOE_SPR = 20


def _cparams(n_axes, vmem=VMEM_LIMIT):
    return pltpu.CompilerParams(dimension_semantics=(ARB,) * n_axes, vmem_limit_bytes=vmem)


def _tile(n, target, align):
    best = None
    for t in range(align, min(n, target) + 1, align):
        if n % t == 0:
            best = t
    assert best is not None, (n, target, align)
    return best


def _rms(xf, g):
    ms = jnp.mean(xf * xf, axis=-1, keepdims=True)
    return xf * lax.rsqrt(ms + EPS) * g


def _sigmoid(x):
    return 1.0 / (1.0 + jnp.exp(-x))


def _softplus(x):
    return jnp.maximum(x, 0.0) + jnp.log1p(jnp.exp(-jnp.abs(x)))


def _gelu_tanh(x):
    c = 0.7978845608028654
    return x * (0.5 * (1.0 + jnp.tanh(c * (x + 0.044715 * (x * x * x)))))


def _dot(a, b):
    return jnp.dot(a, b, preferred_element_type=F32)


def _shift_rows(x, s, row, fill):
    return jnp.where(row >= s, pltpu.roll(x, s, 0), fill)


def _norm_in(xp, xs, g, tp):
    mp, d = xp.shape
    ms = xs.shape[0]
    n_p = mp // tp
    assert mp % tp == 0 and ms <= tp and ms % 16 == 0

    def kern(xp_ref, xs_ref, g_ref, o_ref):
        i = pl.program_id(0)

        @pl.when(i < n_p)
        def _():
            o_ref[...] = _rms(xp_ref[...], g_ref[...]).astype(BF16)

        @pl.when(i == n_p)
        def _():
            o_ref[0:ms, :] = _rms(xs_ref[...], g_ref[...]).astype(BF16)

    return pl.pallas_call(
        kern, grid=(n_p + 1,),
        in_specs=[pl.BlockSpec((tp, d), lambda i: (jnp.minimum(i, n_p - 1), 0)),
                  pl.BlockSpec((ms, d), lambda i: (0, 0)),
                  pl.BlockSpec((1, d), lambda i: (0, 0))],
        out_specs=pl.BlockSpec((tp, d), lambda i: (i, 0)),
        out_shape=SDS((mp + ms, d), BF16),
        compiler_params=_cparams(1), name="norm_in")(xp, xs, g)


def _in_proj0(x, wt3, n_cols, tm, tn):
    m, k = x.shape

    def kern(x_ref, w_ref, o_ref, wb_ref):
        @pl.when(pl.program_id(1) == 0)
        def _():
            wb_ref[...] = w_ref[...].T.astype(BF16)

        o_ref[...] = _dot(x_ref[...], wb_ref[...])

    return pl.pallas_call(
        kern, grid=(n_cols // tn, m // tm),
        in_specs=[pl.BlockSpec((tm, k), lambda n, i: (i, 0)),
                  pl.BlockSpec((None, tn, k), lambda n, i: (0, n, 0))],
        out_specs=pl.BlockSpec((tm, tn), lambda n, i: (i, n)),
        out_shape=SDS((m, n_cols), F32),
        scratch_shapes=[pltpu.VMEM((k, tn), BF16)],
        compiler_params=_cparams(2), name="in_proj0")(x, wt3)


def _gla_gate(x, wt3, col0, w_gate, b_gate, tm):
    m, k = x.shape
    r, n = w_gate.shape
    assert col0 % r == 0

    def kern(x_ref, wl_ref, wg_ref, bg_ref, o_ref):
        lr = lax.dot_general(x_ref[...], wl_ref[...].astype(BF16), (((1,), (1,)), ((), ())),
                             preferred_element_type=F32)
        pre = jnp.dot(lr, wg_ref[...], preferred_element_type=F32,
                      precision=lax.Precision.HIGHEST) + bg_ref[...]
        o_ref[...] = -_softplus(-pre) * (1.0 / GLA_TAU)

    return pl.pallas_call(
        kern, grid=(m // tm,),
        in_specs=[pl.BlockSpec((tm, k), lambda i: (i, 0)),
                  pl.BlockSpec((None, r, k), lambda i: (0, col0 // r, 0)),
                  pl.BlockSpec((r, n), lambda i: (0, 0)),
                  pl.BlockSpec((1, n), lambda i: (0, 0))],
        out_specs=pl.BlockSpec((tm, n), lambda i: (i, 0)),
        out_shape=SDS((m, n), F32),
        compiler_params=_cparams(1), name="gla_gate")(x, wt3, w_gate, b_gate)


def _rg_gates(xc, wa_ref, wx_ref, ba_ref, bx_ref, lam_ref):
    w2 = jnp.concatenate([wa_ref[...], wx_ref[...]], axis=1).astype(BF16)
    pre = _dot(xc.astype(BF16), w2)
    r = _sigmoid(pre[:, :LANE] + ba_ref[...])
    i = _sigmoid(pre[:, LANE:] + bx_ref[...])
    log_a = (-RG_C) * r * _softplus(-lam_ref[...])
    a = jnp.exp(log_a)
    mult = jnp.sqrt(jnp.tanh(-log_a) * (1.0 + a * a))
    return a, mult, i


def _rglru_prompt(z, cw, cb, wa, ba, wx, bx, lam, bsz, t, d_rnn, d_out):
    nh = d_rnn // LANE
    m = z.shape[0]

    def kern(xr_ref, gr_ref, cw_ref, cb_ref, wa_ref, ba_ref, wx_ref, bx_ref, lam_ref,
             y_ref, conv_ref, h_ref, a_s, b_s, c_s):
        xr = xr_ref[...]
        row = lax.broadcasted_iota(I32, (t, LANE), 0)
        w = cw_ref[...]
        nw = w.shape[0]
        xc = w[nw - 1:nw, :] * xr
        for s in range(1, nw):
            xc = xc + w[nw - 1 - s:nw - s, :] * _shift_rows(xr, s, row, 0.0)
        xc = xc + cb_ref[...]
        a, mult, i = _rg_gates(xc, wa_ref, wx_ref, ba_ref, bx_ref, lam_ref)
        mult = jnp.where(row == 0, 1.0, mult)
        b = mult * i * xc
        ng = t // SUB8
        a3, b3 = a.reshape(ng, SUB8, LANE), b.reshape(ng, SUB8, LANE)
        r8 = lax.broadcasted_iota(I32, (ng, SUB8, LANE), 1)
        s = 1
        while s < SUB8:
            b3 = a3 * jnp.where(r8 >= s, pltpu.roll(b3, s, 1), 0.0) + b3
            a3 = a3 * jnp.where(r8 >= s, pltpu.roll(a3, s, 1), 1.0)
            s *= 2
        a_s[...] = a3.reshape(t, LANE)
        b_s[...] = b3.reshape(t, LANE)
        ag = a_s[pl.ds(SUB8 - 1, ng, stride=SUB8), :]
        bg = b_s[pl.ds(SUB8 - 1, ng, stride=SUB8), :]
        rowg = lax.broadcasted_iota(I32, (ng, LANE), 0)
        s = 1
        while s < ng:
            bg = ag * _shift_rows(bg, s, rowg, 0.0) + bg
            ag = ag * _shift_rows(ag, s, rowg, 1.0)
            s *= 2
        carry = _shift_rows(bg, 1, rowg, 0.0)
        for k in range(SUB8):
            c_s[pl.ds(k, ng, stride=SUB8), :] = carry
        h = b_s[...] + a_s[...] * c_s[...]
        y_ref[...] = (h * _gelu_tanh(gr_ref[...])).astype(BF16)
        conv_ref[...] = xr[t - (nw - 1):t, :]
        h_ref[...] = h[t - 1:t, :]

    vec = pl.BlockSpec((1, LANE), lambda b_, h: (0, h))
    blk = pl.BlockSpec((None, LANE, LANE), lambda b_, h: (h, 0, 0))
    nw = cw.shape[0]
    return pl.pallas_call(
        kern, grid=(bsz, nh),
        in_specs=[pl.BlockSpec((t, LANE), lambda b_, h: (b_, h)),
                  pl.BlockSpec((t, LANE), lambda b_, h: (b_, nh + h)),
                  pl.BlockSpec((nw, LANE), lambda b_, h: (0, h)),
                  vec, blk, vec, blk, vec, vec],
        out_specs=[pl.BlockSpec((t, LANE), lambda b_, h: (b_, h)),
                   pl.BlockSpec((None, nw - 1, LANE), lambda b_, h: (b_, 0, h)),
                   pl.BlockSpec((None, 1, LANE), lambda b_, h: (b_, 0, h))],
        out_shape=[SDS((m, d_out), BF16), SDS((bsz, nw - 1, d_rnn), F32), SDS((bsz, 1, d_rnn), F32)],
        scratch_shapes=[pltpu.VMEM((t, LANE), F32)] * 3,
        compiler_params=_cparams(2), name="rglru_prompt")(z, z, cw, cb, wa, ba, wx, bx, lam)


def _rglru_sample(z, y_mix, st_conv, st_h, cw, cb, wa, ba, wx, bx, lam, mp, ms, d_rnn):
    nh = d_rnn // LANE
    nw = cw.shape[0]
    rb = mp // ms
    assert mp % ms == 0

    def kern(xr_ref, gr_ref, y_in_ref, sc_ref, sh_ref, cw_ref, cb_ref, wa_ref, ba_ref, wx_ref,
             bx_ref, lam_ref, y_ref, conv_ref, h_ref):
        del y_in_ref
        xr = xr_ref[...]
        w = cw_ref[...]
        xc = w[nw - 1:nw, :] * xr
        for s in range(nw - 1):
            xc = xc + w[s:s + 1, :] * sc_ref[:, s, :]
        xc = xc + cb_ref[...]
        a, mult, i = _rg_gates(xc, wa_ref, wx_ref, ba_ref, bx_ref, lam_ref)
        h = a * sh_ref[...] + mult * i * xc
        y_ref[...] = (h * _gelu_tanh(gr_ref[...])).astype(BF16)
        for s in range(nw - 2):
            conv_ref[:, s, :] = sc_ref[:, s + 1, :]
        conv_ref[:, nw - 2, :] = xr
        h_ref[...] = h

    vec = pl.BlockSpec((1, LANE), lambda h: (0, h))
    blk = pl.BlockSpec((None, LANE, LANE), lambda h: (h, 0, 0))
    return pl.pallas_call(
        kern, grid=(nh,),
        in_specs=[pl.BlockSpec((ms, LANE), lambda h: (rb, h)),
                  pl.BlockSpec((ms, LANE), lambda h: (rb, nh + h)),
                  pl.BlockSpec(memory_space=pl.ANY),
                  pl.BlockSpec((ms, nw - 1, LANE), lambda h: (0, 0, h)),
                  pl.BlockSpec((ms, LANE), lambda h: (0, h)),
                  pl.BlockSpec((nw, LANE), lambda h: (0, h)),
                  vec, blk, vec, blk, vec, vec],
        out_specs=[pl.BlockSpec((ms, LANE), lambda h: (rb, h)),
                   pl.BlockSpec((ms, nw - 1, LANE), lambda h: (0, 0, h)),
                   pl.BlockSpec((ms, LANE), lambda h: (0, h))],
        out_shape=[SDS(y_mix.shape, BF16), SDS((ms, nw - 1, d_rnn), F32), SDS((ms, d_rnn), F32)],
        input_output_aliases={2: 0},
        compiler_params=_cparams(1), name="rglru_sample")(
            z, z, y_mix, st_conv, st_h, cw, cb, wa, ba, wx, bx, lam)


def _gla_out(o, gn, g):
    return _rms(o, gn) * (g * _sigmoid(g))


def _gla_prompt(z, la, y_mix, gn, bsz, t, nh, dk, dv, col_q, col_y):
    tb = _tile(t, GLA_TB_TARGET, GLA_CHUNK)
    nt = t // tb
    nc = tb // GLA_CHUNK
    c = GLA_CHUNK
    hk, hv = nh * dk, nh * dv
    assert col_q % hk == 0 and (col_q + 2 * hk) % hv == 0 and col_y % hv == 0
    cq, ck = col_q // hk, col_q // hk + 1
    cv, cg = (col_q + 2 * hk) // hv, (col_q + 2 * hk) // hv + 1
    scale = dk ** -0.5

    def kern(q_ref, k_ref, v_ref, g_ref, la_ref, gn_ref, y_in_ref, y_ref, s_ref, st_ref):
        del y_in_ref
        tbi = pl.program_id(1)

        @pl.when(tbi == 0)
        def _():
            st_ref[...] = jnp.zeros_like(st_ref)

        row = lax.broadcasted_iota(I32, (c, dk), 0)
        causal = (lax.broadcasted_iota(I32, (c, c), 0) >= lax.broadcasted_iota(I32, (c, c), 1))

        def chunk(ci, carry):
            rows = pl.ds(pl.multiple_of(ci * c, c), c)
            for hd in range(nh):
                ks = slice(hd * dk, (hd + 1) * dk)
                vs = slice(hd * dv, (hd + 1) * dv)
                q = q_ref[rows, ks] * scale
                k = k_ref[rows, ks]
                v = v_ref[rows, vs].astype(BF16)
                bc = la_ref[rows, ks]
                s = 1
                while s < c:
                    bc = bc + _shift_rows(bc, s, row, 0.0)
                    s *= 2
                b_last = bc[c - 1:c, :]
                qe = (q * jnp.exp(bc)).astype(BF16)
                ke = (k * jnp.exp(-bc)).astype(BF16)
                kd = (k * jnp.exp(b_last - bc)).astype(BF16)
                st = st_ref[hd]
                o = lax.dot_general(qe, st.astype(BF16), (((1,), (1,)), ((), ())),
                                    preferred_element_type=F32)
                attn = lax.dot_general(qe, ke, (((1,), (1,)), ((), ())),
                                       preferred_element_type=F32)
                attn = jnp.where(causal, attn, 0.0).astype(BF16)
                o = o + _dot(attn, v)
                st_ref[hd] = st * jnp.exp(b_last) + lax.dot_general(
                    v, kd, (((0,), (0,)), ((), ())), preferred_element_type=F32)
                y_ref[rows, vs] = _gla_out(o, gn_ref[...], g_ref[rows, vs]).astype(BF16)
            return carry

        lax.fori_loop(0, nc, chunk, 0)

        @pl.when(tbi == nt - 1)
        def _():
            for hd in range(nh):
                s_ref[hd] = st_ref[hd].T

    m = z.shape[0]
    return pl.pallas_call(
        kern, grid=(bsz, nt),
        in_specs=[pl.BlockSpec((tb, hk), lambda b_, i: (b_ * nt + i, cq)),
                  pl.BlockSpec((tb, hk), lambda b_, i: (b_ * nt + i, ck)),
                  pl.BlockSpec((tb, hv), lambda b_, i: (b_ * nt + i, cv)),
                  pl.BlockSpec((tb, hv), lambda b_, i: (b_ * nt + i, cg)),
                  pl.BlockSpec((tb, hk), lambda b_, i: (b_ * nt + i, 0)),
                  pl.BlockSpec((1, dv), lambda b_, i: (0, 0)),
                  pl.BlockSpec(memory_space=pl.ANY)],
        out_specs=[pl.BlockSpec((tb, hv), lambda b_, i: (b_ * nt + i, col_y // hv)),
                   pl.BlockSpec((None, nh, dk, dv), lambda b_, i: (b_, 0, 0, 0))],
        out_shape=[SDS(y_mix.shape, BF16), SDS((bsz, nh, dk, dv), F32)],
        scratch_shapes=[pltpu.VMEM((nh, dv, dk), F32)],
        input_output_aliases={6: 0},
        compiler_params=_cparams(2), name="gla_prompt")(z, z, z, z, la, gn, y_mix)


def _gla_sample(z, la, y_mix, st, gn, mp, ms, nh, dk, dv, col_q, col_y):
    bb = 16
    assert ms % bb == 0 and mp % ms == 0 and dk == LANE
    ns = ms // bb
    hk, hv = nh * dk, nh * dv
    cq, ck = col_q // hk, col_q // hk + 1
    cv, cg = (col_q + 2 * hk) // hv, (col_q + 2 * hk) // hv + 1
    scale = dk ** -0.5

    def kern(q_ref, k_ref, la_ref, v_ref, g_ref, gn_ref, st_ref, y_in_ref, y_ref, so_ref,
             qt_ref, kt_ref, at_ref):
        del y_in_ref
        i = pl.program_id(0)

        @pl.when(i == 0)
        def _():
            for hd in range(nh):
                ks = slice(hd * dk, (hd + 1) * dk)
                qt = (q_ref[:, ks] * scale).T
                kt = k_ref[:, ks].T
                at = jnp.exp(la_ref[:, ks]).T
                for s in range(ns):
                    qt_ref[s, hd] = qt[:, s * bb:(s + 1) * bb]
                    kt_ref[s, hd] = kt[:, s * bb:(s + 1) * bb]
                    at_ref[s, hd] = at[:, s * bb:(s + 1) * bb]

        for hd in range(nh):
            vs = slice(hd * dv, (hd + 1) * dv)
            qt = qt_ref[i, hd]
            kt = kt_ref[i, hd]
            at = at_ref[i, hd]
            outs = []
            for j in range(bb):
                v = v_ref[j:j + 1, vs]
                s_new = at[:, j:j + 1] * st_ref[j, hd] + kt[:, j:j + 1] * v
                so_ref[j, hd] = s_new
                outs.append(jnp.sum(qt[:, j:j + 1] * s_new, axis=0, keepdims=True))
            o = jnp.concatenate(outs, axis=0)
            y_ref[:, vs] = _gla_out(o, gn_ref[...], g_ref[:, vs]).astype(BF16)

    rb = mp // ms
    rbb = mp // bb
    return pl.pallas_call(
        kern, grid=(ns,),
        in_specs=[pl.BlockSpec((ms, hk), lambda i: (rb, cq)),
                  pl.BlockSpec((ms, hk), lambda i: (rb, ck)),
                  pl.BlockSpec((ms, hk), lambda i: (rb, 0)),
                  pl.BlockSpec((bb, hv), lambda i: (rbb + i, cv)),
                  pl.BlockSpec((bb, hv), lambda i: (rbb + i, cg)),
                  pl.BlockSpec((1, dv), lambda i: (0, 0)),
                  pl.BlockSpec((None, bb, nh, dk, dv), lambda i: (0, i, 0, 0, 0)),
                  pl.BlockSpec(memory_space=pl.ANY)],
        out_specs=[pl.BlockSpec((bb, hv), lambda i: (rbb + i, col_y // hv)),
                   pl.BlockSpec((None, bb, nh, dk, dv), lambda i: (0, i, 0, 0, 0))],
        out_shape=[SDS(y_mix.shape, BF16), SDS((1, ms, nh, dk, dv), F32)],
        scratch_shapes=[pltpu.VMEM((ns, nh, dk, bb), F32)] * 3,
        input_output_aliases={7: 0},
        compiler_params=_cparams(1), name="gla_sample")(z, z, la, z, z, gn, st, y_mix)


def _cast_bf16(w3):
    _, k, n = w3.shape
    tk = _tile(k, 512, 16)

    def kern(w_ref, o_ref):
        o_ref[...] = w_ref[...].astype(BF16)

    return pl.pallas_call(
        kern, grid=(k // tk,),
        in_specs=[pl.BlockSpec((None, tk, n), lambda i: (0, i, 0))],
        out_specs=pl.BlockSpec((tk, n), lambda i: (i, 0)),
        out_shape=SDS((k, n), BF16),
        compiler_params=_cparams(1), name="cast_bf16")(w3)


def _route(hn, rw):
    logits = _dot(hn.astype(BF16), rw.astype(BF16))
    ne = float(logits.shape[1])
    lane = lax.broadcasted_iota(I32, logits.shape, 1).astype(F32)
    m1 = jnp.max(logits, axis=1, keepdims=True)
    i1 = jnp.min(jnp.where(logits == m1, lane, ne), axis=1, keepdims=True)
    sel1 = lane == i1
    rest = jnp.where(sel1, -jnp.inf, logits)
    m2 = jnp.max(rest, axis=1, keepdims=True)
    i2 = jnp.min(jnp.where(rest == m2, lane, ne), axis=1, keepdims=True)
    sel2 = lane == i2
    e2 = jnp.exp(m2 - m1)
    g1 = 1.0 / (1.0 + e2)
    g2 = e2 / (1.0 + e2)
    mh = jnp.where(sel1 | sel2, 1.0, 0.0)
    gd = jnp.where(sel1, g1, 0.0) + jnp.where(sel2, g2, 0.0)
    return mh, gd


def _out_proj(y, wb, res, g, tm, res_tail=None, router_w=None):
    m, k = y.shape
    d = wb.shape[1]
    n_m = m // tm
    split = res_tail is not None
    if split:
        mp, ms = res.shape[0], res_tail.shape[0]
        assert mp + ms == m and ms <= tm
        head = tm - ms
        n_rb = pl.cdiv(mp, tm)
    route = router_w is not None

    def kern(*refs):
        it = iter(refs)
        y_ref, w_ref, r_ref = next(it), next(it), next(it)
        rt_ref = next(it) if split else None
        g_ref = next(it)
        rw_ref = next(it) if route else None
        h_ref, n_ref = next(it), next(it)
        mh_ref, gd_ref = (next(it), next(it)) if route else (None, None)
        acc_ref = next(it)
        i = pl.program_id(0)

        def matmul():
            acc_ref[i & 1] = _dot(y_ref[...], w_ref[...])

        def epilogue():
            acc = acc_ref[(i - 1) & 1]
            if split:
                last = i == n_m
                tail = acc[head:] + jnp.where(last, rt_ref[...], r_ref[head:tm, :])
                h = tail if head == 0 else jnp.concatenate([acc[:head] + r_ref[0:head, :], tail], 0)
            else:
                h = acc + r_ref[...]
            h_ref[...] = h
            hn = _rms(h, g_ref[...])
            if route:
                n_ref[...] = hn
                mh, gd = _route(hn, rw_ref[...])
                mh_ref[...] = mh
                gd_ref[...] = gd
            else:
                n_ref[...] = hn.astype(BF16)

        @pl.when(i == 0)
        def _():
            matmul()

        @pl.when((i > 0) & (i < n_m))
        def _():
            matmul()
            epilogue()

        @pl.when(i == n_m)
        def _():
            epilogue()

    def lag(i):
        return jnp.maximum(i - 1, 0)

    in_specs = [pl.BlockSpec((tm, k), lambda i: (jnp.minimum(i, n_m - 1), 0)),
                pl.BlockSpec((k, d), lambda i: (0, 0))]
    args = [y, wb]
    if split:
        in_specs += [pl.BlockSpec((tm, d), lambda i: (jnp.minimum(lag(i), n_rb - 1), 0)),
                     pl.BlockSpec((ms, d), lambda i: (0, 0))]
        args += [res, res_tail]
    else:
        in_specs += [pl.BlockSpec((tm, d), lambda i: (lag(i), 0))]
        args += [res]
    in_specs += [pl.BlockSpec((1, d), lambda i: (0, 0))]
    args += [g]
    out_specs = [pl.BlockSpec((tm, d), lambda i: (lag(i), 0))]
    out_shape = [SDS((m, d), F32)]
    if route:
        ne = router_w.shape[1]
        in_specs += [pl.BlockSpec((d, ne), lambda i: (0, 0))]
        args += [router_w]
        out_specs += [pl.BlockSpec((tm, d), lambda i: (lag(i), 0)),
                      pl.BlockSpec((tm, ne), lambda i: (lag(i), 0)),
                      pl.BlockSpec((tm, ne), lambda i: (lag(i), 0))]
        out_shape += [SDS((m, d), F32), SDS((m, ne), F32), SDS((m, ne), F32)]
    else:
        out_specs += [pl.BlockSpec((tm, d), lambda i: (lag(i), 0))]
        out_shape += [SDS((m, d), BF16)]
    return pl.pallas_call(
        kern, grid=(n_m + 1,), in_specs=in_specs, out_specs=out_specs, out_shape=out_shape,
        scratch_shapes=[pltpu.VMEM((2, tm, d), F32)],
        compiler_params=_cparams(1), name="out_proj_route" if route else "out_proj")(*args)


def _ffn(x, wg3, wu3, wd3, tm, tf):
    m, k = x.shape
    f = wg3.shape[2]
    d = wd3.shape[2]

    def kern(x_ref, wg_ref, wu_ref, wd_ref, o_ref):
        @pl.when(pl.program_id(1) == 0)
        def _():
            o_ref[...] = jnp.zeros_like(o_ref)

        xb = x_ref[...]
        gg = _dot(xb, wg_ref[...].astype(BF16))
        uu = _dot(xb, wu_ref[...].astype(BF16))
        a = (gg * _sigmoid(gg) * uu).astype(BF16)
        o_ref[...] += _dot(a, wd_ref[...].astype(BF16))

    return pl.pallas_call(
        kern, grid=(m // tm, f // tf),
        in_specs=[pl.BlockSpec((tm, k), lambda i, j: (i, 0)),
                  pl.BlockSpec((None, k, tf), lambda i, j: (0, 0, j)),
                  pl.BlockSpec((None, k, tf), lambda i, j: (0, 0, j)),
                  pl.BlockSpec((None, tf, d), lambda i, j: (0, j, 0))],
        out_specs=pl.BlockSpec((tm, d), lambda i, j: (i, 0)),
        out_shape=SDS((m, d), F32),
        compiler_params=_cparams(2), name="ffn")(x, wg3, wu3, wd3)


def _add_norm(a, b, g, tm):
    m, d = a.shape

    def kern(a_ref, b_ref, g_ref, h_ref, n_ref):
        h = a_ref[...] + b_ref[...]
        h_ref[...] = h
        n_ref[...] = _rms(h, g_ref[...]).astype(BF16)

    row = pl.BlockSpec((tm, d), lambda i: (i, 0))
    return pl.pallas_call(
        kern, grid=(m // tm,),
        in_specs=[row, row, pl.BlockSpec((1, d), lambda i: (0, 0))],
        out_specs=[row, row],
        out_shape=[SDS((m, d), F32), SDS((m, d), BF16)],
        compiler_params=_cparams(1), name="add_norm")(a, b, g)


def _in_proj1(x, w3, tm, tn):
    m, k = x.shape
    dc = w3.shape[2] // 3
    nb = dc // tn

    def kern(x_ref, wb_ref, wc_ref, wv_ref, gb_ref, cv_ref, sb_ref, sc_ref, sv_ref):
        @pl.when(pl.program_id(1) == 0)
        def _():
            sb_ref[...] = wb_ref[...].astype(BF16)
            sc_ref[...] = wc_ref[...].astype(BF16)
            sv_ref[...] = wv_ref[...].astype(BF16)

        xb = x_ref[...]
        gb_ref[...] = _dot(xb, sb_ref[...]).astype(BF16)
        cv_ref[...] = _dot(xb, sc_ref[...]) * _dot(xb, sv_ref[...])

    def wspec(g):
        return pl.BlockSpec((None, k, tn), lambda n, i: (0, 0, g * nb + n))

    return pl.pallas_call(
        kern, grid=(nb, m // tm),
        in_specs=[pl.BlockSpec((tm, k), lambda n, i: (i, 0)), wspec(0), wspec(1), wspec(2)],
        out_specs=[pl.BlockSpec((tm, tn), lambda n, i: (i, n))] * 2,
        out_shape=[SDS((m, dc), BF16), SDS((m, dc), F32)],
        scratch_shapes=[pltpu.VMEM((k, tn), BF16)] * 3,
        compiler_params=_cparams(2), name="in_proj1")(x, w3, w3, w3)


def _shortconv_prompt(gb, cv, cw, bsz, t):
    m, dc = cv.shape
    tc = _tile(dc, 512, LANE)
    nw = cw.shape[0]

    def kern(gb_ref, cv_ref, cw_ref, u_ref, buf_ref):
        x = cv_ref[...]
        row = lax.broadcasted_iota(I32, x.shape, 0)
        w = cw_ref[...]
        u = w[nw - 1:nw, :] * x
        for s in range(1, nw):
            u = u + w[nw - 1 - s:nw - s, :] * _shift_rows(x, s, row, 0.0)
        u_ref[...] = (gb_ref[...].astype(F32) * u).astype(BF16)
        buf_ref[...] = x[t - (nw - 1):t, :]

    blk = pl.BlockSpec((t, tc), lambda b_, c: (b_, c))
    return pl.pallas_call(
        kern, grid=(bsz, dc // tc),
        in_specs=[blk, blk, pl.BlockSpec((nw, tc), lambda b_, c: (0, c))],
        out_specs=[blk, pl.BlockSpec((None, nw - 1, tc), lambda b_, c: (b_, 0, c))],
        out_shape=[SDS((m, dc), BF16), SDS((bsz, nw - 1, dc), F32)],
        compiler_params=_cparams(2), name="shortconv_prompt")(gb, cv, cw)


def _shortconv_sample(gb, cv, u_all, st, cw, mp, ms):
    dc = cv.shape[1]
    tc = _tile(dc, 512, LANE)
    nw = cw.shape[0]
    rb = mp // ms

    def kern(gb_ref, cv_ref, u_in_ref, st_ref, cw_ref, u_ref, buf_ref):
        del u_in_ref
        x = cv_ref[...]
        w = cw_ref[...]
        u = w[nw - 1:nw, :] * x
        for s in range(nw - 1):
            u = u + w[s:s + 1, :] * st_ref[:, s, :]
        u_ref[...] = (gb_ref[...].astype(F32) * u).astype(BF16)
        for s in range(nw - 2):
            buf_ref[:, s, :] = st_ref[:, s + 1, :]
        buf_ref[:, nw - 2, :] = x

    blk = pl.BlockSpec((ms, tc), lambda c: (rb, c))
    stb = pl.BlockSpec((ms, nw - 1, tc), lambda c: (0, 0, c))
    return pl.pallas_call(
        kern, grid=(dc // tc,),
        in_specs=[blk, blk, pl.BlockSpec(memory_space=pl.ANY), stb,
                  pl.BlockSpec((nw, tc), lambda c: (0, c))],
        out_specs=[blk, stb],
        out_shape=[SDS(u_all.shape, BF16), SDS((ms, nw - 1, dc), F32)],
        input_output_aliases={2: 0},
        compiler_params=_cparams(1), name="shortconv_sample")(gb, cv, u_all, st, cw)


def _moe_rank(mh, tm):
    m, ne = mh.shape

    def kern(mh_ref, ex_ref, cnt_ref, carry_ref):
        @pl.when(pl.program_id(0) == 0)
        def _():
            carry_ref[...] = jnp.zeros_like(carry_ref)

        x = mh_ref[...]
        tri = (lax.broadcasted_iota(I32, (tm, tm), 0) > lax.broadcasted_iota(I32, (tm, tm), 1))
        ex = _dot(jnp.where(tri, 1.0, 0.0).astype(BF16), x.astype(BF16)) + carry_ref[...]
        ex_ref[...] = ex
        tot = ex[tm - 1:tm, :] + x[tm - 1:tm, :]
        carry_ref[...] = tot
        cnt_ref[...] = tot

    return pl.pallas_call(
        kern, grid=(m // tm,),
        in_specs=[pl.BlockSpec((tm, ne), lambda i: (i, 0))],
        out_specs=[pl.BlockSpec((tm, ne), lambda i: (i, 0)), pl.BlockSpec((1, ne), lambda i: (0, 0))],
        out_shape=[SDS((m, ne), F32), SDS((1, ne), F32)],
        scratch_shapes=[pltpu.VMEM((1, ne), F32)],
        compiler_params=_cparams(1), name="moe_rank")(mh)


def _moe_pos(mh, gd, ex, off, tm):
    m, ne = mh.shape

    def kern(mh_ref, gd_ref, ex_ref, off_ref, pos_ref, gate_ref):
        sel = mh_ref[...] > 0.5
        pd = ex_ref[...] + off_ref[...]
        big = jnp.float32(3e38)
        p_lo = jnp.min(jnp.where(sel, pd, big), axis=1, keepdims=True)
        p_hi = jnp.max(jnp.where(sel, pd, -big), axis=1, keepdims=True)
        gdv = gd_ref[...]
        g_lo = jnp.sum(jnp.where(sel & (pd == p_lo), gdv, 0.0), axis=1, keepdims=True)
        g_hi = jnp.sum(jnp.where(sel & (pd == p_hi), gdv, 0.0), axis=1, keepdims=True)
        pos_ref[:, 0:1] = p_lo.astype(I32)
        pos_ref[:, 1:2] = p_hi.astype(I32)
        gate_ref[:, 0:1] = g_lo
        gate_ref[:, 1:2] = g_hi

    blk = pl.BlockSpec((tm, ne), lambda i: (i, 0))
    two = pl.BlockSpec((tm, TOP_K), lambda i: (i, 0))
    return pl.pallas_call(
        kern, grid=(m // tm,),
        in_specs=[blk, blk, blk, pl.BlockSpec((1, ne), lambda i: (0, 0))],
        out_specs=[two, two],
        out_shape=[SDS((m, TOP_K), I32), SDS((m, TOP_K), F32)],
        compiler_params=_cparams(1), name="moe_pos")(mh, gd, ex, off)


def _moe_scatter(pos_flat, pad_rows, n_pad, x, n_rows, tm):
    m, w = x.shape

    def kern(pos_ref, pad_ref, npad_ref, x_ref, xs_ref, zero_ref, sem, zsem):
        i = pl.program_id(0)

        def row_copy(r, p):
            return pltpu.make_async_copy(x_ref.at[pl.ds(r, 1)], xs_ref.at[pl.ds(p, 1)], sem)

        def issue(r, c):
            t = i * tm + r
            row_copy(r, pos_ref[TOP_K * t]).start()
            row_copy(r, pos_ref[TOP_K * t + 1]).start()
            return c

        lax.fori_loop(0, tm, issue, 0, unroll=8)

        @pl.when(i == 0)
        def _():
            zero_ref[...] = jnp.zeros_like(zero_ref)
            npad = npad_ref[0]

            def zcopy(j):
                return pltpu.make_async_copy(zero_ref.at[pl.ds(0, 1)],
                                             xs_ref.at[pl.ds(pad_ref[j], 1)], zsem)

            def zissue(j, c):
                zcopy(j).start()
                return c

            def zwait(j, c):
                zcopy(j).wait()
                return c

            lax.fori_loop(0, npad, zissue, 0)
            lax.fori_loop(0, npad, zwait, 0)

        def drain(r, c):
            row_copy(0, 0).wait()
            row_copy(0, 0).wait()
            return c

        lax.fori_loop(0, tm, drain, 0, unroll=8)

    return pl.pallas_call(
        kern,
        grid_spec=pltpu.PrefetchScalarGridSpec(
            num_scalar_prefetch=3, grid=(m // tm,),
            in_specs=[pl.BlockSpec((tm, w), lambda i, *_: (i, 0))],
            out_specs=pl.BlockSpec(memory_space=pl.ANY),
            scratch_shapes=[pltpu.VMEM((8, w), F32), pltpu.SemaphoreType.DMA(()),
                            pltpu.SemaphoreType.DMA(())]),
        out_shape=SDS((n_rows, w), F32),
        compiler_params=_cparams(1), name="moe_scatter")(pos_flat, pad_rows, n_pad, x)


def _moe_ffn(st_e, st_row, st_nsub, xs, wg4, wu4, wd4, rs, sub, tf):
    n_rows, k = xs.shape
    f = wg4.shape[3]
    d = wd4.shape[3]
    ng = st_e.shape[0]
    nf = f // tf
    assert k == d and nf >= 2
    kc = _tile(k, 512, LANE)
    big, mid = 4 * sub, 2 * sub

    def kern(se_ref, sr_ref, sn_ref, xs_ref, wg_ref, wu_ref, wd_ref, ys_ref,
             x_ref, acc_ref, a_ref, wgb_ref, wub_ref, wdb_ref, sem_in, sem_out):
        g = pl.program_id(0)
        j = pl.program_id(1)
        nsub = sn_ref[g]
        row0 = sr_ref[g]
        nbig = lax.shift_right_logical(nsub, 2)
        has_mid = (nsub & 2) != 0
        has_small = (nsub & 1) != 0
        start_mid = nbig * big
        start_small = start_mid + jnp.where(has_mid, mid, 0)

        def rows(start, size):
            return pl.ds(pl.multiple_of(start, sub), size)

        def in_copy(s):
            return pltpu.make_async_copy(xs_ref.at[rows(row0 + s * sub, sub)],
                                         acc_ref.at[rows(s * sub, sub)], sem_in)

        def out_copy(start, size):
            return pltpu.make_async_copy(acc_ref.at[rows(start, size)],
                                         ys_ref.at[rows(row0 + start, size)], sem_out)

        def each_sub(fn):
            def body(s, c):
                fn(s)
                return c
            lax.fori_loop(0, nsub, body, 0)

        def cast_weights():
            for c in range(k // kc):
                ks = slice(c * kc, (c + 1) * kc)
                wgb_ref[ks, :] = wg_ref[ks, :].astype(BF16)
                wub_ref[ks, :] = wu_ref[ks, :].astype(BF16)
            wdb_ref[...] = wd_ref[...].astype(BF16)

        def up(start, size, slot):
            xb = x_ref[rows(start, size), :]
            gg = _dot(xb, wgb_ref[...])
            uu = _dot(xb, wub_ref[...])
            a_ref[slot, 0:size, :] = (gg * _sigmoid(gg) * uu).astype(BF16)

        def down(start, size, slot, first, last):
            dd = _dot(a_ref[slot, 0:size, :], wdb_ref[...])
            if first:
                acc_ref[rows(start, size), :] = dd
            else:
                acc_ref[rows(start, size), :] += dd
            if last:
                out_copy(start, size).start()

        def compute(first, last, cast):
            @pl.when(nbig > 0)
            def _():
                if cast:
                    cast_weights()
                up(0, big, 0)

                def body(s, c):
                    up(s * big, big, s & 1)
                    down((s - 1) * big, big, (s - 1) & 1, first, last)
                    return c

                lax.fori_loop(1, nbig, body, 0)
                down((nbig - 1) * big, big, (nbig - 1) & 1, first, last)

            if cast:
                @pl.when(nbig == 0)
                def _():
                    cast_weights()

            @pl.when(has_mid)
            def _():
                up(start_mid, mid, 0)
                down(start_mid, mid, 0, first, last)

            @pl.when(has_small)
            def _():
                up(start_small, sub, 0)
                down(start_small, sub, 0, first, last)

        @pl.when(nsub > 0)
        def _():
            @pl.when(j == 0)
            def _():
                each_sub(lambda s: in_copy(s).start())
                cast_weights()

                def to_bf16(s):
                    x_ref[rows(s * sub, sub), :] = acc_ref[rows(s * sub, sub), :].astype(BF16)

                each_sub(lambda s: in_copy(s).wait())
                each_sub(to_bf16)
                compute(True, False, False)

            @pl.when((j > 0) & (j < nf - 1))
            def _():
                compute(False, False, True)

            @pl.when(j == nf - 1)
            def _():
                compute(False, True, True)

                def wait_big(s, c):
                    out_copy(0, big).wait()
                    return c

                lax.fori_loop(0, nbig, wait_big, 0)

                @pl.when(has_mid)
                def _():
                    out_copy(0, mid).wait()

                @pl.when(has_small)
                def _():
                    out_copy(0, sub).wait()

    def widx(g, j, se, sr, sn):
        return (0, se[g], 0, jnp.where(sn[g] > 0, j, nf - 1))

    def didx(g, j, se, sr, sn):
        return (0, se[g], jnp.where(sn[g] > 0, j, nf - 1), 0)

    return pl.pallas_call(
        kern,
        grid_spec=pltpu.PrefetchScalarGridSpec(
            num_scalar_prefetch=3, grid=(ng, nf),
            in_specs=[pl.BlockSpec(memory_space=pl.ANY),
                      pl.BlockSpec((None, None, k, tf), widx),
                      pl.BlockSpec((None, None, k, tf), widx),
                      pl.BlockSpec((None, None, tf, d), didx)],
            out_specs=pl.BlockSpec(memory_space=pl.ANY),
            scratch_shapes=[pltpu.VMEM((rs, k), BF16), pltpu.VMEM((rs, d), F32),
                            pltpu.VMEM((2, 4 * sub, tf), BF16),
                            pltpu.VMEM((k, tf), BF16), pltpu.VMEM((k, tf), BF16),
                            pltpu.VMEM((tf, d), BF16),
                            pltpu.SemaphoreType.DMA(()), pltpu.SemaphoreType.DMA(())]),
        out_shape=SDS((n_rows, d), F32),
        compiler_params=_cparams(2), name="moe_ffn")(st_e, st_row, st_nsub, xs, wg4, wu4, wd4)


def _moe_combine(pos_flat, h, gates, ys, g, mp, ms, tp):
    m, d = h.shape
    n_p = mp // tp
    assert mp % tp == 0 and ms <= tp and ms % 8 == 0

    def kern(pos_ref, h_ref, gate_ref, ys_ref, g_ref, yp_ref, ysm_ref, a_ref, b_ref, sem):
        i = pl.program_id(0)

        def copies(tile, r):
            t = tile * tp + r
            slot = tile & 1
            return (pltpu.make_async_copy(ys_ref.at[pl.ds(pos_ref[TOP_K * t], 1)],
                                          a_ref.at[slot, pl.ds(r, 1)], sem.at[slot]),
                    pltpu.make_async_copy(ys_ref.at[pl.ds(pos_ref[TOP_K * t + 1], 1)],
                                          b_ref.at[slot, pl.ds(r, 1)], sem.at[slot]))

        def gather(tile):
            def issue(r, c):
                ca, cb = copies(tile, r)
                ca.start()
                cb.start()
                return c

            @pl.when(tile < n_p)
            def _():
                lax.fori_loop(0, tp, issue, 0, unroll=8)

            @pl.when(tile == n_p)
            def _():
                lax.fori_loop(0, ms, issue, 0, unroll=8)

        def combined(n):
            def drain(r, c):
                ca, cb = copies(i, 0)
                ca.wait()
                cb.wait()
                return c

            lax.fori_loop(0, n, drain, 0, unroll=8)
            slot = i & 1
            gt = gate_ref[0:n, :]
            hh = (h_ref[0:n, :] + gt[:, 0:1] * a_ref[slot, 0:n, :]
                  + gt[:, 1:2] * b_ref[slot, 0:n, :])
            return _rms(hh, g_ref[...])

        @pl.when(i == 0)
        def _():
            gather(i)

        gather(i + 1)

        @pl.when(i < n_p)
        def _():
            yp_ref[...] = combined(tp)

        @pl.when(i == n_p)
        def _():
            ysm_ref[...] = combined(ms)

    return pl.pallas_call(
        kern,
        grid_spec=pltpu.PrefetchScalarGridSpec(
            num_scalar_prefetch=1, grid=(n_p + 1,),
            in_specs=[pl.BlockSpec((tp, d), lambda i, *_: (i, 0)),
                      pl.BlockSpec((tp, TOP_K), lambda i, *_: (i, 0)),
                      pl.BlockSpec(memory_space=pl.ANY),
                      pl.BlockSpec((1, d), lambda i, *_: (0, 0))],
            out_specs=[pl.BlockSpec((tp, d), lambda i, *_: (jnp.minimum(i, n_p - 1), 0)),
                       pl.BlockSpec((ms, d), lambda i, *_: (0, 0))],
            scratch_shapes=[pltpu.VMEM((2, tp, d), F32), pltpu.VMEM((2, tp, d), F32),
                            pltpu.SemaphoreType.DMA((2,))]),
        out_shape=[SDS((mp, d), F32), SDS((ms, d), F32)],
        compiler_params=_cparams(1), name="moe_combine")(pos_flat, h, gates, ys, g)


def _moe_tables(cnt, sub, rs, ng):
    ne = cnt.shape[0]
    nsub_e = (cnt + sub - 1) // sub
    size_e = nsub_e * sub
    off = jnp.cumsum(size_e) - size_e
    spr = rs // sub
    nst_e = (nsub_e + spr - 1) // spr
    st_start = jnp.cumsum(nst_e) - nst_e
    n_act = jnp.sum(nst_e)
    gidx = jnp.arange(ng, dtype=I32)
    e_of = jnp.sum((gidx[:, None] >= st_start[None, :]).astype(I32), axis=1) - 1
    e_of = jnp.clip(e_of, 0, ne - 1)
    kth = gidx - st_start[e_of]
    active = gidx < n_act
    nsub = jnp.where(active, jnp.clip(nsub_e[e_of] - kth * spr, 0, spr), 0)
    row = off[e_of] + kth * rs
    last = jnp.maximum(n_act - 1, 0)
    st_e = jnp.where(active, e_of, e_of[last]).astype(I32)
    st_row = jnp.where(active, row, 0).astype(I32)
    n_pad_e = size_e - cnt
    pad_start = jnp.cumsum(n_pad_e) - n_pad_e
    n_pad = jnp.sum(n_pad_e)
    pidx = jnp.arange(ne * (sub - 1), dtype=I32)
    pe = jnp.clip(jnp.sum((pidx[:, None] >= pad_start[None, :]).astype(I32), axis=1) - 1, 0, ne - 1)
    pad_rows = jnp.where(pidx < n_pad, off[pe] + cnt[pe] + (pidx - pad_start[pe]), 0).astype(I32)
    return off, st_e, st_row, nsub.astype(I32), pad_rows, n_pad.astype(I32).reshape(1)


def kernel(x_prompt, x_sample, state_rg_conv, state_rg_h, state_gla, state_sc_conv, norm_mix_e, w_in_e, rg_conv_w, rg_conv_b, rg_w_a, rg_b_a, rg_w_x, rg_b_x, rg_lambda, gla_w_gate, gla_b_gate, gla_norm, w_out_e, norm_ffn_e, ffn_w_gate, ffn_w_up, ffn_w_down, norm_mix_o, w_in_o, sc_conv_w, w_out_o, norm_ffn_o, router_w, moe_w_gate, moe_w_up, moe_w_down, final_norm):
    bsz, t, d = x_prompt.shape
    ms = x_sample.shape[0]
    assert x_sample.shape[1] == 1 and w_in_e.shape[0] == 1 and w_in_o.shape[0] == 1
    mp = bsz * t
    m = mp + ms
    d_rnn = rg_lambda.shape[1]
    nh, dk, dv = state_gla.shape[2:]
    hk, hv = nh * dk, nh * dv
    n_main = 2 * d_rnn + 2 * hk + 2 * hv
    d_mix = d_rnn + hv
    ne = router_w.shape[2]
    xp = x_prompt.reshape(mp, d)
    xs = x_sample.reshape(ms, d)
    row = lambda v: v.reshape(1, -1)

    tp = _tile(mp, TP_TARGET, 16)
    tm = _tile(m, TM_TARGET, 16)
    tm_in = _tile(m, TM_IN_TARGET, 16)
    tm_out = _tile(m, TM_OUT_TARGET, 16)

    hn0 = _norm_in(xp, xs, norm_mix_e, tp)
    w_in_t = jnp.swapaxes(w_in_e, 1, 2)
    z0 = _in_proj0(hn0, w_in_t, n_main, tm_in, _tile(n_main, 1024, LANE))
    la = _gla_gate(hn0, w_in_t, n_main, gla_w_gate[0], gla_b_gate, tm)
    y_mix, rgc_p, rgh_p = _rglru_prompt(z0, rg_conv_w[0], rg_conv_b, rg_w_a[0], rg_b_a, rg_w_x[0],
                                        rg_b_x, rg_lambda, bsz, t, d_rnn, d_mix)
    y_mix, rgc_s, rgh_s = _rglru_sample(z0, y_mix, state_rg_conv[0], state_rg_h[0], rg_conv_w[0],
                                        rg_conv_b, rg_w_a[0], rg_b_a, rg_w_x[0], rg_b_x, rg_lambda,
                                        mp, ms, d_rnn)
    y_mix, gla_p = _gla_prompt(z0, la, y_mix, gla_norm, bsz, t, nh, dk, dv, 2 * d_rnn, d_rnn)
    y_mix, gla_s = _gla_sample(z0, la, y_mix, state_gla, gla_norm, mp, ms, nh, dk, dv,
                               2 * d_rnn, d_rnn)
    h1, hn1 = _out_proj(y_mix, _cast_bf16(w_out_e), xp, norm_ffn_e, tm_out, res_tail=xs)

    f0 = _ffn(hn1, ffn_w_gate, ffn_w_up, ffn_w_down, tm, _tile(ffn_w_gate.shape[2], 512, LANE))
    h2, hn2 = _add_norm(h1, f0, norm_mix_o, tm)

    gb, cv = _in_proj1(hn2, w_in_o, tm_in, _tile(w_in_o.shape[2] // 3, 256, LANE))
    u, sc_p = _shortconv_prompt(gb, cv, sc_conv_w[0], bsz, t)
    u, sc_s = _shortconv_sample(gb, cv, u, state_sc_conv[0], sc_conv_w[0], mp, ms)
    h3, hn3, mh, gd = _out_proj(u, _cast_bf16(w_out_o), h2, norm_ffn_o, tm_out,
                                router_w=router_w[0])

    sub = MOE_SUB
    rs = MOE_SPR * sub
    ng = (TOP_K * m) // rs + ne
    n_rows = TOP_K * m + ne * (sub - 1)
    n_rows = ((n_rows + sub - 1) // sub) * sub
    ex, cnt = _moe_rank(mh, tm)
    off, st_e, st_row, st_nsub, pad_rows, n_pad = _moe_tables(cnt[0].astype(I32), sub, rs, ng)
    pos, gates = _moe_pos(mh, gd, ex, off.astype(F32).reshape(1, ne), tm)
    pos_flat = pos.reshape(TOP_K * m)
    xsort = _moe_scatter(pos_flat, pad_rows, n_pad, hn3, n_rows, tm)
    ys = _moe_ffn(st_e, st_row, st_nsub, xsort, moe_w_gate, moe_w_up, moe_w_down, rs, sub,
                  _tile(moe_w_gate.shape[3], 256, LANE))
    y_p, y_s = _moe_combine(pos_flat, h3, gates, ys, row(final_norm), mp, ms, tp)

    return (y_p.reshape(bsz, t, d), y_s.reshape(ms, 1, d),
            rgc_p[None], rgc_s[None], rgh_p.reshape(1, bsz, d_rnn), rgh_s[None],
            gla_p[None], gla_s, sc_p[None], sc_s[None])
```

```python
import functools

import jax
import jax.numpy as jnp
from jax import lax
from jax.experimental import pallas as pl
from jax.experimental.pallas import tpu as pltpu

F32 = jnp.float32
BF16 = jnp.bfloat16
I32 = jnp.int32
SDS = jax.ShapeDtypeStruct

EPS = 1e-6
RG_C = 8.0
GLA_TAU = 16.0
GLA_CHUNK = 64
TOP_K = 2
LANE = 128
SUB8 = 8
VMEM_LIMIT = 56 * 1024 * 1024
ARB = "arbitrary"
TM_TARGET = 640
TM_IN_TARGET = 1664
TM_OUT_TARGET = 320
TP_TARGET = 512
GLA_TB_TARGET = 512
MOE_SUB = 128
MOE_SPR = 20


def _cparams(n_axes, vmem=VMEM_LIMIT):
    return pltpu.CompilerParams(dimension_semantics=(ARB,) * n_axes, vmem_limit_bytes=vmem)


def _tile(n, target, align):
    best = None
    for t in range(align, min(n, target) + 1, align):
        if n % t == 0:
            best = t
    assert best is not None, (n, target, align)
    return best


def _rms(xf, g):
    ms = jnp.mean(xf * xf, axis=-1, keepdims=True)
    return xf * lax.rsqrt(ms + EPS) * g


def _sigmoid(x):
    return 1.0 / (1.0 + jnp.exp(-x))


def _softplus(x):
    return jnp.maximum(x, 0.0) + jnp.log1p(jnp.exp(-jnp.abs(x)))


def _gelu_tanh(x):
    c = 0.7978845608028654
    return x * (0.5 * (1.0 + jnp.tanh(c * (x + 0.044715 * (x * x * x)))))


def _dot(a, b):
    return jnp.dot(a, b, preferred_element_type=F32)


def _shift_rows(x, s, row, fill):
    return jnp.where(row >= s, pltpu.roll(x, s, 0), fill)


def _norm_in(xp, xs, g, tp):
    mp, d = xp.shape
    ms = xs.shape[0]
    n_p = mp // tp
    assert mp % tp == 0 and ms <= tp and ms % 16 == 0

    def kern(xp_ref, xs_ref, g_ref, o_ref):
        i = pl.program_id(0)

        @pl.when(i < n_p)
        def _():
            o_ref[...] = _rms(xp_ref[...], g_ref[...]).astype(BF16)

        @pl.when(i == n_p)
        def _():
            o_ref[0:ms, :] = _rms(xs_ref[...], g_ref[...]).astype(BF16)

    return pl.pallas_call(
        kern, grid=(n_p + 1,),
        in_specs=[pl.BlockSpec((tp, d), lambda i: (jnp.minimum(i, n_p - 1), 0)),
                  pl.BlockSpec((ms, d), lambda i: (0, 0)),
                  pl.BlockSpec((1, d), lambda i: (0, 0))],
        out_specs=pl.BlockSpec((tp, d), lambda i: (i, 0)),
        out_shape=SDS((mp + ms, d), BF16),
        compiler_params=_cparams(1), name="norm_in")(xp, xs, g)


def _in_proj0(x, wt3, n_cols, tm, tn):
    m, k = x.shape

    def kern(x_ref, w_ref, o_ref, wb_ref):
        @pl.when(pl.program_id(1) == 0)
        def _():
            wb_ref[...] = w_ref[...].T.astype(BF16)

        o_ref[...] = _dot(x_ref[...], wb_ref[...])

    return pl.pallas_call(
        kern, grid=(n_cols // tn, m // tm),
        in_specs=[pl.BlockSpec((tm, k), lambda n, i: (i, 0)),
                  pl.BlockSpec((None, tn, k), lambda n, i: (0, n, 0))],
        out_specs=pl.BlockSpec((tm, tn), lambda n, i: (i, n)),
        out_shape=SDS((m, n_cols), F32),
        scratch_shapes=[pltpu.VMEM((k, tn), BF16)],
        compiler_params=_cparams(2), name="in_proj0")(x, wt3)


def _gla_gate(x, wt3, col0, w_gate, b_gate, tm):
    m, k = x.shape
    r, n = w_gate.shape
    assert col0 % r == 0

    def kern(x_ref, wl_ref, wg_ref, bg_ref, o_ref):
        lr = lax.dot_general(x_ref[...], wl_ref[...].astype(BF16), (((1,), (1,)), ((), ())),
                             preferred_element_type=F32)
        pre = jnp.dot(lr, wg_ref[...], preferred_element_type=F32,
                      precision=lax.Precision.HIGHEST) + bg_ref[...]
        o_ref[...] = -_softplus(-pre) * (1.0 / GLA_TAU)

    return pl.pallas_call(
        kern, grid=(m // tm,),
        in_specs=[pl.BlockSpec((tm, k), lambda i: (i, 0)),
                  pl.BlockSpec((None, r, k), lambda i: (0, col0 // r, 0)),
                  pl.BlockSpec((r, n), lambda i: (0, 0)),
                  pl.BlockSpec((1, n), lambda i: (0, 0))],
        out_specs=pl.BlockSpec((tm, n), lambda i: (i, 0)),
        out_shape=SDS((m, n), F32),
        compiler_params=_cparams(1), name="gla_gate")(x, wt3, w_gate, b_gate)


def _rg_gates(xc, wa_ref, wx_ref, ba_ref, bx_ref, lam_ref):
    w2 = jnp.concatenate([wa_ref[...], wx_ref[...]], axis=1).astype(BF16)
    pre = _dot(xc.astype(BF16), w2)
    r = _sigmoid(pre[:, :LANE] + ba_ref[...])
    i = _sigmoid(pre[:, LANE:] + bx_ref[...])
    log_a = (-RG_C) * r * _softplus(-lam_ref[...])
    a = jnp.exp(log_a)
    mult = jnp.sqrt(jnp.tanh(-log_a) * (1.0 + a * a))
    return a, mult, i


def _rglru_prompt(z, cw, cb, wa, ba, wx, bx, lam, bsz, t, d_rnn, d_out):
    nh = d_rnn // LANE
    m = z.shape[0]

    def kern(xr_ref, gr_ref, cw_ref, cb_ref, wa_ref, ba_ref, wx_ref, bx_ref, lam_ref,
             y_ref, conv_ref, h_ref, a_s, b_s, c_s):
        xr = xr_ref[...]
        row = lax.broadcasted_iota(I32, (t, LANE), 0)
        w = cw_ref[...]
        nw = w.shape[0]
        xc = w[nw - 1:nw, :] * xr
        for s in range(1, nw):
            xc = xc + w[nw - 1 - s:nw - s, :] * _shift_rows(xr, s, row, 0.0)
        xc = xc + cb_ref[...]
        a, mult, i = _rg_gates(xc, wa_ref, wx_ref, ba_ref, bx_ref, lam_ref)
        mult = jnp.where(row == 0, 1.0, mult)
        b = mult * i * xc
        ng = t // SUB8
        a3, b3 = a.reshape(ng, SUB8, LANE), b.reshape(ng, SUB8, LANE)
        r8 = lax.broadcasted_iota(I32, (ng, SUB8, LANE), 1)
        s = 1
        while s < SUB8:
            b3 = a3 * jnp.where(r8 >= s, pltpu.roll(b3, s, 1), 0.0) + b3
            a3 = a3 * jnp.where(r8 >= s, pltpu.roll(a3, s, 1), 1.0)
            s *= 2
        a_s[...] = a3.reshape(t, LANE)
        b_s[...] = b3.reshape(t, LANE)
        ag = a_s[pl.ds(SUB8 - 1, ng, stride=SUB8), :]
        bg = b_s[pl.ds(SUB8 - 1, ng, stride=SUB8), :]
        rowg = lax.broadcasted_iota(I32, (ng, LANE), 0)
        s = 1
        while s < ng:
            bg = ag * _shift_rows(bg, s, rowg, 0.0) + bg
            ag = ag * _shift_rows(ag, s, rowg, 1.0)
            s *= 2
        carry = _shift_rows(bg, 1, rowg, 0.0)
        for k in range(SUB8):
            c_s[pl.ds(k, ng, stride=SUB8), :] = carry
        h = b_s[...] + a_s[...] * c_s[...]
        y_ref[...] = (h * _gelu_tanh(gr_ref[...])).astype(BF16)
        conv_ref[...] = xr[t - (nw - 1):t, :]
        h_ref[...] = h[t - 1:t, :]

    vec = pl.BlockSpec((1, LANE), lambda b_, h: (0, h))
    blk = pl.BlockSpec((None, LANE, LANE), lambda b_, h: (h, 0, 0))
    nw = cw.shape[0]
    return pl.pallas_call(
        kern, grid=(bsz, nh),
        in_specs=[pl.BlockSpec((t, LANE), lambda b_, h: (b_, h)),
                  pl.BlockSpec((t, LANE), lambda b_, h: (b_, nh + h)),
                  pl.BlockSpec((nw, LANE), lambda b_, h: (0, h)),
                  vec, blk, vec, blk, vec, vec],
        out_specs=[pl.BlockSpec((t, LANE), lambda b_, h: (b_, h)),
                   pl.BlockSpec((None, nw - 1, LANE), lambda b_, h: (b_, 0, h)),
                   pl.BlockSpec((None, 1, LANE), lambda b_, h: (b_, 0, h))],
        out_shape=[SDS((m, d_out), BF16), SDS((bsz, nw - 1, d_rnn), F32), SDS((bsz, 1, d_rnn), F32)],
        scratch_shapes=[pltpu.VMEM((t, LANE), F32)] * 3,
        compiler_params=_cparams(2), name="rglru_prompt")(z, z, cw, cb, wa, ba, wx, bx, lam)


def _rglru_sample(z, y_mix, st_conv, st_h, cw, cb, wa, ba, wx, bx, lam, mp, ms, d_rnn):
    nh = d_rnn // LANE
    nw = cw.shape[0]
    rb = mp // ms
    assert mp % ms == 0

    def kern(xr_ref, gr_ref, y_in_ref, sc_ref, sh_ref, cw_ref, cb_ref, wa_ref, ba_ref, wx_ref,
             bx_ref, lam_ref, y_ref, conv_ref, h_ref):
        del y_in_ref
        xr = xr_ref[...]
        w = cw_ref[...]
        xc = w[nw - 1:nw, :] * xr
        for s in range(nw - 1):
            xc = xc + w[s:s + 1, :] * sc_ref[:, s, :]
        xc = xc + cb_ref[...]
        a, mult, i = _rg_gates(xc, wa_ref, wx_ref, ba_ref, bx_ref, lam_ref)
        h = a * sh_ref[...] + mult * i * xc
        y_ref[...] = (h * _gelu_tanh(gr_ref[...])).astype(BF16)
        for s in range(nw - 2):
            conv_ref[:, s, :] = sc_ref[:, s + 1, :]
        conv_ref[:, nw - 2, :] = xr
        h_ref[...] = h

    vec = pl.BlockSpec((1, LANE), lambda h: (0, h))
    blk = pl.BlockSpec((None, LANE, LANE), lambda h: (h, 0, 0))
    return pl.pallas_call(
        kern, grid=(nh,),
        in_specs=[pl.BlockSpec((ms, LANE), lambda h: (rb, h)),
                  pl.BlockSpec((ms, LANE), lambda h: (rb, nh + h)),
                  pl.BlockSpec(memory_space=pl.ANY),
                  pl.BlockSpec((ms, nw - 1, LANE), lambda h: (0, 0, h)),
                  pl.BlockSpec((ms, LANE), lambda h: (0, h)),
                  pl.BlockSpec((nw, LANE), lambda h: (0, h)),
                  vec, blk, vec, blk, vec, vec],
        out_specs=[pl.BlockSpec((ms, LANE), lambda h: (rb, h)),
                   pl.BlockSpec((ms, nw - 1, LANE), lambda h: (0, 0, h)),
                   pl.BlockSpec((ms, LANE), lambda h: (0, h))],
        out_shape=[SDS(y_mix.shape, BF16), SDS((ms, nw - 1, d_rnn), F32), SDS((ms, d_rnn), F32)],
        input_output_aliases={2: 0},
        compiler_params=_cparams(1), name="rglru_sample")(
            z, z, y_mix, st_conv, st_h, cw, cb, wa, ba, wx, bx, lam)


def _gla_out(o, gn, g):
    return _rms(o, gn) * (g * _sigmoid(g))


def _gla_prompt(z, la, y_mix, gn, bsz, t, nh, dk, dv, col_q, col_y):
    tb = _tile(t, GLA_TB_TARGET, GLA_CHUNK)
    nt = t // tb
    nc = tb // GLA_CHUNK
    c = GLA_CHUNK
    hk, hv = nh * dk, nh * dv
    assert col_q % hk == 0 and (col_q + 2 * hk) % hv == 0 and col_y % hv == 0
    cq, ck = col_q // hk, col_q // hk + 1
    cv, cg = (col_q + 2 * hk) // hv, (col_q + 2 * hk) // hv + 1
    scale = dk ** -0.5

    def kern(q_ref, k_ref, v_ref, g_ref, la_ref, gn_ref, y_in_ref, y_ref, s_ref, st_ref):
        del y_in_ref
        tbi = pl.program_id(1)

        @pl.when(tbi == 0)
        def _():
            st_ref[...] = jnp.zeros_like(st_ref)

        row = lax.broadcasted_iota(I32, (c, dk), 0)
        causal = (lax.broadcasted_iota(I32, (c, c), 0) >= lax.broadcasted_iota(I32, (c, c), 1))

        def chunk(ci, carry):
            rows = pl.ds(pl.multiple_of(ci * c, c), c)
            for hd in range(nh):
                ks = slice(hd * dk, (hd + 1) * dk)
                vs = slice(hd * dv, (hd + 1) * dv)
                q = q_ref[rows, ks] * scale
                k = k_ref[rows, ks]
                v = v_ref[rows, vs].astype(BF16)
                bc = la_ref[rows, ks]
                s = 1
                while s < c:
                    bc = bc + _shift_rows(bc, s, row, 0.0)
                    s *= 2
                b_last = bc[c - 1:c, :]
                qe = (q * jnp.exp(bc)).astype(BF16)
                ke = (k * jnp.exp(-bc)).astype(BF16)
                kd = (k * jnp.exp(b_last - bc)).astype(BF16)
                st = st_ref[hd]
                o = lax.dot_general(qe, st.astype(BF16), (((1,), (1,)), ((), ())),
                                    preferred_element_type=F32)
                attn = lax.dot_general(qe, ke, (((1,), (1,)), ((), ())),
                                       preferred_element_type=F32)
                attn = jnp.where(causal, attn, 0.0).astype(BF16)
                o = o + _dot(attn, v)
                st_ref[hd] = st * jnp.exp(b_last) + lax.dot_general(
                    v, kd, (((0,), (0,)), ((), ())), preferred_element_type=F32)
                y_ref[rows, vs] = _gla_out(o, gn_ref[...], g_ref[rows, vs]).astype(BF16)
            return carry

        lax.fori_loop(0, nc, chunk, 0)

        @pl.when(tbi == nt - 1)
        def _():
            for hd in range(nh):
                s_ref[hd] = st_ref[hd].T

    m = z.shape[0]
    return pl.pallas_call(
        kern, grid=(bsz, nt),
        in_specs=[pl.BlockSpec((tb, hk), lambda b_, i: (b_ * nt + i, cq)),
                  pl.BlockSpec((tb, hk), lambda b_, i: (b_ * nt + i, ck)),
                  pl.BlockSpec((tb, hv), lambda b_, i: (b_ * nt + i, cv)),
                  pl.BlockSpec((tb, hv), lambda b_, i: (b_ * nt + i, cg)),
                  pl.BlockSpec((tb, hk), lambda b_, i: (b_ * nt + i, 0)),
                  pl.BlockSpec((1, dv), lambda b_, i: (0, 0)),
                  pl.BlockSpec(memory_space=pl.ANY)],
        out_specs=[pl.BlockSpec((tb, hv), lambda b_, i: (b_ * nt + i, col_y // hv)),
                   pl.BlockSpec((None, nh, dk, dv), lambda b_, i: (b_, 0, 0, 0))],
        out_shape=[SDS(y_mix.shape, BF16), SDS((bsz, nh, dk, dv), F32)],
        scratch_shapes=[pltpu.VMEM((nh, dv, dk), F32)],
        input_output_aliases={6: 0},
        compiler_params=_cparams(2), name="gla_prompt")(z, z, z, z, la, gn, y_mix)


def _gla_sample(z, la, y_mix, st, gn, mp, ms, nh, dk, dv, col_q, col_y):
    bb = 16
    assert ms % bb == 0 and mp % ms == 0 and dk == LANE
    ns = ms // bb
    hk, hv = nh * dk, nh * dv
    cq, ck = col_q // hk, col_q // hk + 1
    cv, cg = (col_q + 2 * hk) // hv, (col_q + 2 * hk) // hv + 1
    scale = dk ** -0.5

    def kern(q_ref, k_ref, la_ref, v_ref, g_ref, gn_ref, st_ref, y_in_ref, y_ref, so_ref,
             qt_ref, kt_ref, at_ref):
        del y_in_ref
        i = pl.program_id(0)

        @pl.when(i == 0)
        def _():
            for hd in range(nh):
                ks = slice(hd * dk, (hd + 1) * dk)
                qt = (q_ref[:, ks] * scale).T
                kt = k_ref[:, ks].T
                at = jnp.exp(la_ref[:, ks]).T
                for s in range(ns):
                    qt_ref[s, hd] = qt[:, s * bb:(s + 1) * bb]
                    kt_ref[s, hd] = kt[:, s * bb:(s + 1) * bb]
                    at_ref[s, hd] = at[:, s * bb:(s + 1) * bb]

        for hd in range(nh):
            vs = slice(hd * dv, (hd + 1) * dv)
            qt = qt_ref[i, hd]
            kt = kt_ref[i, hd]
            at = at_ref[i, hd]
            outs = []
            for j in range(bb):
                v = v_ref[j:j + 1, vs]
                s_new = at[:, j:j + 1] * st_ref[j, hd] + kt[:, j:j + 1] * v
                so_ref[j, hd] = s_new
                outs.append(jnp.sum(qt[:, j:j + 1] * s_new, axis=0, keepdims=True))
            o = jnp.concatenate(outs, axis=0)
            y_ref[:, vs] = _gla_out(o, gn_ref[...], g_ref[:, vs]).astype(BF16)

    rb = mp // ms
    rbb = mp // bb
    return pl.pallas_call(
        kern, grid=(ns,),
        in_specs=[pl.BlockSpec((ms, hk), lambda i: (rb, cq)),
                  pl.BlockSpec((ms, hk), lambda i: (rb, ck)),
                  pl.BlockSpec((ms, hk), lambda i: (rb, 0)),
                  pl.BlockSpec((bb, hv), lambda i: (rbb + i, cv)),
                  pl.BlockSpec((bb, hv), lambda i: (rbb + i, cg)),
                  pl.BlockSpec((1, dv), lambda i: (0, 0)),
                  pl.BlockSpec((None, bb, nh, dk, dv), lambda i: (0, i, 0, 0, 0)),
                  pl.BlockSpec(memory_space=pl.ANY)],
        out_specs=[pl.BlockSpec((bb, hv), lambda i: (rbb + i, col_y // hv)),
                   pl.BlockSpec((None, bb, nh, dk, dv), lambda i: (0, i, 0, 0, 0))],
        out_shape=[SDS(y_mix.shape, BF16), SDS((1, ms, nh, dk, dv), F32)],
        scratch_shapes=[pltpu.VMEM((ns, nh, dk, bb), F32)] * 3,
        input_output_aliases={7: 0},
        compiler_params=_cparams(1), name="gla_sample")(z, z, la, z, z, gn, st, y_mix)


def _cast_bf16(w3):
    _, k, n = w3.shape
    tk = _tile(k, 512, 16)

    def kern(w_ref, o_ref):
        o_ref[...] = w_ref[...].astype(BF16)

    return pl.pallas_call(
        kern, grid=(k // tk,),
        in_specs=[pl.BlockSpec((None, tk, n), lambda i: (0, i, 0))],
        out_specs=pl.BlockSpec((tk, n), lambda i: (i, 0)),
        out_shape=SDS((k, n), BF16),
        compiler_params=_cparams(1), name="cast_bf16")(w3)


def _route(hn, rw):
    logits = _dot(hn.astype(BF16), rw.astype(BF16))
    ne = float(logits.shape[1])
    lane = lax.broadcasted_iota(I32, logits.shape, 1).astype(F32)
    m1 = jnp.max(logits, axis=1, keepdims=True)
    i1 = jnp.min(jnp.where(logits == m1, lane, ne), axis=1, keepdims=True)
    sel1 = lane == i1
    rest = jnp.where(sel1, -jnp.inf, logits)
    m2 = jnp.max(rest, axis=1, keepdims=True)
    i2 = jnp.min(jnp.where(rest == m2, lane, ne), axis=1, keepdims=True)
    sel2 = lane == i2
    e2 = jnp.exp(m2 - m1)
    g1 = 1.0 / (1.0 + e2)
    g2 = e2 / (1.0 + e2)
    mh = jnp.where(sel1 | sel2, 1.0, 0.0)
    gd = jnp.where(sel1, g1, 0.0) + jnp.where(sel2, g2, 0.0)
    return mh, gd


def _out_proj(y, wb, res, g, tm, res_tail=None, router_w=None):
    m, k = y.shape
    d = wb.shape[1]
    n_m = m // tm
    split = res_tail is not None
    if split:
        mp, ms = res.shape[0], res_tail.shape[0]
        assert mp + ms == m and ms <= tm
        head = tm - ms
        n_rb = pl.cdiv(mp, tm)
    route = router_w is not None

    def kern(*refs):
        it = iter(refs)
        y_ref, w_ref, r_ref = next(it), next(it), next(it)
        rt_ref = next(it) if split else None
        g_ref = next(it)
        rw_ref = next(it) if route else None
        h_ref, n_ref = next(it), next(it)
        mh_ref, gd_ref = (next(it), next(it)) if route else (None, None)
        acc_ref = next(it)
        i = pl.program_id(0)

        def matmul():
            acc_ref[i & 1] = _dot(y_ref[...], w_ref[...])

        def epilogue():
            acc = acc_ref[(i - 1) & 1]
            if split:
                last = i == n_m
                tail = acc[head:] + jnp.where(last, rt_ref[...], r_ref[head:tm, :])
                h = tail if head == 0 else jnp.concatenate([acc[:head] + r_ref[0:head, :], tail], 0)
            else:
                h = acc + r_ref[...]
            h_ref[...] = h
            hn = _rms(h, g_ref[...])
            if route:
                n_ref[...] = hn
                mh, gd = _route(hn, rw_ref[...])
                mh_ref[...] = mh
                gd_ref[...] = gd
            else:
                n_ref[...] = hn.astype(BF16)

        @pl.when(i < n_m)
        def _():
            matmul()

        @pl.when(i > 0)
        def _():
            epilogue()

    def lag(i):
        return jnp.maximum(i - 1, 0)

    in_specs = [pl.BlockSpec((tm, k), lambda i: (jnp.minimum(i, n_m - 1), 0)),
                pl.BlockSpec((k, d), lambda i: (0, 0))]
    args = [y, wb]
    if split:
        in_specs += [pl.BlockSpec((tm, d), lambda i: (jnp.minimum(lag(i), n_rb - 1), 0)),
                     pl.BlockSpec((ms, d), lambda i: (0, 0))]
        args += [res, res_tail]
    else:
        in_specs += [pl.BlockSpec((tm, d), lambda i: (lag(i), 0))]
        args += [res]
    in_specs += [pl.BlockSpec((1, d), lambda i: (0, 0))]
    args += [g]
    out_specs = [pl.BlockSpec((tm, d), lambda i: (lag(i), 0))]
    out_shape = [SDS((m, d), F32)]
    if route:
        ne = router_w.shape[1]
        in_specs += [pl.BlockSpec((d, ne), lambda i: (0, 0))]
        args += [router_w]
        out_specs += [pl.BlockSpec((tm, d), lambda i: (lag(i), 0)),
                      pl.BlockSpec((tm, ne), lambda i: (lag(i), 0)),
                      pl.BlockSpec((tm, ne), lambda i: (lag(i), 0))]
        out_shape += [SDS((m, d), F32), SDS((m, ne), F32), SDS((m, ne), F32)]
    else:
        out_specs += [pl.BlockSpec((tm, d), lambda i: (lag(i), 0))]
        out_shape += [SDS((m, d), BF16)]
    return pl.pallas_call(
        kern, grid=(n_m + 1,), in_specs=in_specs, out_specs=out_specs, out_shape=out_shape,
        scratch_shapes=[pltpu.VMEM((2, tm, d), F32)],
        compiler_params=_cparams(1), name="out_proj_route" if route else "out_proj")(*args)


def _ffn_residual(x, wg3, wu3, wd3, h, tm, tf):
    m, k = x.shape
    f = wg3.shape[2]
    d = wd3.shape[2]

    def kern(x_ref, wg_ref, wu_ref, wd_ref, h_ref, o_ref, sem):
        i = pl.program_id(0)
        j = pl.program_id(1)
        res = pltpu.make_async_copy(h_ref.at[pl.ds(pl.multiple_of(i * tm, tm), tm)], o_ref, sem)

        @pl.when(j == 0)
        def _():
            res.start()

        xb = x_ref[...]
        gg = _dot(xb, wg_ref[...].astype(BF16))
        uu = _dot(xb, wu_ref[...].astype(BF16))
        a = (gg * _sigmoid(gg) * uu).astype(BF16)

        @pl.when(j == 0)
        def _():
            res.wait()

        o_ref[...] += _dot(a, wd_ref[...].astype(BF16))

    return pl.pallas_call(
        kern, grid=(m // tm, f // tf),
        in_specs=[pl.BlockSpec((tm, k), lambda i, j: (i, 0)),
                  pl.BlockSpec((None, k, tf), lambda i, j: (0, 0, j)),
                  pl.BlockSpec((None, k, tf), lambda i, j: (0, 0, j)),
                  pl.BlockSpec((None, tf, d), lambda i, j: (0, j, 0)),
                  pl.BlockSpec(memory_space=pl.ANY)],
        out_specs=pl.BlockSpec((tm, d), lambda i, j: (i, 0)),
        out_shape=SDS((m, d), F32),
        scratch_shapes=[pltpu.SemaphoreType.DMA(())],
        compiler_params=_cparams(2), name="ffn")(x, wg3, wu3, wd3, h)


def _norm(h, g, tm):
    m, d = h.shape

    def kern(h_ref, g_ref, n_ref):
        n_ref[...] = _rms(h_ref[...], g_ref[...]).astype(BF16)

    row = pl.BlockSpec((tm, d), lambda i: (i, 0))
    return pl.pallas_call(
        kern, grid=(m // tm,),
        in_specs=[row, pl.BlockSpec((1, d), lambda i: (0, 0))],
        out_specs=row,
        out_shape=SDS((m, d), BF16),
        compiler_params=_cparams(1), name="norm")(h, g)


def _in_proj1(x, w3, tm, tn):
    m, k = x.shape
    dc = w3.shape[2] // 3
    nb = dc // tn

    def kern(x_ref, wb_ref, wc_ref, wv_ref, gb_ref, cv_ref, sb_ref, sc_ref, sv_ref):
        @pl.when(pl.program_id(1) == 0)
        def _():
            sb_ref[...] = wb_ref[...].astype(BF16)
            sc_ref[...] = wc_ref[...].astype(BF16)
            sv_ref[...] = wv_ref[...].astype(BF16)

        xb = x_ref[...]
        gb_ref[...] = _dot(xb, sb_ref[...]).astype(BF16)
        cv_ref[...] = _dot(xb, sc_ref[...]) * _dot(xb, sv_ref[...])

    def wspec(g):
        return pl.BlockSpec((None, k, tn), lambda n, i: (0, 0, g * nb + n))

    return pl.pallas_call(
        kern, grid=(nb, m // tm),
        in_specs=[pl.BlockSpec((tm, k), lambda n, i: (i, 0)), wspec(0), wspec(1), wspec(2)],
        out_specs=[pl.BlockSpec((tm, tn), lambda n, i: (i, n))] * 2,
        out_shape=[SDS((m, dc), BF16), SDS((m, dc), F32)],
        scratch_shapes=[pltpu.VMEM((k, tn), BF16)] * 3,
        compiler_params=_cparams(2), name="in_proj1")(x, w3, w3, w3)


def _shortconv_prompt(gb, cv, cw, bsz, t):
    m, dc = cv.shape
    tc = _tile(dc, 512, LANE)
    nw = cw.shape[0]

    def kern(gb_ref, cv_ref, cw_ref, u_ref, buf_ref):
        x = cv_ref[...]
        row = lax.broadcasted_iota(I32, x.shape, 0)
        w = cw_ref[...]
        u = w[nw - 1:nw, :] * x
        for s in range(1, nw):
            u = u + w[nw - 1 - s:nw - s, :] * _shift_rows(x, s, row, 0.0)
        u_ref[...] = (gb_ref[...].astype(F32) * u).astype(BF16)
        buf_ref[...] = x[t - (nw - 1):t, :]

    blk = pl.BlockSpec((t, tc), lambda b_, c: (b_, c))
    return pl.pallas_call(
        kern, grid=(bsz, dc // tc),
        in_specs=[blk, blk, pl.BlockSpec((nw, tc), lambda b_, c: (0, c))],
        out_specs=[blk, pl.BlockSpec((None, nw - 1, tc), lambda b_, c: (b_, 0, c))],
        out_shape=[SDS((m, dc), BF16), SDS((bsz, nw - 1, dc), F32)],
        compiler_params=_cparams(2), name="shortconv_prompt")(gb, cv, cw)


def _shortconv_sample(gb, cv, u_all, st, cw, mp, ms):
    dc = cv.shape[1]
    tc = _tile(dc, 512, LANE)
    nw = cw.shape[0]
    rb = mp // ms

    def kern(gb_ref, cv_ref, u_in_ref, st_ref, cw_ref, u_ref, buf_ref):
        del u_in_ref
        x = cv_ref[...]
        w = cw_ref[...]
        u = w[nw - 1:nw, :] * x
        for s in range(nw - 1):
            u = u + w[s:s + 1, :] * st_ref[:, s, :]
        u_ref[...] = (gb_ref[...].astype(F32) * u).astype(BF16)
        for s in range(nw - 2):
            buf_ref[:, s, :] = st_ref[:, s + 1, :]
        buf_ref[:, nw - 2, :] = x

    blk = pl.BlockSpec((ms, tc), lambda c: (rb, c))
    stb = pl.BlockSpec((ms, nw - 1, tc), lambda c: (0, 0, c))
    return pl.pallas_call(
        kern, grid=(dc // tc,),
        in_specs=[blk, blk, pl.BlockSpec(memory_space=pl.ANY), stb,
                  pl.BlockSpec((nw, tc), lambda c: (0, c))],
        out_specs=[blk, stb],
        out_shape=[SDS(u_all.shape, BF16), SDS((ms, nw - 1, dc), F32)],
        input_output_aliases={2: 0},
        compiler_params=_cparams(1), name="shortconv_sample")(gb, cv, u_all, st, cw)


def _moe_rank(mh, tm):
    m, ne = mh.shape

    def kern(mh_ref, ex_ref, cnt_ref, carry_ref):
        @pl.when(pl.program_id(0) == 0)
        def _():
            carry_ref[...] = jnp.zeros_like(carry_ref)

        x = mh_ref[...]
        tri = (lax.broadcasted_iota(I32, (tm, tm), 0) > lax.broadcasted_iota(I32, (tm, tm), 1))
        ex = _dot(jnp.where(tri, 1.0, 0.0).astype(BF16), x.astype(BF16)) + carry_ref[...]
        ex_ref[...] = ex
        tot = ex[tm - 1:tm, :] + x[tm - 1:tm, :]
        carry_ref[...] = tot
        cnt_ref[...] = tot

    return pl.pallas_call(
        kern, grid=(m // tm,),
        in_specs=[pl.BlockSpec((tm, ne), lambda i: (i, 0))],
        out_specs=[pl.BlockSpec((tm, ne), lambda i: (i, 0)), pl.BlockSpec((1, ne), lambda i: (0, 0))],
        out_shape=[SDS((m, ne), F32), SDS((1, ne), F32)],
        scratch_shapes=[pltpu.VMEM((1, ne), F32)],
        compiler_params=_cparams(1), name="moe_rank")(mh)


def _moe_pos(mh, gd, ex, off, tm):
    m, ne = mh.shape

    def kern(mh_ref, gd_ref, ex_ref, off_ref, pos_ref, gate_ref):
        sel = mh_ref[...] > 0.5
        pd = ex_ref[...] + off_ref[...]
        big = jnp.float32(3e38)
        p_lo = jnp.min(jnp.where(sel, pd, big), axis=1, keepdims=True)
        p_hi = jnp.max(jnp.where(sel, pd, -big), axis=1, keepdims=True)
        gdv = gd_ref[...]
        g_lo = jnp.sum(jnp.where(sel & (pd == p_lo), gdv, 0.0), axis=1, keepdims=True)
        g_hi = jnp.sum(jnp.where(sel & (pd == p_hi), gdv, 0.0), axis=1, keepdims=True)
        pos_ref[:, 0:1] = p_lo.astype(I32)
        pos_ref[:, 1:2] = p_hi.astype(I32)
        gate_ref[:, 0:1] = g_lo
        gate_ref[:, 1:2] = g_hi

    blk = pl.BlockSpec((tm, ne), lambda i: (i, 0))
    two = pl.BlockSpec((tm, TOP_K), lambda i: (i, 0))
    return pl.pallas_call(
        kern, grid=(m // tm,),
        in_specs=[blk, blk, blk, pl.BlockSpec((1, ne), lambda i: (0, 0))],
        out_specs=[two, two],
        out_shape=[SDS((m, TOP_K), I32), SDS((m, TOP_K), F32)],
        compiler_params=_cparams(1), name="moe_pos")(mh, gd, ex, off)


def _moe_scatter(pos_flat, pad_rows, n_pad, x, n_rows, tm):
    m, w = x.shape

    def kern(pos_ref, pad_ref, npad_ref, x_ref, xs_ref, zero_ref, sem, zsem):
        i = pl.program_id(0)

        def row_copy(r, p):
            return pltpu.make_async_copy(x_ref.at[pl.ds(r, 1)], xs_ref.at[pl.ds(p, 1)], sem)

        def issue(r, c):
            t = i * tm + r
            row_copy(r, pos_ref[TOP_K * t]).start()
            row_copy(r, pos_ref[TOP_K * t + 1]).start()
            return c

        lax.fori_loop(0, tm, issue, 0, unroll=8)

        @pl.when(i == 0)
        def _():
            zero_ref[...] = jnp.zeros_like(zero_ref)
            npad = npad_ref[0]

            def zcopy(j):
                return pltpu.make_async_copy(zero_ref.at[pl.ds(0, 1)],
                                             xs_ref.at[pl.ds(pad_ref[j], 1)], zsem)

            def zissue(j, c):
                zcopy(j).start()
                return c

            def zwait(j, c):
                zcopy(j).wait()
                return c

            lax.fori_loop(0, npad, zissue, 0)
            lax.fori_loop(0, npad, zwait, 0)

        def drain(r, c):
            row_copy(0, 0).wait()
            row_copy(0, 0).wait()
            return c

        lax.fori_loop(0, tm, drain, 0, unroll=8)

    return pl.pallas_call(
        kern,
        grid_spec=pltpu.PrefetchScalarGridSpec(
            num_scalar_prefetch=3, grid=(m // tm,),
            in_specs=[pl.BlockSpec((tm, w), lambda i, *_: (i, 0))],
            out_specs=pl.BlockSpec(memory_space=pl.ANY),
            scratch_shapes=[pltpu.VMEM((8, w), F32), pltpu.SemaphoreType.DMA(()),
                            pltpu.SemaphoreType.DMA(())]),
        out_shape=SDS((n_rows, w), F32),
        compiler_params=_cparams(1), name="moe_scatter")(pos_flat, pad_rows, n_pad, x)


def _moe_ffn(st_e, st_row, st_nsub, xs, wg4, wu4, wd4, rs, sub, tf):
    n_rows, k = xs.shape
    f = wg4.shape[3]
    d = wd4.shape[3]
    ng = st_e.shape[0]
    nf = f // tf
    assert k == d and nf >= 2
    kc = _tile(k, 512, LANE)
    big, mid = 4 * sub, 2 * sub

    def kern(se_ref, sr_ref, sn_ref, xs_ref, wg_ref, wu_ref, wd_ref, ys_ref,
             x_ref, acc_ref, a_ref, wgb_ref, wub_ref, wdb_ref, sem_in, sem_out):
        g = pl.program_id(0)
        j = pl.program_id(1)
        nsub = sn_ref[g]
        row0 = sr_ref[g]
        nbig = lax.shift_right_logical(nsub, 2)
        has_mid = (nsub & 2) != 0
        has_small = (nsub & 1) != 0
        start_mid = nbig * big
        start_small = start_mid + jnp.where(has_mid, mid, 0)

        def rows(start, size):
            return pl.ds(pl.multiple_of(start, sub), size)

        def in_copy(s):
            return pltpu.make_async_copy(xs_ref.at[rows(row0 + s * sub, sub)],
                                         acc_ref.at[rows(s * sub, sub)], sem_in)

        def out_copy(start, size):
            return pltpu.make_async_copy(acc_ref.at[rows(start, size)],
                                         ys_ref.at[rows(row0 + start, size)], sem_out)

        def each_sub(fn):
            def body(s, c):
                fn(s)
                return c
            lax.fori_loop(0, nsub, body, 0)

        def cast_weights():
            for c in range(k // kc):
                ks = slice(c * kc, (c + 1) * kc)
                wgb_ref[ks, :] = wg_ref[ks, :].astype(BF16)
                wub_ref[ks, :] = wu_ref[ks, :].astype(BF16)
            wdb_ref[...] = wd_ref[...].astype(BF16)

        def up(start, size, slot):
            xb = x_ref[rows(start, size), :]
            gg = _dot(xb, wgb_ref[...])
            uu = _dot(xb, wub_ref[...])
            a_ref[slot, 0:size, :] = (gg * _sigmoid(gg) * uu).astype(BF16)

        def down(start, size, slot, first, last):
            dd = _dot(a_ref[slot, 0:size, :], wdb_ref[...])
            if first:
                acc_ref[rows(start, size), :] = dd
            else:
                acc_ref[rows(start, size), :] += dd
            if last:
                out_copy(start, size).start()

        def compute(first, last, cast):
            @pl.when(nbig > 0)
            def _():
                if cast:
                    cast_weights()
                up(0, big, 0)

                def body(s, c):
                    up(s * big, big, s & 1)
                    down((s - 1) * big, big, (s - 1) & 1, first, last)
                    return c

                lax.fori_loop(1, nbig, body, 0)
                down((nbig - 1) * big, big, (nbig - 1) & 1, first, last)

            if cast:
                @pl.when(nbig == 0)
                def _():
                    cast_weights()

            @pl.when(has_mid)
            def _():
                up(start_mid, mid, 0)
                down(start_mid, mid, 0, first, last)

            @pl.when(has_small)
            def _():
                up(start_small, sub, 0)
                down(start_small, sub, 0, first, last)

        @pl.when(nsub > 0)
        def _():
            @pl.when(j == 0)
            def _():
                each_sub(lambda s: in_copy(s).start())
                cast_weights()

                def to_bf16(s):
                    x_ref[rows(s * sub, sub), :] = acc_ref[rows(s * sub, sub), :].astype(BF16)

                each_sub(lambda s: in_copy(s).wait())
                each_sub(to_bf16)
                compute(True, False, False)

            @pl.when((j > 0) & (j < nf - 1))
            def _():
                compute(False, False, True)

            @pl.when(j == nf - 1)
            def _():
                compute(False, True, True)

                def wait_big(s, c):
                    out_copy(0, big).wait()
                    return c

                lax.fori_loop(0, nbig, wait_big, 0)

                @pl.when(has_mid)
                def _():
                    out_copy(0, mid).wait()

                @pl.when(has_small)
                def _():
                    out_copy(0, sub).wait()

    def widx(g, j, se, sr, sn):
        return (0, se[g], 0, jnp.where(sn[g] > 0, j, nf - 1))

    def didx(g, j, se, sr, sn):
        return (0, se[g], jnp.where(sn[g] > 0, j, nf - 1), 0)

    return pl.pallas_call(
        kern,
        grid_spec=pltpu.PrefetchScalarGridSpec(
            num_scalar_prefetch=3, grid=(ng, nf),
            in_specs=[pl.BlockSpec(memory_space=pl.ANY),
                      pl.BlockSpec((None, None, k, tf), widx),
                      pl.BlockSpec((None, None, k, tf), widx),
                      pl.BlockSpec((None, None, tf, d), didx)],
            out_specs=pl.BlockSpec(memory_space=pl.ANY),
            scratch_shapes=[pltpu.VMEM((rs, k), BF16), pltpu.VMEM((rs, d), F32),
                            pltpu.VMEM((2, 4 * sub, tf), BF16),
                            pltpu.VMEM((k, tf), BF16), pltpu.VMEM((k, tf), BF16),
                            pltpu.VMEM((tf, d), BF16),
                            pltpu.SemaphoreType.DMA(()), pltpu.SemaphoreType.DMA(())]),
        out_shape=SDS((n_rows, d), F32),
        compiler_params=_cparams(2), name="moe_ffn")(st_e, st_row, st_nsub, xs, wg4, wu4, wd4)


def _moe_combine(pos_flat, h, gates, ys, g, mp, ms, tp):
    m, d = h.shape
    n_p = mp // tp
    assert mp % tp == 0 and ms <= tp and ms % 8 == 0

    def kern(pos_ref, h_ref, gate_ref, ys_ref, g_ref, yp_ref, ysm_ref, a_ref, b_ref, sem):
        i = pl.program_id(0)

        def copies(tile, r):
            t = tile * tp + r
            slot = tile & 1
            return (pltpu.make_async_copy(ys_ref.at[pl.ds(pos_ref[TOP_K * t], 1)],
                                          a_ref.at[slot, pl.ds(r, 1)], sem.at[slot]),
                    pltpu.make_async_copy(ys_ref.at[pl.ds(pos_ref[TOP_K * t + 1], 1)],
                                          b_ref.at[slot, pl.ds(r, 1)], sem.at[slot]))

        def gather(tile):
            def issue(r, c):
                ca, cb = copies(tile, r)
                ca.start()
                cb.start()
                return c

            @pl.when(tile < n_p)
            def _():
                lax.fori_loop(0, tp, issue, 0, unroll=8)

            @pl.when(tile == n_p)
            def _():
                lax.fori_loop(0, ms, issue, 0, unroll=8)

        def combined(n):
            def drain(r, c):
                ca, cb = copies(i, 0)
                ca.wait()
                cb.wait()
                return c

            lax.fori_loop(0, n, drain, 0, unroll=8)
            slot = i & 1
            gt = gate_ref[0:n, :]
            hh = (h_ref[0:n, :] + gt[:, 0:1] * a_ref[slot, 0:n, :]
                  + gt[:, 1:2] * b_ref[slot, 0:n, :])
            return _rms(hh, g_ref[...])

        @pl.when(i == 0)
        def _():
            gather(i)

        gather(i + 1)

        @pl.when(i < n_p)
        def _():
            yp_ref[...] = combined(tp)

        @pl.when(i == n_p)
        def _():
            ysm_ref[...] = combined(ms)

    return pl.pallas_call(
        kern,
        grid_spec=pltpu.PrefetchScalarGridSpec(
            num_scalar_prefetch=1, grid=(n_p + 1,),
            in_specs=[pl.BlockSpec((tp, d), lambda i, *_: (i, 0)),
                      pl.BlockSpec((tp, TOP_K), lambda i, *_: (i, 0)),
                      pl.BlockSpec(memory_space=pl.ANY),
                      pl.BlockSpec((1, d), lambda i, *_: (0, 0))],
            out_specs=[pl.BlockSpec((tp, d), lambda i, *_: (jnp.minimum(i, n_p - 1), 0)),
                       pl.BlockSpec((ms, d), lambda i, *_: (0, 0))],
            scratch_shapes=[pltpu.VMEM((2, tp, d), F32), pltpu.VMEM((2, tp, d), F32),
                            pltpu.SemaphoreType.DMA((2,))]),
        out_shape=[SDS((mp, d), F32), SDS((ms, d), F32)],
        compiler_params=_cparams(1), name="moe_combine")(pos_flat, h, gates, ys, g)


def _moe_tables(cnt, sub, rs, ng):
    ne = cnt.shape[0]
    nsub_e = (cnt + sub - 1) // sub
    size_e = nsub_e * sub
    off = jnp.cumsum(size_e) - size_e
    spr = rs // sub
    nst_e = (nsub_e + spr - 1) // spr
    st_start = jnp.cumsum(nst_e) - nst_e
    n_act = jnp.sum(nst_e)
    gidx = jnp.arange(ng, dtype=I32)
    e_of = jnp.sum((gidx[:, None] >= st_start[None, :]).astype(I32), axis=1) - 1
    e_of = jnp.clip(e_of, 0, ne - 1)
    kth = gidx - st_start[e_of]
    active = gidx < n_act
    nsub = jnp.where(active, jnp.clip(nsub_e[e_of] - kth * spr, 0, spr), 0)
    row = off[e_of] + kth * rs
    last = jnp.maximum(n_act - 1, 0)
    st_e = jnp.where(active, e_of, e_of[last]).astype(I32)
    st_row = jnp.where(active, row, 0).astype(I32)
    n_pad_e = size_e - cnt
    pad_start = jnp.cumsum(n_pad_e) - n_pad_e
    n_pad = jnp.sum(n_pad_e)
    pidx = jnp.arange(ne * (sub - 1), dtype=I32)
    pe = jnp.clip(jnp.sum((pidx[:, None] >= pad_start[None, :]).astype(I32), axis=1) - 1, 0, ne - 1)
    pad_rows = jnp.where(pidx < n_pad, off[pe] + cnt[pe] + (pidx - pad_start[pe]), 0).astype(I32)
    return off, st_e, st_row, nsub.astype(I32), pad_rows, n_pad.astype(I32).reshape(1)


def kernel(x_prompt, x_sample, state_rg_conv, state_rg_h, state_gla, state_sc_conv, norm_mix_e, w_in_e, rg_conv_w, rg_conv_b, rg_w_a, rg_b_a, rg_w_x, rg_b_x, rg_lambda, gla_w_gate, gla_b_gate, gla_norm, w_out_e, norm_ffn_e, ffn_w_gate, ffn_w_up, ffn_w_down, norm_mix_o, w_in_o, sc_conv_w, w_out_o, norm_ffn_o, router_w, moe_w_gate, moe_w_up, moe_w_down, final_norm):
    bsz, t, d = x_prompt.shape
    ms = x_sample.shape[0]
    assert x_sample.shape[1] == 1 and w_in_e.shape[0] == 1 and w_in_o.shape[0] == 1
    mp = bsz * t
    m = mp + ms
    d_rnn = rg_lambda.shape[1]
    nh, dk, dv = state_gla.shape[2:]
    hk, hv = nh * dk, nh * dv
    n_main = 2 * d_rnn + 2 * hk + 2 * hv
    d_mix = d_rnn + hv
    ne = router_w.shape[2]
    xp = x_prompt.reshape(mp, d)
    xs = x_sample.reshape(ms, d)
    row = lambda v: v.reshape(1, -1)

    tp = _tile(mp, TP_TARGET, 16)
    tm = _tile(m, TM_TARGET, 16)
    tm_in = _tile(m, TM_IN_TARGET, 16)
    tm_out = _tile(m, TM_OUT_TARGET, 16)

    hn0 = _norm_in(xp, xs, norm_mix_e, tp)
    w_in_t = jnp.swapaxes(w_in_e, 1, 2)
    z0 = _in_proj0(hn0, w_in_t, n_main, tm_in, _tile(n_main, 1024, LANE))
    la = _gla_gate(hn0, w_in_t, n_main, gla_w_gate[0], gla_b_gate, tm)
    y_mix, rgc_p, rgh_p = _rglru_prompt(z0, rg_conv_w[0], rg_conv_b, rg_w_a[0], rg_b_a, rg_w_x[0],
                                        rg_b_x, rg_lambda, bsz, t, d_rnn, d_mix)
    y_mix, rgc_s, rgh_s = _rglru_sample(z0, y_mix, state_rg_conv[0], state_rg_h[0], rg_conv_w[0],
                                        rg_conv_b, rg_w_a[0], rg_b_a, rg_w_x[0], rg_b_x, rg_lambda,
                                        mp, ms, d_rnn)
    y_mix, gla_p = _gla_prompt(z0, la, y_mix, gla_norm, bsz, t, nh, dk, dv, 2 * d_rnn, d_rnn)
    y_mix, gla_s = _gla_sample(z0, la, y_mix, state_gla, gla_norm, mp, ms, nh, dk, dv,
                               2 * d_rnn, d_rnn)
    h1, hn1 = _out_proj(y_mix, _cast_bf16(w_out_e), xp, norm_ffn_e, tm_out, res_tail=xs)

    h2 = _ffn_residual(hn1, ffn_w_gate, ffn_w_up, ffn_w_down, h1, tm,
                       _tile(ffn_w_gate.shape[2], 512, LANE))
    hn2 = _norm(h2, norm_mix_o, tm_in)

    gb, cv = _in_proj1(hn2, w_in_o, tm_in, _tile(w_in_o.shape[2] // 3, 256, LANE))
    u, sc_p = _shortconv_prompt(gb, cv, sc_conv_w[0], bsz, t)
    u, sc_s = _shortconv_sample(gb, cv, u, state_sc_conv[0], sc_conv_w[0], mp, ms)
    h3, hn3, mh, gd = _out_proj(u, _cast_bf16(w_out_o), h2, norm_ffn_o, tm_out,
                                router_w=router_w[0])

    sub = MOE_SUB
    rs = MOE_SPR * sub
    ng = (TOP_K * m) // rs + ne
    n_rows = TOP_K * m + ne * (sub - 1)
    n_rows = ((n_rows + sub - 1) // sub) * sub
    ex, cnt = _moe_rank(mh, tm)
    off, st_e, st_row, st_nsub, pad_rows, n_pad = _moe_tables(cnt[0].astype(I32), sub, rs, ng)
    pos, gates = _moe_pos(mh, gd, ex, off.astype(F32).reshape(1, ne), tm)
    pos_flat = pos.reshape(TOP_K * m)
    xsort = _moe_scatter(pos_flat, pad_rows, n_pad, hn3, n_rows, tm)
    ys = _moe_ffn(st_e, st_row, st_nsub, xsort, moe_w_gate, moe_w_up, moe_w_down, rs, sub,
                  _tile(moe_w_gate.shape[3], 256, LANE))
    y_p, y_s = _moe_combine(pos_flat, h3, gates, ys, row(final_norm), mp, ms, tp)

    return (y_p.reshape(bsz, t, d), y_s.reshape(ms, 1, d),
            rgc_p[None], rgc_s[None], rgh_p.reshape(1, bsz, d_rnn), rgh_s[None],
            gla_p[None], gla_s, sc_p[None], sc_s[None])
```

```python
import functools

import jax
import jax.numpy as jnp
from jax import lax
from jax.experimental import pallas as pl
from jax.experimental.pallas import tpu as pltpu

F32 = jnp.float32
BF16 = jnp.bfloat16
I32 = jnp.int32
SDS = jax.ShapeDtypeStruct

EPS = 1e-6
RG_C = 8.0
GLA_TAU = 16.0
GLA_CHUNK = 64
TOP_K = 2
LANE = 128
SUB8 = 8
VMEM_LIMIT = 56 * 1024 * 1024
ARB = "arbitrary"
TM_TARGET = 640
TM_IN_TARGET = 1664
TM_OUT_TARGET = 320
TP_TARGET = 512
GLA_TB_TARGET = 512
MOE_SUB = 128
MOE_SPR = 20


def _cparams(n_axes, vmem=VMEM_LIMIT):
    return pltpu.CompilerParams(dimension_semantics=(ARB,) * n_axes, vmem_limit_bytes=vmem)


def _tile(n, target, align):
    best = None
    for t in range(align, min(n, target) + 1, align):
        if n % t == 0:
            best = t
    assert best is not None, (n, target, align)
    return best


def _rms(xf, g):
    ms = jnp.mean(xf * xf, axis=-1, keepdims=True)
    return xf * lax.rsqrt(ms + EPS) * g


def _sigmoid(x):
    return 1.0 / (1.0 + jnp.exp(-x))


def _softplus(x):
    return jnp.maximum(x, 0.0) + jnp.log1p(jnp.exp(-jnp.abs(x)))


def _gelu_tanh(x):
    c = 0.7978845608028654
    return x * (0.5 * (1.0 + jnp.tanh(c * (x + 0.044715 * (x * x * x)))))


def _dot(a, b):
    return jnp.dot(a, b, preferred_element_type=F32)


def _shift_rows(x, s, row, fill):
    return jnp.where(row >= s, pltpu.roll(x, s, 0), fill)


def _norm_in(xp, xs, g, tp):
    mp, d = xp.shape
    ms = xs.shape[0]
    n_p = mp // tp
    assert mp % tp == 0 and ms <= tp and ms % 16 == 0

    def kern(xp_ref, xs_ref, g_ref, o_ref):
        i = pl.program_id(0)

        @pl.when(i < n_p)
        def _():
            o_ref[...] = _rms(xp_ref[...], g_ref[...]).astype(BF16)

        @pl.when(i == n_p)
        def _():
            o_ref[0:ms, :] = _rms(xs_ref[...], g_ref[...]).astype(BF16)

    return pl.pallas_call(
        kern, grid=(n_p + 1,),
        in_specs=[pl.BlockSpec((tp, d), lambda i: (jnp.minimum(i, n_p - 1), 0)),
                  pl.BlockSpec((ms, d), lambda i: (0, 0)),
                  pl.BlockSpec((1, d), lambda i: (0, 0))],
        out_specs=pl.BlockSpec((tp, d), lambda i: (i, 0)),
        out_shape=SDS((mp + ms, d), BF16),
        compiler_params=_cparams(1), name="norm_in")(xp, xs, g)


def _in_proj0(x, wt3, n_cols, tm, tn):
    m, k = x.shape

    def kern(x_ref, w_ref, o_ref, wb_ref):
        @pl.when(pl.program_id(1) == 0)
        def _():
            wb_ref[...] = w_ref[...].T.astype(BF16)

        o_ref[...] = _dot(x_ref[...], wb_ref[...])

    return pl.pallas_call(
        kern, grid=(n_cols // tn, m // tm),
        in_specs=[pl.BlockSpec((tm, k), lambda n, i: (i, 0)),
                  pl.BlockSpec((None, tn, k), lambda n, i: (0, n, 0))],
        out_specs=pl.BlockSpec((tm, tn), lambda n, i: (i, n)),
        out_shape=SDS((m, n_cols), F32),
        scratch_shapes=[pltpu.VMEM((k, tn), BF16)],
        compiler_params=_cparams(2), name="in_proj0")(x, wt3)


def _gla_gate(x, wt3, col0, w_gate, b_gate, tm):
    m, k = x.shape
    r, n = w_gate.shape
    assert col0 % r == 0

    def kern(x_ref, wl_ref, wg_ref, bg_ref, o_ref):
        lr = lax.dot_general(x_ref[...], wl_ref[...].astype(BF16), (((1,), (1,)), ((), ())),
                             preferred_element_type=F32)
        pre = _dot(lr.astype(BF16), wg_ref[...].astype(BF16)) + bg_ref[...]
        o_ref[...] = -_softplus(-pre) * (1.0 / GLA_TAU)

    return pl.pallas_call(
        kern, grid=(m // tm,),
        in_specs=[pl.BlockSpec((tm, k), lambda i: (i, 0)),
                  pl.BlockSpec((None, r, k), lambda i: (0, col0 // r, 0)),
                  pl.BlockSpec((r, n), lambda i: (0, 0)),
                  pl.BlockSpec((1, n), lambda i: (0, 0))],
        out_specs=pl.BlockSpec((tm, n), lambda i: (i, 0)),
        out_shape=SDS((m, n), F32),
        compiler_params=_cparams(1), name="gla_gate")(x, wt3, w_gate, b_gate)


def _rg_gates(xc, wa_ref, wx_ref, ba_ref, bx_ref, lam_ref):
    w2 = jnp.concatenate([wa_ref[...], wx_ref[...]], axis=1).astype(BF16)
    pre = _dot(xc.astype(BF16), w2)
    r = _sigmoid(pre[:, :LANE] + ba_ref[...])
    i = _sigmoid(pre[:, LANE:] + bx_ref[...])
    log_a = (-RG_C) * r * _softplus(-lam_ref[...])
    a = jnp.exp(log_a)
    mult = jnp.sqrt(jnp.tanh(-log_a) * (1.0 + a * a))
    return a, mult, i


def _rglru_prompt(z, cw, cb, wa, ba, wx, bx, lam, bsz, t, d_rnn, d_out):
    nh = d_rnn // LANE
    m = z.shape[0]

    def kern(xr_ref, gr_ref, cw_ref, cb_ref, wa_ref, ba_ref, wx_ref, bx_ref, lam_ref,
             y_ref, conv_ref, h_ref, a_s, b_s, c_s):
        xr = xr_ref[...]
        row = lax.broadcasted_iota(I32, (t, LANE), 0)
        w = cw_ref[...]
        nw = w.shape[0]
        xc = w[nw - 1:nw, :] * xr
        for s in range(1, nw):
            xc = xc + w[nw - 1 - s:nw - s, :] * _shift_rows(xr, s, row, 0.0)
        xc = xc + cb_ref[...]
        a, mult, i = _rg_gates(xc, wa_ref, wx_ref, ba_ref, bx_ref, lam_ref)
        mult = jnp.where(row == 0, 1.0, mult)
        b = mult * i * xc
        ng = t // SUB8
        a3, b3 = a.reshape(ng, SUB8, LANE), b.reshape(ng, SUB8, LANE)
        r8 = lax.broadcasted_iota(I32, (ng, SUB8, LANE), 1)
        s = 1
        while s < SUB8:
            b3 = a3 * jnp.where(r8 >= s, pltpu.roll(b3, s, 1), 0.0) + b3
            a3 = a3 * jnp.where(r8 >= s, pltpu.roll(a3, s, 1), 1.0)
            s *= 2
        a_s[...] = a3.reshape(t, LANE)
        b_s[...] = b3.reshape(t, LANE)
        ag = a_s[pl.ds(SUB8 - 1, ng, stride=SUB8), :]
        bg = b_s[pl.ds(SUB8 - 1, ng, stride=SUB8), :]
        rowg = lax.broadcasted_iota(I32, (ng, LANE), 0)
        s = 1
        while s < ng:
            bg = ag * _shift_rows(bg, s, rowg, 0.0) + bg
            ag = ag * _shift_rows(ag, s, rowg, 1.0)
            s *= 2
        carry = _shift_rows(bg, 1, rowg, 0.0)
        for k in range(SUB8):
            c_s[pl.ds(k, ng, stride=SUB8), :] = carry
        h = b_s[...] + a_s[...] * c_s[...]
        y_ref[...] = (h * _gelu_tanh(gr_ref[...])).astype(BF16)
        conv_ref[...] = xr[t - (nw - 1):t, :]
        h_ref[...] = h[t - 1:t, :]

    vec = pl.BlockSpec((1, LANE), lambda b_, h: (0, h))
    blk = pl.BlockSpec((None, LANE, LANE), lambda b_, h: (h, 0, 0))
    nw = cw.shape[0]
    return pl.pallas_call(
        kern, grid=(bsz, nh),
        in_specs=[pl.BlockSpec((t, LANE), lambda b_, h: (b_, h)),
                  pl.BlockSpec((t, LANE), lambda b_, h: (b_, nh + h)),
                  pl.BlockSpec((nw, LANE), lambda b_, h: (0, h)),
                  vec, blk, vec, blk, vec, vec],
        out_specs=[pl.BlockSpec((t, LANE), lambda b_, h: (b_, h)),
                   pl.BlockSpec((None, nw - 1, LANE), lambda b_, h: (b_, 0, h)),
                   pl.BlockSpec((None, 1, LANE), lambda b_, h: (b_, 0, h))],
        out_shape=[SDS((m, d_out), BF16), SDS((bsz, nw - 1, d_rnn), F32), SDS((bsz, 1, d_rnn), F32)],
        scratch_shapes=[pltpu.VMEM((t, LANE), F32)] * 3,
        compiler_params=_cparams(2), name="rglru_prompt")(z, z, cw, cb, wa, ba, wx, bx, lam)


def _rglru_sample(z, y_mix, st_conv, st_h, cw, cb, wa, ba, wx, bx, lam, mp, ms, d_rnn):
    nh = d_rnn // LANE
    nw = cw.shape[0]
    rb = mp // ms
    assert mp % ms == 0

    def kern(xr_ref, gr_ref, y_in_ref, sc_ref, sh_ref, cw_ref, cb_ref, wa_ref, ba_ref, wx_ref,
             bx_ref, lam_ref, y_ref, conv_ref, h_ref):
        del y_in_ref
        xr = xr_ref[...]
        w = cw_ref[...]
        xc = w[nw - 1:nw, :] * xr
        for s in range(nw - 1):
            xc = xc + w[s:s + 1, :] * sc_ref[:, s, :]
        xc = xc + cb_ref[...]
        a, mult, i = _rg_gates(xc, wa_ref, wx_ref, ba_ref, bx_ref, lam_ref)
        h = a * sh_ref[...] + mult * i * xc
        y_ref[...] = (h * _gelu_tanh(gr_ref[...])).astype(BF16)
        for s in range(nw - 2):
            conv_ref[:, s, :] = sc_ref[:, s + 1, :]
        conv_ref[:, nw - 2, :] = xr
        h_ref[...] = h

    vec = pl.BlockSpec((1, LANE), lambda h: (0, h))
    blk = pl.BlockSpec((None, LANE, LANE), lambda h: (h, 0, 0))
    return pl.pallas_call(
        kern, grid=(nh,),
        in_specs=[pl.BlockSpec((ms, LANE), lambda h: (rb, h)),
                  pl.BlockSpec((ms, LANE), lambda h: (rb, nh + h)),
                  pl.BlockSpec(memory_space=pl.ANY),
                  pl.BlockSpec((ms, nw - 1, LANE), lambda h: (0, 0, h)),
                  pl.BlockSpec((ms, LANE), lambda h: (0, h)),
                  pl.BlockSpec((nw, LANE), lambda h: (0, h)),
                  vec, blk, vec, blk, vec, vec],
        out_specs=[pl.BlockSpec((ms, LANE), lambda h: (rb, h)),
                   pl.BlockSpec((ms, nw - 1, LANE), lambda h: (0, 0, h)),
                   pl.BlockSpec((ms, LANE), lambda h: (0, h))],
        out_shape=[SDS(y_mix.shape, BF16), SDS((ms, nw - 1, d_rnn), F32), SDS((ms, d_rnn), F32)],
        input_output_aliases={2: 0},
        compiler_params=_cparams(1), name="rglru_sample")(
            z, z, y_mix, st_conv, st_h, cw, cb, wa, ba, wx, bx, lam)


def _gla_out(o, gn, g):
    return _rms(o, gn) * (g * _sigmoid(g))


def _gla_prompt(z, la, y_mix, gn, bsz, t, nh, dk, dv, col_q, col_y):
    tb = _tile(t, GLA_TB_TARGET, GLA_CHUNK)
    nt = t // tb
    nc = tb // GLA_CHUNK
    c = GLA_CHUNK
    hk, hv = nh * dk, nh * dv
    assert col_q % hk == 0 and (col_q + 2 * hk) % hv == 0 and col_y % hv == 0
    cq, ck = col_q // hk, col_q // hk + 1
    cv, cg = (col_q + 2 * hk) // hv, (col_q + 2 * hk) // hv + 1
    scale = dk ** -0.5

    def kern(q_ref, k_ref, v_ref, g_ref, la_ref, gn_ref, y_in_ref, y_ref, s_ref, st_ref):
        del y_in_ref
        tbi = pl.program_id(1)

        @pl.when(tbi == 0)
        def _():
            st_ref[...] = jnp.zeros_like(st_ref)

        row = lax.broadcasted_iota(I32, (c, dk), 0)
        causal = (lax.broadcasted_iota(I32, (c, c), 0) >= lax.broadcasted_iota(I32, (c, c), 1))

        def chunk(ci, carry):
            rows = pl.ds(pl.multiple_of(ci * c, c), c)
            for hd in range(nh):
                ks = slice(hd * dk, (hd + 1) * dk)
                vs = slice(hd * dv, (hd + 1) * dv)
                q = q_ref[rows, ks] * scale
                k = k_ref[rows, ks]
                v = v_ref[rows, vs].astype(BF16)
                bc = la_ref[rows, ks]
                s = 1
                while s < c:
                    bc = bc + _shift_rows(bc, s, row, 0.0)
                    s *= 2
                b_last = bc[c - 1:c, :]
                b_mid = bc[c // 2 - 1:c // 2, :]
                qe = (q * jnp.exp(bc)).astype(BF16)
                qm = (q * jnp.exp(bc - b_mid)).astype(BF16)
                km = (k * jnp.exp(b_mid - bc)).astype(BF16)
                kd = (k * jnp.exp(b_last - bc)).astype(BF16)
                st = st_ref[hd]
                o = lax.dot_general(qe, st.astype(BF16), (((1,), (1,)), ((), ())),
                                    preferred_element_type=F32)
                attn = lax.dot_general(qm, km, (((1,), (1,)), ((), ())),
                                       preferred_element_type=F32)
                attn = jnp.where(causal, attn, 0.0).astype(BF16)
                o = o + _dot(attn, v)
                st_ref[hd] = st * jnp.exp(b_last) + lax.dot_general(
                    v, kd, (((0,), (0,)), ((), ())), preferred_element_type=F32)
                y_ref[rows, vs] = _gla_out(o, gn_ref[...], g_ref[rows, vs]).astype(BF16)
            return carry

        lax.fori_loop(0, nc, chunk, 0)

        @pl.when(tbi == nt - 1)
        def _():
            for hd in range(nh):
                s_ref[hd] = st_ref[hd].T

    m = z.shape[0]
    return pl.pallas_call(
        kern, grid=(bsz, nt),
        in_specs=[pl.BlockSpec((tb, hk), lambda b_, i: (b_ * nt + i, cq)),
                  pl.BlockSpec((tb, hk), lambda b_, i: (b_ * nt + i, ck)),
                  pl.BlockSpec((tb, hv), lambda b_, i: (b_ * nt + i, cv)),
                  pl.BlockSpec((tb, hv), lambda b_, i: (b_ * nt + i, cg)),
                  pl.BlockSpec((tb, hk), lambda b_, i: (b_ * nt + i, 0)),
                  pl.BlockSpec((1, dv), lambda b_, i: (0, 0)),
                  pl.BlockSpec(memory_space=pl.ANY)],
        out_specs=[pl.BlockSpec((tb, hv), lambda b_, i: (b_ * nt + i, col_y // hv)),
                   pl.BlockSpec((None, nh, dk, dv), lambda b_, i: (b_, 0, 0, 0))],
        out_shape=[SDS(y_mix.shape, BF16), SDS((bsz, nh, dk, dv), F32)],
        scratch_shapes=[pltpu.VMEM((nh, dv, dk), F32)],
        input_output_aliases={6: 0},
        compiler_params=_cparams(2), name="gla_prompt")(z, z, z, z, la, gn, y_mix)


def _gla_sample(z, la, y_mix, st, gn, mp, ms, nh, dk, dv, col_q, col_y):
    bb = 16
    assert ms % bb == 0 and mp % ms == 0 and dk == LANE
    ns = ms // bb
    hk, hv = nh * dk, nh * dv
    cq, ck = col_q // hk, col_q // hk + 1
    cv, cg = (col_q + 2 * hk) // hv, (col_q + 2 * hk) // hv + 1
    scale = dk ** -0.5

    def kern(q_ref, k_ref, la_ref, v_ref, g_ref, gn_ref, st_ref, y_in_ref, y_ref, so_ref,
             qt_ref, kt_ref, at_ref):
        del y_in_ref
        i = pl.program_id(0)

        @pl.when(i == 0)
        def _():
            for hd in range(nh):
                ks = slice(hd * dk, (hd + 1) * dk)
                qt = (q_ref[:, ks] * scale).T
                kt = k_ref[:, ks].T
                at = jnp.exp(la_ref[:, ks]).T
                for s in range(ns):
                    qt_ref[s, hd] = qt[:, s * bb:(s + 1) * bb]
                    kt_ref[s, hd] = kt[:, s * bb:(s + 1) * bb]
                    at_ref[s, hd] = at[:, s * bb:(s + 1) * bb]

        for hd in range(nh):
            vs = slice(hd * dv, (hd + 1) * dv)
            qt = qt_ref[i, hd]
            kt = kt_ref[i, hd]
            at = at_ref[i, hd]
            outs = []
            for j in range(bb):
                v = v_ref[j:j + 1, vs]
                s_new = at[:, j:j + 1] * st_ref[j, hd] + kt[:, j:j + 1] * v
                so_ref[j, hd] = s_new
                outs.append(jnp.sum(qt[:, j:j + 1] * s_new, axis=0, keepdims=True))
            o = jnp.concatenate(outs, axis=0)
            y_ref[:, vs] = _gla_out(o, gn_ref[...], g_ref[:, vs]).astype(BF16)

    rb = mp // ms
    rbb = mp // bb
    return pl.pallas_call(
        kern, grid=(ns,),
        in_specs=[pl.BlockSpec((ms, hk), lambda i: (rb, cq)),
                  pl.BlockSpec((ms, hk), lambda i: (rb, ck)),
                  pl.BlockSpec((ms, hk), lambda i: (rb, 0)),
                  pl.BlockSpec((bb, hv), lambda i: (rbb + i, cv)),
                  pl.BlockSpec((bb, hv), lambda i: (rbb + i, cg)),
                  pl.BlockSpec((1, dv), lambda i: (0, 0)),
                  pl.BlockSpec((None, bb, nh, dk, dv), lambda i: (0, i, 0, 0, 0)),
                  pl.BlockSpec(memory_space=pl.ANY)],
        out_specs=[pl.BlockSpec((bb, hv), lambda i: (rbb + i, col_y // hv)),
                   pl.BlockSpec((None, bb, nh, dk, dv), lambda i: (0, i, 0, 0, 0))],
        out_shape=[SDS(y_mix.shape, BF16), SDS((1, ms, nh, dk, dv), F32)],
        scratch_shapes=[pltpu.VMEM((ns, nh, dk, bb), F32)] * 3,
        input_output_aliases={7: 0},
        compiler_params=_cparams(1), name="gla_sample")(z, z, la, z, z, gn, st, y_mix)


def _cast_bf16(w3):
    _, k, n = w3.shape
    tk = _tile(k, 512, 16)

    def kern(w_ref, o_ref):
        o_ref[...] = w_ref[...].astype(BF16)

    return pl.pallas_call(
        kern, grid=(k // tk,),
        in_specs=[pl.BlockSpec((None, tk, n), lambda i: (0, i, 0))],
        out_specs=pl.BlockSpec((tk, n), lambda i: (i, 0)),
        out_shape=SDS((k, n), BF16),
        compiler_params=_cparams(1), name="cast_bf16")(w3)


def _route(hn, rw):
    logits = _dot(hn.astype(BF16), rw.astype(BF16))
    ne = float(logits.shape[1])
    lane = lax.broadcasted_iota(I32, logits.shape, 1).astype(F32)
    m1 = jnp.max(logits, axis=1, keepdims=True)
    i1 = jnp.min(jnp.where(logits == m1, lane, ne), axis=1, keepdims=True)
    sel1 = lane == i1
    rest = jnp.where(sel1, -jnp.inf, logits)
    m2 = jnp.max(rest, axis=1, keepdims=True)
    i2 = jnp.min(jnp.where(rest == m2, lane, ne), axis=1, keepdims=True)
    sel2 = lane == i2
    e2 = jnp.exp(m2 - m1)
    g1 = 1.0 / (1.0 + e2)
    g2 = e2 / (1.0 + e2)
    mh = jnp.where(sel1 | sel2, 1.0, 0.0)
    gd = jnp.where(sel1, g1, 0.0) + jnp.where(sel2, g2, 0.0)
    return mh, gd


def _out_proj(y, wb, res, g, tm, res_tail=None, router_w=None):
    m, k = y.shape
    d = wb.shape[1]
    n_m = m // tm
    split = res_tail is not None
    if split:
        mp, ms = res.shape[0], res_tail.shape[0]
        assert mp + ms == m and ms <= tm
        head = tm - ms
        n_rb = pl.cdiv(mp, tm)
    route = router_w is not None

    def kern(*refs):
        it = iter(refs)
        y_ref, w_ref, r_ref = next(it), next(it), next(it)
        rt_ref = next(it) if split else None
        g_ref = next(it)
        rw_ref = next(it) if route else None
        h_ref, n_ref = next(it), next(it)
        mh_ref, gd_ref = (next(it), next(it)) if route else (None, None)
        acc_ref = next(it)
        i = pl.program_id(0)

        def matmul():
            acc_ref[i & 1] = _dot(y_ref[...], w_ref[...])

        def epilogue():
            acc = acc_ref[(i - 1) & 1]
            if split:
                last = i == n_m
                tail = acc[head:] + jnp.where(last, rt_ref[...], r_ref[head:tm, :])
                h = tail if head == 0 else jnp.concatenate([acc[:head] + r_ref[0:head, :], tail], 0)
            else:
                h = acc + r_ref[...]
            h_ref[...] = h
            hn = _rms(h, g_ref[...])
            if route:
                n_ref[...] = hn
                mh, gd = _route(hn, rw_ref[...])
                mh_ref[...] = mh
                gd_ref[...] = gd
            else:
                n_ref[...] = hn.astype(BF16)

        @pl.when(i < n_m)
        def _():
            matmul()

        @pl.when(i > 0)
        def _():
            epilogue()

    def lag(i):
        return jnp.maximum(i - 1, 0)

    in_specs = [pl.BlockSpec((tm, k), lambda i: (jnp.minimum(i, n_m - 1), 0)),
                pl.BlockSpec((k, d), lambda i: (0, 0))]
    args = [y, wb]
    if split:
        in_specs += [pl.BlockSpec((tm, d), lambda i: (jnp.minimum(lag(i), n_rb - 1), 0)),
                     pl.BlockSpec((ms, d), lambda i: (0, 0))]
        args += [res, res_tail]
    else:
        in_specs += [pl.BlockSpec((tm, d), lambda i: (lag(i), 0))]
        args += [res]
    in_specs += [pl.BlockSpec((1, d), lambda i: (0, 0))]
    args += [g]
    out_specs = [pl.BlockSpec((tm, d), lambda i: (lag(i), 0))]
    out_shape = [SDS((m, d), F32)]
    if route:
        ne = router_w.shape[1]
        in_specs += [pl.BlockSpec((d, ne), lambda i: (0, 0))]
        args += [router_w]
        out_specs += [pl.BlockSpec((tm, d), lambda i: (lag(i), 0)),
                      pl.BlockSpec((tm, ne), lambda i: (lag(i), 0)),
                      pl.BlockSpec((tm, ne), lambda i: (lag(i), 0))]
        out_shape += [SDS((m, d), F32), SDS((m, ne), F32), SDS((m, ne), F32)]
    else:
        out_specs += [pl.BlockSpec((tm, d), lambda i: (lag(i), 0))]
        out_shape += [SDS((m, d), BF16)]
    return pl.pallas_call(
        kern, grid=(n_m + 1,), in_specs=in_specs, out_specs=out_specs, out_shape=out_shape,
        scratch_shapes=[pltpu.VMEM((2, tm, d), F32)],
        compiler_params=_cparams(1), name="out_proj_route" if route else "out_proj")(*args)


def _ffn_residual(x, wg3, wu3, wd3, h, tm, tf):
    m, k = x.shape
    f = wg3.shape[2]
    d = wd3.shape[2]

    def kern(x_ref, wg_ref, wu_ref, wd_ref, h_ref, o_ref, sem):
        i = pl.program_id(0)
        j = pl.program_id(1)
        res = pltpu.make_async_copy(h_ref.at[pl.ds(pl.multiple_of(i * tm, tm), tm)], o_ref, sem)

        @pl.when(j == 0)
        def _():
            res.start()
            res.wait()

        xb = x_ref[...]
        gg = _dot(xb, wg_ref[...].astype(BF16))
        uu = _dot(xb, wu_ref[...].astype(BF16))
        a = (gg * _sigmoid(gg) * uu).astype(BF16)
        o_ref[...] += _dot(a, wd_ref[...].astype(BF16))

    return pl.pallas_call(
        kern, grid=(m // tm, f // tf),
        in_specs=[pl.BlockSpec((tm, k), lambda i, j: (i, 0)),
                  pl.BlockSpec((None, k, tf), lambda i, j: (0, 0, j)),
                  pl.BlockSpec((None, k, tf), lambda i, j: (0, 0, j)),
                  pl.BlockSpec((None, tf, d), lambda i, j: (0, j, 0)),
                  pl.BlockSpec(memory_space=pl.ANY)],
        out_specs=pl.BlockSpec((tm, d), lambda i, j: (i, 0)),
        out_shape=SDS((m, d), F32),
        scratch_shapes=[pltpu.SemaphoreType.DMA(())],
        compiler_params=_cparams(2), name="ffn")(x, wg3, wu3, wd3, h)


def _norm(h, g, tm):
    m, d = h.shape

    def kern(h_ref, g_ref, n_ref):
        n_ref[...] = _rms(h_ref[...], g_ref[...]).astype(BF16)

    row = pl.BlockSpec((tm, d), lambda i: (i, 0))
    return pl.pallas_call(
        kern, grid=(m // tm,),
        in_specs=[row, pl.BlockSpec((1, d), lambda i: (0, 0))],
        out_specs=row,
        out_shape=SDS((m, d), BF16),
        compiler_params=_cparams(1), name="norm")(h, g)


def _in_proj1(x, w3, tm, tn):
    m, k = x.shape
    dc = w3.shape[2] // 3
    nb = dc // tn

    def kern(x_ref, wb_ref, wc_ref, wv_ref, gb_ref, cv_ref, sb_ref, sc_ref, sv_ref):
        @pl.when(pl.program_id(1) == 0)
        def _():
            sb_ref[...] = wb_ref[...].astype(BF16)
            sc_ref[...] = wc_ref[...].astype(BF16)
            sv_ref[...] = wv_ref[...].astype(BF16)

        xb = x_ref[...]
        gb_ref[...] = _dot(xb, sb_ref[...]).astype(BF16)
        cv_ref[...] = _dot(xb, sc_ref[...]) * _dot(xb, sv_ref[...])

    def wspec(g):
        return pl.BlockSpec((None, k, tn), lambda n, i: (0, 0, g * nb + n))

    return pl.pallas_call(
        kern, grid=(nb, m // tm),
        in_specs=[pl.BlockSpec((tm, k), lambda n, i: (i, 0)), wspec(0), wspec(1), wspec(2)],
        out_specs=[pl.BlockSpec((tm, tn), lambda n, i: (i, n))] * 2,
        out_shape=[SDS((m, dc), BF16), SDS((m, dc), F32)],
        scratch_shapes=[pltpu.VMEM((k, tn), BF16)] * 3,
        compiler_params=_cparams(2), name="in_proj1")(x, w3, w3, w3)


def _shortconv_prompt(gb, cv, cw, bsz, t):
    m, dc = cv.shape
    tc = _tile(dc, 512, LANE)
    nw = cw.shape[0]

    def kern(gb_ref, cv_ref, cw_ref, u_ref, buf_ref):
        x = cv_ref[...]
        row = lax.broadcasted_iota(I32, x.shape, 0)
        w = cw_ref[...]
        u = w[nw - 1:nw, :] * x
        for s in range(1, nw):
            u = u + w[nw - 1 - s:nw - s, :] * _shift_rows(x, s, row, 0.0)
        u_ref[...] = (gb_ref[...].astype(F32) * u).astype(BF16)
        buf_ref[...] = x[t - (nw - 1):t, :]

    blk = pl.BlockSpec((t, tc), lambda b_, c: (b_, c))
    return pl.pallas_call(
        kern, grid=(bsz, dc // tc),
        in_specs=[blk, blk, pl.BlockSpec((nw, tc), lambda b_, c: (0, c))],
        out_specs=[blk, pl.BlockSpec((None, nw - 1, tc), lambda b_, c: (b_, 0, c))],
        out_shape=[SDS((m, dc), BF16), SDS((bsz, nw - 1, dc), F32)],
        compiler_params=_cparams(2), name="shortconv_prompt")(gb, cv, cw)


def _shortconv_sample(gb, cv, u_all, st, cw, mp, ms):
    dc = cv.shape[1]
    tc = _tile(dc, 512, LANE)
    nw = cw.shape[0]
    rb = mp // ms

    def kern(gb_ref, cv_ref, u_in_ref, st_ref, cw_ref, u_ref, buf_ref):
        del u_in_ref
        x = cv_ref[...]
        w = cw_ref[...]
        u = w[nw - 1:nw, :] * x
        for s in range(nw - 1):
            u = u + w[s:s + 1, :] * st_ref[:, s, :]
        u_ref[...] = (gb_ref[...].astype(F32) * u).astype(BF16)
        for s in range(nw - 2):
            buf_ref[:, s, :] = st_ref[:, s + 1, :]
        buf_ref[:, nw - 2, :] = x

    blk = pl.BlockSpec((ms, tc), lambda c: (rb, c))
    stb = pl.BlockSpec((ms, nw - 1, tc), lambda c: (0, 0, c))
    return pl.pallas_call(
        kern, grid=(dc // tc,),
        in_specs=[blk, blk, pl.BlockSpec(memory_space=pl.ANY), stb,
                  pl.BlockSpec((nw, tc), lambda c: (0, c))],
        out_specs=[blk, stb],
        out_shape=[SDS(u_all.shape, BF16), SDS((ms, nw - 1, dc), F32)],
        input_output_aliases={2: 0},
        compiler_params=_cparams(1), name="shortconv_sample")(gb, cv, u_all, st, cw)


def _moe_rank(mh, tm):
    m, ne = mh.shape

    def kern(mh_ref, ex_ref, cnt_ref, carry_ref):
        @pl.when(pl.program_id(0) == 0)
        def _():
            carry_ref[...] = jnp.zeros_like(carry_ref)

        x = mh_ref[...]
        tri = (lax.broadcasted_iota(I32, (tm, tm), 0) > lax.broadcasted_iota(I32, (tm, tm), 1))
        ex = _dot(jnp.where(tri, 1.0, 0.0).astype(BF16), x.astype(BF16)) + carry_ref[...]
        ex_ref[...] = ex
        tot = ex[tm - 1:tm, :] + x[tm - 1:tm, :]
        carry_ref[...] = tot
        cnt_ref[...] = tot

    return pl.pallas_call(
        kern, grid=(m // tm,),
        in_specs=[pl.BlockSpec((tm, ne), lambda i: (i, 0))],
        out_specs=[pl.BlockSpec((tm, ne), lambda i: (i, 0)), pl.BlockSpec((1, ne), lambda i: (0, 0))],
        out_shape=[SDS((m, ne), F32), SDS((1, ne), F32)],
        scratch_shapes=[pltpu.VMEM((1, ne), F32)],
        compiler_params=_cparams(1), name="moe_rank")(mh)


def _moe_pos(mh, gd, ex, off, tm):
    m, ne = mh.shape

    def kern(mh_ref, gd_ref, ex_ref, off_ref, pos_ref, gate_ref):
        sel = mh_ref[...] > 0.5
        pd = ex_ref[...] + off_ref[...]
        big = jnp.float32(3e38)
        p_lo = jnp.min(jnp.where(sel, pd, big), axis=1, keepdims=True)
        p_hi = jnp.max(jnp.where(sel, pd, -big), axis=1, keepdims=True)
        gdv = gd_ref[...]
        g_lo = jnp.sum(jnp.where(sel & (pd == p_lo), gdv, 0.0), axis=1, keepdims=True)
        g_hi = jnp.sum(jnp.where(sel & (pd == p_hi), gdv, 0.0), axis=1, keepdims=True)
        pos_ref[:, 0:1] = p_lo.astype(I32)
        pos_ref[:, 1:2] = p_hi.astype(I32)
        gate_ref[:, 0:1] = g_lo
        gate_ref[:, 1:2] = g_hi

    blk = pl.BlockSpec((tm, ne), lambda i: (i, 0))
    two = pl.BlockSpec((tm, TOP_K), lambda i: (i, 0))
    return pl.pallas_call(
        kern, grid=(m // tm,),
        in_specs=[blk, blk, blk, pl.BlockSpec((1, ne), lambda i: (0, 0))],
        out_specs=[two, two],
        out_shape=[SDS((m, TOP_K), I32), SDS((m, TOP_K), F32)],
        compiler_params=_cparams(1), name="moe_pos")(mh, gd, ex, off)


def _moe_scatter(pos_flat, pad_rows, n_pad, x, n_rows, tm):
    m, w = x.shape

    def kern(pos_ref, pad_ref, npad_ref, x_ref, xs_ref, zero_ref, sem, zsem):
        i = pl.program_id(0)

        def row_copy(r, p):
            return pltpu.make_async_copy(x_ref.at[pl.ds(r, 1)], xs_ref.at[pl.ds(p, 1)], sem)

        def issue(r, c):
            t = i * tm + r
            row_copy(r, pos_ref[TOP_K * t]).start(priority=0)
            row_copy(r, pos_ref[TOP_K * t + 1]).start(priority=1)
            return c

        lax.fori_loop(0, tm, issue, 0, unroll=8)

        @pl.when(i == 0)
        def _():
            zero_ref[...] = jnp.zeros_like(zero_ref)
            npad = npad_ref[0]

            def zcopy(j):
                return pltpu.make_async_copy(zero_ref.at[pl.ds(0, 1)],
                                             xs_ref.at[pl.ds(pad_ref[j], 1)], zsem)

            def zissue(j, c):
                zcopy(j).start()
                return c

            def zwait(j, c):
                zcopy(j).wait()
                return c

            lax.fori_loop(0, npad, zissue, 0)
            lax.fori_loop(0, npad, zwait, 0)

        def drain(r, c):
            row_copy(0, 0).wait()
            row_copy(0, 0).wait()
            return c

        lax.fori_loop(0, tm, drain, 0, unroll=8)

    return pl.pallas_call(
        kern,
        grid_spec=pltpu.PrefetchScalarGridSpec(
            num_scalar_prefetch=3, grid=(m // tm,),
            in_specs=[pl.BlockSpec((tm, w), lambda i, *_: (i, 0))],
            out_specs=pl.BlockSpec(memory_space=pl.ANY),
            scratch_shapes=[pltpu.VMEM((8, w), F32), pltpu.SemaphoreType.DMA(()),
                            pltpu.SemaphoreType.DMA(())]),
        out_shape=SDS((n_rows, w), F32),
        compiler_params=_cparams(1), name="moe_scatter")(pos_flat, pad_rows, n_pad, x)


def _moe_ffn(st_e, st_row, st_nsub, xs, wg4, wu4, wd4, rs, sub, tf):
    n_rows, k = xs.shape
    f = wg4.shape[3]
    d = wd4.shape[3]
    ng = st_e.shape[0]
    nf = f // tf
    assert k == d and nf >= 2
    kc = _tile(k, 512, LANE)
    big, mid = 4 * sub, 2 * sub

    def kern(se_ref, sr_ref, sn_ref, xs_ref, wg_ref, wu_ref, wd_ref, ys_ref,
             x_ref, acc_ref, a_ref, wgb_ref, wub_ref, wdb_ref, sem_in, sem_out):
        g = pl.program_id(0)
        j = pl.program_id(1)
        nsub = sn_ref[g]
        row0 = sr_ref[g]
        nbig = lax.shift_right_logical(nsub, 2)
        has_mid = (nsub & 2) != 0
        has_small = (nsub & 1) != 0
        start_mid = nbig * big
        start_small = start_mid + jnp.where(has_mid, mid, 0)

        def rows(start, size):
            return pl.ds(pl.multiple_of(start, sub), size)

        def in_copy(s):
            return pltpu.make_async_copy(xs_ref.at[rows(row0 + s * sub, sub)],
                                         acc_ref.at[rows(s * sub, sub)], sem_in)

        def out_copy(start, size):
            return pltpu.make_async_copy(acc_ref.at[rows(start, size)],
                                         ys_ref.at[rows(row0 + start, size)], sem_out)

        def each_sub(fn):
            def body(s, c):
                fn(s)
                return c
            lax.fori_loop(0, nsub, body, 0)

        def cast_weights():
            for c in range(k // kc):
                ks = slice(c * kc, (c + 1) * kc)
                wgb_ref[ks, :] = wg_ref[ks, :].astype(BF16)
                wub_ref[ks, :] = wu_ref[ks, :].astype(BF16)
            wdb_ref[...] = wd_ref[...].astype(BF16)

        def up(start, size, slot):
            xb = x_ref[rows(start, size), :]
            gg = _dot(xb, wgb_ref[...])
            uu = _dot(xb, wub_ref[...])
            a_ref[slot, 0:size, :] = (gg * _sigmoid(gg) * uu).astype(BF16)

        def down(start, size, slot, first, last):
            dd = _dot(a_ref[slot, 0:size, :], wdb_ref[...])
            if first:
                acc_ref[rows(start, size), :] = dd
            else:
                acc_ref[rows(start, size), :] += dd
            if last:
                out_copy(start, size).start()

        def compute(first, last, cast):
            @pl.when(nbig > 0)
            def _():
                if cast:
                    cast_weights()
                up(0, big, 0)

                def body(s, c):
                    up(s * big, big, s & 1)
                    down((s - 1) * big, big, (s - 1) & 1, first, last)
                    return c

                lax.fori_loop(1, nbig, body, 0)
                down((nbig - 1) * big, big, (nbig - 1) & 1, first, last)

            if cast:
                @pl.when(nbig == 0)
                def _():
                    cast_weights()

            @pl.when(has_mid)
            def _():
                up(start_mid, mid, 0)
                down(start_mid, mid, 0, first, last)

            @pl.when(has_small)
            def _():
                up(start_small, sub, 0)
                down(start_small, sub, 0, first, last)

        @pl.when(nsub > 0)
        def _():
            @pl.when(j == 0)
            def _():
                each_sub(lambda s: in_copy(s).start())
                cast_weights()

                def to_bf16(s):
                    x_ref[rows(s * sub, sub), :] = acc_ref[rows(s * sub, sub), :].astype(BF16)

                each_sub(lambda s: in_copy(s).wait())
                each_sub(to_bf16)
                compute(True, False, False)

            @pl.when((j > 0) & (j < nf - 1))
            def _():
                compute(False, False, True)

            @pl.when(j == nf - 1)
            def _():
                compute(False, True, True)

                def wait_big(s, c):
                    out_copy(0, big).wait()
                    return c

                lax.fori_loop(0, nbig, wait_big, 0)

                @pl.when(has_mid)
                def _():
                    out_copy(0, mid).wait()

                @pl.when(has_small)
                def _():
                    out_copy(0, sub).wait()

    def widx(g, j, se, sr, sn):
        return (0, se[g], 0, jnp.where(sn[g] > 0, j, nf - 1))

    def didx(g, j, se, sr, sn):
        return (0, se[g], jnp.where(sn[g] > 0, j, nf - 1), 0)

    return pl.pallas_call(
        kern,
        grid_spec=pltpu.PrefetchScalarGridSpec(
            num_scalar_prefetch=3, grid=(ng, nf),
            in_specs=[pl.BlockSpec(memory_space=pl.ANY),
                      pl.BlockSpec((None, None, k, tf), widx),
                      pl.BlockSpec((None, None, k, tf), widx),
                      pl.BlockSpec((None, None, tf, d), didx)],
            out_specs=pl.BlockSpec(memory_space=pl.ANY),
            scratch_shapes=[pltpu.VMEM((rs, k), BF16), pltpu.VMEM((rs, d), F32),
                            pltpu.VMEM((2, 4 * sub, tf), BF16),
                            pltpu.VMEM((k, tf), BF16), pltpu.VMEM((k, tf), BF16),
                            pltpu.VMEM((tf, d), BF16),
                            pltpu.SemaphoreType.DMA(()), pltpu.SemaphoreType.DMA(())]),
        out_shape=SDS((n_rows, d), F32),
        compiler_params=_cparams(2), name="moe_ffn")(st_e, st_row, st_nsub, xs, wg4, wu4, wd4)


def _moe_combine(pos_flat, h, gates, ys, g, mp, ms, tp):
    m, d = h.shape
    n_p = mp // tp
    assert mp % tp == 0 and ms <= tp and ms % 8 == 0

    def kern(pos_ref, h_ref, gate_ref, ys_ref, g_ref, yp_ref, ysm_ref, a_ref, b_ref, sem):
        i = pl.program_id(0)

        def copies(tile, r):
            t = tile * tp + r
            slot = tile & 1
            return (pltpu.make_async_copy(ys_ref.at[pl.ds(pos_ref[TOP_K * t], 1)],
                                          a_ref.at[slot, pl.ds(r, 1)], sem.at[slot]),
                    pltpu.make_async_copy(ys_ref.at[pl.ds(pos_ref[TOP_K * t + 1], 1)],
                                          b_ref.at[slot, pl.ds(r, 1)], sem.at[slot]))

        def gather(tile):
            def issue(r, c):
                ca, cb = copies(tile, r)
                ca.start(priority=0)
                cb.start(priority=1)
                return c

            @pl.when(tile < n_p)
            def _():
                lax.fori_loop(0, tp, issue, 0, unroll=8)

            @pl.when(tile == n_p)
            def _():
                lax.fori_loop(0, ms, issue, 0, unroll=8)

        def combined(n):
            def drain(r, c):
                ca, cb = copies(i, 0)
                ca.wait()
                cb.wait()
                return c

            lax.fori_loop(0, n, drain, 0, unroll=8)
            slot = i & 1
            gt = gate_ref[0:n, :]
            hh = (h_ref[0:n, :] + gt[:, 0:1] * a_ref[slot, 0:n, :]
                  + gt[:, 1:2] * b_ref[slot, 0:n, :])
            return _rms(hh, g_ref[...])

        @pl.when(i == 0)
        def _():
            gather(i)

        gather(i + 1)

        @pl.when(i < n_p)
        def _():
            yp_ref[...] = combined(tp)

        @pl.when(i == n_p)
        def _():
            ysm_ref[...] = combined(ms)

    return pl.pallas_call(
        kern,
        grid_spec=pltpu.PrefetchScalarGridSpec(
            num_scalar_prefetch=1, grid=(n_p + 1,),
            in_specs=[pl.BlockSpec((tp, d), lambda i, *_: (i, 0)),
                      pl.BlockSpec((tp, TOP_K), lambda i, *_: (i, 0)),
                      pl.BlockSpec(memory_space=pl.ANY),
                      pl.BlockSpec((1, d), lambda i, *_: (0, 0))],
            out_specs=[pl.BlockSpec((tp, d), lambda i, *_: (jnp.minimum(i, n_p - 1), 0)),
                       pl.BlockSpec((ms, d), lambda i, *_: (0, 0))],
            scratch_shapes=[pltpu.VMEM((2, tp, d), F32), pltpu.VMEM((2, tp, d), F32),
                            pltpu.SemaphoreType.DMA((2,))]),
        out_shape=[SDS((mp, d), F32), SDS((ms, d), F32)],
        compiler_params=_cparams(1), name="moe_combine")(pos_flat, h, gates, ys, g)


def _moe_tables(cnt, sub, rs, ng):
    ne = cnt.shape[0]
    nsub_e = (cnt + sub - 1) // sub
    size_e = nsub_e * sub
    off = jnp.cumsum(size_e) - size_e
    spr = rs // sub
    nst_e = (nsub_e + spr - 1) // spr
    st_start = jnp.cumsum(nst_e) - nst_e
    n_act = jnp.sum(nst_e)
    gidx = jnp.arange(ng, dtype=I32)
    e_of = jnp.sum((gidx[:, None] >= st_start[None, :]).astype(I32), axis=1) - 1
    e_of = jnp.clip(e_of, 0, ne - 1)
    kth = gidx - st_start[e_of]
    active = gidx < n_act
    nsub = jnp.where(active, jnp.clip(nsub_e[e_of] - kth * spr, 0, spr), 0)
    row = off[e_of] + kth * rs
    last = jnp.maximum(n_act - 1, 0)
    st_e = jnp.where(active, e_of, e_of[last]).astype(I32)
    st_row = jnp.where(active, row, 0).astype(I32)
    n_pad_e = size_e - cnt
    pad_start = jnp.cumsum(n_pad_e) - n_pad_e
    n_pad = jnp.sum(n_pad_e)
    pidx = jnp.arange(ne * (sub - 1), dtype=I32)
    pe = jnp.clip(jnp.sum((pidx[:, None] >= pad_start[None, :]).astype(I32), axis=1) - 1, 0, ne - 1)
    pad_rows = jnp.where(pidx < n_pad, off[pe] + cnt[pe] + (pidx - pad_start[pe]), 0).astype(I32)
    return off, st_e, st_row, nsub.astype(I32), pad_rows, n_pad.astype(I32).reshape(1)


def kernel(x_prompt, x_sample, state_rg_conv, state_rg_h, state_gla, state_sc_conv, norm_mix_e, w_in_e, rg_conv_w, rg_conv_b, rg_w_a, rg_b_a, rg_w_x, rg_b_x, rg_lambda, gla_w_gate, gla_b_gate, gla_norm, w_out_e, norm_ffn_e, ffn_w_gate, ffn_w_up, ffn_w_down, norm_mix_o, w_in_o, sc_conv_w, w_out_o, norm_ffn_o, router_w, moe_w_gate, moe_w_up, moe_w_down, final_norm):
    bsz, t, d = x_prompt.shape
    ms = x_sample.shape[0]
    assert x_sample.shape[1] == 1 and w_in_e.shape[0] == 1 and w_in_o.shape[0] == 1
    mp = bsz * t
    m = mp + ms
    d_rnn = rg_lambda.shape[1]
    nh, dk, dv = state_gla.shape[2:]
    hk, hv = nh * dk, nh * dv
    n_main = 2 * d_rnn + 2 * hk + 2 * hv
    d_mix = d_rnn + hv
    ne = router_w.shape[2]
    xp = x_prompt.reshape(mp, d)
    xs = x_sample.reshape(ms, d)
    row = lambda v: v.reshape(1, -1)

    tp = _tile(mp, TP_TARGET, 16)
    tm = _tile(m, TM_TARGET, 16)
    tm_in = _tile(m, TM_IN_TARGET, 16)
    tm_out = _tile(m, TM_OUT_TARGET, 16)

    hn0 = _norm_in(xp, xs, norm_mix_e, tp)
    w_in_t = jnp.swapaxes(w_in_e, 1, 2)
    z0 = _in_proj0(hn0, w_in_t, n_main, tm_in, _tile(n_main, 1024, LANE))
    la = _gla_gate(hn0, w_in_t, n_main, gla_w_gate[0], gla_b_gate, tm)
    y_mix, rgc_p, rgh_p = _rglru_prompt(z0, rg_conv_w[0], rg_conv_b, rg_w_a[0], rg_b_a, rg_w_x[0],
                                        rg_b_x, rg_lambda, bsz, t, d_rnn, d_mix)
    y_mix, rgc_s, rgh_s = _rglru_sample(z0, y_mix, state_rg_conv[0], state_rg_h[0], rg_conv_w[0],
                                        rg_conv_b, rg_w_a[0], rg_b_a, rg_w_x[0], rg_b_x, rg_lambda,
                                        mp, ms, d_rnn)
    y_mix, gla_p = _gla_prompt(z0, la, y_mix, gla_norm, bsz, t, nh, dk, dv, 2 * d_rnn, d_rnn)
    y_mix, gla_s = _gla_sample(z0, la, y_mix, state_gla, gla_norm, mp, ms, nh, dk, dv,
                               2 * d_rnn, d_rnn)
    h1, hn1 = _out_proj(y_mix, _cast_bf16(w_out_e), xp, norm_ffn_e, tm_out, res_tail=xs)

    h2 = _ffn_residual(hn1, ffn_w_gate, ffn_w_up, ffn_w_down, h1, tm,
                       _tile(ffn_w_gate.shape[2], 512, LANE))
    hn2 = _norm(h2, norm_mix_o, tm_in)

    gb, cv = _in_proj1(hn2, w_in_o, tm_in, _tile(w_in_o.shape[2] // 3, 256, LANE))
    u, sc_p = _shortconv_prompt(gb, cv, sc_conv_w[0], bsz, t)
    u, sc_s = _shortconv_sample(gb, cv, u, state_sc_conv[0], sc_conv_w[0], mp, ms)
    h3, hn3, mh, gd = _out_proj(u, _cast_bf16(w_out_o), h2, norm_ffn_o, tm_out,
                                router_w=router_w[0])

    sub = MOE_SUB
    rs = MOE_SPR * sub
    ng = (TOP_K * m) // rs + ne
    n_rows = TOP_K * m + ne * (sub - 1)
    n_rows = ((n_rows + sub - 1) // sub) * sub
    ex, cnt = _moe_rank(mh, tm)
    off, st_e, st_row, st_nsub, pad_rows, n_pad = _moe_tables(cnt[0].astype(I32), sub, rs, ng)
    pos, gates = _moe_pos(mh, gd, ex, off.astype(F32).reshape(1, ne), tm)
    pos_flat = pos.reshape(TOP_K * m)
    xsort = _moe_scatter(pos_flat, pad_rows, n_pad, hn3, n_rows, tm)
    ys = _moe_ffn(st_e, st_row, st_nsub, xsort, moe_w_gate, moe_w_up, moe_w_down, rs, sub,
                  _tile(moe_w_gate.shape[3], 256, LANE))
    y_p, y_s = _moe_combine(pos_flat, h3, gates, ys, row(final_norm), mp, ms, tp)

    return (y_p.reshape(bsz, t, d), y_s.reshape(ms, 1, d),
            rgc_p[None], rgc_s[None], rgh_p.reshape(1, bsz, d_rnn), rgh_s[None],
            gla_p[None], gla_s, sc_p[None], sc_s[None])
```

```python
import functools

import jax
import jax.numpy as jnp
from jax import lax
from jax.experimental import pallas as pl
from jax.experimental.pallas import tpu as pltpu

F32 = jnp.float32
BF16 = jnp.bfloat16
I32 = jnp.int32
SDS = jax.ShapeDtypeStruct

EPS = 1e-6
RG_C = 8.0
GLA_TAU = 16.0
GLA_CHUNK = 64
TOP_K = 2
LANE = 128
SUB8 = 8
VMEM_LIMIT = 56 * 1024 * 1024
ARB = "arbitrary"
TM_TARGET = 640
TM_IN_TARGET = 1664
TM_OUT_TARGET = 320
TP_TARGET = 512
GLA_TB_TARGET = 512
MOE_SUB = 128
MOE_SPR = 20


def _cparams(n_axes, vmem=VMEM_LIMIT):
    return pltpu.CompilerParams(dimension_semantics=(ARB,) * n_axes, vmem_limit_bytes=vmem)


def _tile(n, target, align):
    best = None
    for t in range(align, min(n, target) + 1, align):
        if n % t == 0:
            best = t
    assert best is not None, (n, target, align)
    return best


def _rms(xf, g):
    ms = jnp.mean(xf * xf, axis=-1, keepdims=True)
    return xf * lax.rsqrt(ms + EPS) * g


def _sigmoid(x):
    return 1.0 / (1.0 + jnp.exp(-x))


def _softplus(x):
    return jnp.maximum(x, 0.0) + jnp.log1p(jnp.exp(-jnp.abs(x)))


def _gelu_tanh(x):
    c = 0.7978845608028654
    return x * (0.5 * (1.0 + jnp.tanh(c * (x + 0.044715 * (x * x * x)))))


def _dot(a, b):
    return jnp.dot(a, b, preferred_element_type=F32)


def _shift_rows(x, s, row, fill):
    return jnp.where(row >= s, pltpu.roll(x, s, 0), fill)


def _norm_in(xp, xs, g, tp):
    mp, d = xp.shape
    ms = xs.shape[0]
    n_p = mp // tp
    assert mp % tp == 0 and ms <= tp and ms % 16 == 0

    def kern(xp_ref, xs_ref, g_ref, o_ref):
        i = pl.program_id(0)

        @pl.when(i < n_p)
        def _():
            o_ref[...] = _rms(xp_ref[...], g_ref[...]).astype(BF16)

        @pl.when(i == n_p)
        def _():
            o_ref[0:ms, :] = _rms(xs_ref[...], g_ref[...]).astype(BF16)

    return pl.pallas_call(
        kern, grid=(n_p + 1,),
        in_specs=[pl.BlockSpec((tp, d), lambda i: (jnp.minimum(i, n_p - 1), 0)),
                  pl.BlockSpec((ms, d), lambda i: (0, 0)),
                  pl.BlockSpec((1, d), lambda i: (0, 0))],
        out_specs=pl.BlockSpec((tp, d), lambda i: (i, 0)),
        out_shape=SDS((mp + ms, d), BF16),
        compiler_params=_cparams(1), name="norm_in")(xp, xs, g)


def _in_proj0(x, wt3, n_cols, tm, tn):
    m, k = x.shape

    def kern(x_ref, w_ref, o_ref, wb_ref):
        @pl.when(pl.program_id(1) == 0)
        def _():
            wb_ref[...] = w_ref[...].T.astype(BF16)

        o_ref[...] = _dot(x_ref[...], wb_ref[...])

    return pl.pallas_call(
        kern, grid=(n_cols // tn, m // tm),
        in_specs=[pl.BlockSpec((tm, k), lambda n, i: (i, 0)),
                  pl.BlockSpec((None, tn, k), lambda n, i: (0, n, 0))],
        out_specs=pl.BlockSpec((tm, tn), lambda n, i: (i, n)),
        out_shape=SDS((m, n_cols), F32),
        scratch_shapes=[pltpu.VMEM((k, tn), BF16)],
        compiler_params=_cparams(2), name="in_proj0")(x, wt3)


def _gla_gate(x, wt3, col0, w_gate, b_gate, tm):
    m, k = x.shape
    r, n = w_gate.shape
    assert col0 % r == 0

    def kern(x_ref, wl_ref, wg_ref, bg_ref, o_ref):
        lr = lax.dot_general(x_ref[...], wl_ref[...].astype(BF16), (((1,), (1,)), ((), ())),
                             preferred_element_type=F32)
        pre = _dot(lr.astype(BF16), wg_ref[...].astype(BF16)) + bg_ref[...]
        o_ref[...] = -_softplus(-pre) * (1.0 / GLA_TAU)

    return pl.pallas_call(
        kern, grid=(m // tm,),
        in_specs=[pl.BlockSpec((tm, k), lambda i: (i, 0)),
                  pl.BlockSpec((None, r, k), lambda i: (0, col0 // r, 0)),
                  pl.BlockSpec((r, n), lambda i: (0, 0)),
                  pl.BlockSpec((1, n), lambda i: (0, 0))],
        out_specs=pl.BlockSpec((tm, n), lambda i: (i, 0)),
        out_shape=SDS((m, n), F32),
        compiler_params=_cparams(1), name="gla_gate")(x, wt3, w_gate, b_gate)


def _rg_gates(xc, wa_ref, wx_ref, ba_ref, bx_ref, lam_ref):
    w2 = jnp.concatenate([wa_ref[...], wx_ref[...]], axis=1).astype(BF16)
    pre = _dot(xc.astype(BF16), w2)
    r = _sigmoid(pre[:, :LANE] + ba_ref[...])
    i = _sigmoid(pre[:, LANE:] + bx_ref[...])
    log_a = (-RG_C) * r * _softplus(-lam_ref[...])
    a = jnp.exp(log_a)
    mult = jnp.sqrt(jnp.tanh(-log_a) * (1.0 + a * a))
    return a, mult, i


def _rglru_prompt(z, cw, cb, wa, ba, wx, bx, lam, bsz, t, d_rnn, d_out):
    nh = d_rnn // LANE
    m = z.shape[0]

    def kern(xr_ref, gr_ref, cw_ref, cb_ref, wa_ref, ba_ref, wx_ref, bx_ref, lam_ref,
             y_ref, conv_ref, h_ref, a_s, b_s, c_s):
        xr = xr_ref[...]
        row = lax.broadcasted_iota(I32, (t, LANE), 0)
        w = cw_ref[...]
        nw = w.shape[0]
        xc = w[nw - 1:nw, :] * xr
        for s in range(1, nw):
            xc = xc + w[nw - 1 - s:nw - s, :] * _shift_rows(xr, s, row, 0.0)
        xc = xc + cb_ref[...]
        a, mult, i = _rg_gates(xc, wa_ref, wx_ref, ba_ref, bx_ref, lam_ref)
        mult = jnp.where(row == 0, 1.0, mult)
        b = mult * i * xc
        ng = t // SUB8
        a3, b3 = a.reshape(ng, SUB8, LANE), b.reshape(ng, SUB8, LANE)
        r8 = lax.broadcasted_iota(I32, (ng, SUB8, LANE), 1)
        s = 1
        while s < SUB8:
            b3 = a3 * jnp.where(r8 >= s, pltpu.roll(b3, s, 1), 0.0) + b3
            a3 = a3 * jnp.where(r8 >= s, pltpu.roll(a3, s, 1), 1.0)
            s *= 2
        a_s[...] = a3.reshape(t, LANE)
        b_s[...] = b3.reshape(t, LANE)
        ag = a_s[pl.ds(SUB8 - 1, ng, stride=SUB8), :]
        bg = b_s[pl.ds(SUB8 - 1, ng, stride=SUB8), :]
        rowg = lax.broadcasted_iota(I32, (ng, LANE), 0)
        s = 1
        while s < ng:
            bg = ag * _shift_rows(bg, s, rowg, 0.0) + bg
            ag = ag * _shift_rows(ag, s, rowg, 1.0)
            s *= 2
        carry = _shift_rows(bg, 1, rowg, 0.0)
        for k in range(SUB8):
            c_s[pl.ds(k, ng, stride=SUB8), :] = carry
        h = b_s[...] + a_s[...] * c_s[...]
        y_ref[...] = (h * _gelu_tanh(gr_ref[...])).astype(BF16)
        conv_ref[...] = xr[t - (nw - 1):t, :]
        h_ref[...] = h[t - 1:t, :]

    vec = pl.BlockSpec((1, LANE), lambda b_, h: (0, h))
    blk = pl.BlockSpec((None, LANE, LANE), lambda b_, h: (h, 0, 0))
    nw = cw.shape[0]
    return pl.pallas_call(
        kern, grid=(bsz, nh),
        in_specs=[pl.BlockSpec((t, LANE), lambda b_, h: (b_, h)),
                  pl.BlockSpec((t, LANE), lambda b_, h: (b_, nh + h)),
                  pl.BlockSpec((nw, LANE), lambda b_, h: (0, h)),
                  vec, blk, vec, blk, vec, vec],
        out_specs=[pl.BlockSpec((t, LANE), lambda b_, h: (b_, h)),
                   pl.BlockSpec((None, nw - 1, LANE), lambda b_, h: (b_, 0, h)),
                   pl.BlockSpec((None, 1, LANE), lambda b_, h: (b_, 0, h))],
        out_shape=[SDS((m, d_out), BF16), SDS((bsz, nw - 1, d_rnn), F32), SDS((bsz, 1, d_rnn), F32)],
        scratch_shapes=[pltpu.VMEM((t, LANE), F32)] * 3,
        compiler_params=_cparams(2), name="rglru_prompt")(z, z, cw, cb, wa, ba, wx, bx, lam)


def _rglru_sample(z, y_mix, st_conv, st_h, cw, cb, wa, ba, wx, bx, lam, mp, ms, d_rnn):
    nh = d_rnn // LANE
    nw = cw.shape[0]
    rb = mp // ms
    assert mp % ms == 0

    def kern(xr_ref, gr_ref, y_in_ref, sc_ref, sh_ref, cw_ref, cb_ref, wa_ref, ba_ref, wx_ref,
             bx_ref, lam_ref, y_ref, conv_ref, h_ref):
        del y_in_ref
        xr = xr_ref[...]
        w = cw_ref[...]
        xc = w[nw - 1:nw, :] * xr
        for s in range(nw - 1):
            xc = xc + w[s:s + 1, :] * sc_ref[:, s, :]
        xc = xc + cb_ref[...]
        a, mult, i = _rg_gates(xc, wa_ref, wx_ref, ba_ref, bx_ref, lam_ref)
        h = a * sh_ref[...] + mult * i * xc
        y_ref[...] = (h * _gelu_tanh(gr_ref[...])).astype(BF16)
        for s in range(nw - 2):
            conv_ref[:, s, :] = sc_ref[:, s + 1, :]
        conv_ref[:, nw - 2, :] = xr
        h_ref[...] = h

    vec = pl.BlockSpec((1, LANE), lambda h: (0, h))
    blk = pl.BlockSpec((None, LANE, LANE), lambda h: (h, 0, 0))
    return pl.pallas_call(
        kern, grid=(nh,),
        in_specs=[pl.BlockSpec((ms, LANE), lambda h: (rb, h)),
                  pl.BlockSpec((ms, LANE), lambda h: (rb, nh + h)),
                  pl.BlockSpec(memory_space=pl.ANY),
                  pl.BlockSpec((ms, nw - 1, LANE), lambda h: (0, 0, h)),
                  pl.BlockSpec((ms, LANE), lambda h: (0, h)),
                  pl.BlockSpec((nw, LANE), lambda h: (0, h)),
                  vec, blk, vec, blk, vec, vec],
        out_specs=[pl.BlockSpec((ms, LANE), lambda h: (rb, h)),
                   pl.BlockSpec((ms, nw - 1, LANE), lambda h: (0, 0, h)),
                   pl.BlockSpec((ms, LANE), lambda h: (0, h))],
        out_shape=[SDS(y_mix.shape, BF16), SDS((ms, nw - 1, d_rnn), F32), SDS((ms, d_rnn), F32)],
        input_output_aliases={2: 0},
        compiler_params=_cparams(1), name="rglru_sample")(
            z, z, y_mix, st_conv, st_h, cw, cb, wa, ba, wx, bx, lam)


def _gla_out(o, gn, g):
    return _rms(o, gn) * (g * _sigmoid(g))


def _gla_prompt(z, la, y_mix, gn, bsz, t, nh, dk, dv, col_q, col_y):
    tb = _tile(t, GLA_TB_TARGET, GLA_CHUNK)
    nt = t // tb
    nc = tb // GLA_CHUNK
    c = GLA_CHUNK
    hk, hv = nh * dk, nh * dv
    assert col_q % hk == 0 and (col_q + 2 * hk) % hv == 0 and col_y % hv == 0
    cq, ck = col_q // hk, col_q // hk + 1
    cv, cg = (col_q + 2 * hk) // hv, (col_q + 2 * hk) // hv + 1
    scale = dk ** -0.5

    def kern(q_ref, k_ref, v_ref, g_ref, la_ref, gn_ref, y_in_ref, y_ref, s_ref, st_ref):
        del y_in_ref
        tbi = pl.program_id(1)

        @pl.when(tbi == 0)
        def _():
            st_ref[...] = jnp.zeros_like(st_ref)

        row = lax.broadcasted_iota(I32, (c, dk), 0)
        causal = (lax.broadcasted_iota(I32, (c, c), 0) >= lax.broadcasted_iota(I32, (c, c), 1))

        def chunk(ci, carry):
            rows = pl.ds(pl.multiple_of(ci * c, c), c)
            for hd in range(nh):
                ks = slice(hd * dk, (hd + 1) * dk)
                vs = slice(hd * dv, (hd + 1) * dv)
                q = q_ref[rows, ks] * scale
                k = k_ref[rows, ks]
                v = v_ref[rows, vs].astype(BF16)
                bc = la_ref[rows, ks]
                s = 1
                while s < c:
                    bc = bc + _shift_rows(bc, s, row, 0.0)
                    s *= 2
                b_last = bc[c - 1:c, :]
                b_mid = bc[c // 2 - 1:c // 2, :]
                qe = (q * jnp.exp(bc)).astype(BF16)
                qm = (q * jnp.exp(bc - b_mid)).astype(BF16)
                km = (k * jnp.exp(b_mid - bc)).astype(BF16)
                kd = (k * jnp.exp(b_last - bc)).astype(BF16)
                st = st_ref[hd]
                o = lax.dot_general(qe, st.astype(BF16), (((1,), (1,)), ((), ())),
                                    preferred_element_type=F32)
                attn = lax.dot_general(qm, km, (((1,), (1,)), ((), ())),
                                       preferred_element_type=F32)
                attn = jnp.where(causal, attn, 0.0).astype(BF16)
                o = o + _dot(attn, v)
                st_ref[hd] = st * jnp.exp(b_last) + lax.dot_general(
                    v, kd, (((0,), (0,)), ((), ())), preferred_element_type=F32)
                y_ref[rows, vs] = _gla_out(o, gn_ref[...], g_ref[rows, vs]).astype(BF16)
            return carry

        lax.fori_loop(0, nc, chunk, 0)

        @pl.when(tbi == nt - 1)
        def _():
            for hd in range(nh):
                s_ref[hd] = st_ref[hd].T

    m = z.shape[0]
    return pl.pallas_call(
        kern, grid=(bsz, nt),
        in_specs=[pl.BlockSpec((tb, hk), lambda b_, i: (b_ * nt + i, cq)),
                  pl.BlockSpec((tb, hk), lambda b_, i: (b_ * nt + i, ck)),
                  pl.BlockSpec((tb, hv), lambda b_, i: (b_ * nt + i, cv)),
                  pl.BlockSpec((tb, hv), lambda b_, i: (b_ * nt + i, cg)),
                  pl.BlockSpec((tb, hk), lambda b_, i: (b_ * nt + i, 0)),
                  pl.BlockSpec((1, dv), lambda b_, i: (0, 0)),
                  pl.BlockSpec(memory_space=pl.ANY)],
        out_specs=[pl.BlockSpec((tb, hv), lambda b_, i: (b_ * nt + i, col_y // hv)),
                   pl.BlockSpec((None, nh, dk, dv), lambda b_, i: (b_, 0, 0, 0))],
        out_shape=[SDS(y_mix.shape, BF16), SDS((bsz, nh, dk, dv), F32)],
        scratch_shapes=[pltpu.VMEM((nh, dv, dk), F32)],
        input_output_aliases={6: 0},
        compiler_params=_cparams(2), name="gla_prompt")(z, z, z, z, la, gn, y_mix)


def _gla_sample(z, la, y_mix, st, gn, mp, ms, nh, dk, dv, col_q, col_y):
    bb = 16
    assert ms % bb == 0 and mp % ms == 0 and dk == LANE
    ns = ms // bb
    hk, hv = nh * dk, nh * dv
    cq, ck = col_q // hk, col_q // hk + 1
    cv, cg = (col_q + 2 * hk) // hv, (col_q + 2 * hk) // hv + 1
    scale = dk ** -0.5

    def kern(q_ref, k_ref, la_ref, v_ref, g_ref, gn_ref, st_ref, y_in_ref, y_ref, so_ref,
             qt_ref, kt_ref, at_ref):
        del y_in_ref
        i = pl.program_id(0)

        @pl.when(i == 0)
        def _():
            for hd in range(nh):
                ks = slice(hd * dk, (hd + 1) * dk)
                qt = (q_ref[:, ks] * scale).T
                kt = k_ref[:, ks].T
                at = jnp.exp(la_ref[:, ks]).T
                for s in range(ns):
                    qt_ref[s, hd] = qt[:, s * bb:(s + 1) * bb]
                    kt_ref[s, hd] = kt[:, s * bb:(s + 1) * bb]
                    at_ref[s, hd] = at[:, s * bb:(s + 1) * bb]

        for hd in range(nh):
            vs = slice(hd * dv, (hd + 1) * dv)
            qt = qt_ref[i, hd]
            kt = kt_ref[i, hd]
            at = at_ref[i, hd]
            outs = []
            for j in range(bb):
                v = v_ref[j:j + 1, vs]
                s_new = at[:, j:j + 1] * st_ref[j, hd] + kt[:, j:j + 1] * v
                so_ref[j, hd] = s_new
                outs.append(jnp.sum(qt[:, j:j + 1] * s_new, axis=0, keepdims=True))
            o = jnp.concatenate(outs, axis=0)
            y_ref[:, vs] = _gla_out(o, gn_ref[...], g_ref[:, vs]).astype(BF16)

    rb = mp // ms
    rbb = mp // bb
    return pl.pallas_call(
        kern, grid=(ns,),
        in_specs=[pl.BlockSpec((ms, hk), lambda i: (rb, cq)),
                  pl.BlockSpec((ms, hk), lambda i: (rb, ck)),
                  pl.BlockSpec((ms, hk), lambda i: (rb, 0)),
                  pl.BlockSpec((bb, hv), lambda i: (rbb + i, cv)),
                  pl.BlockSpec((bb, hv), lambda i: (rbb + i, cg)),
                  pl.BlockSpec((1, dv), lambda i: (0, 0)),
                  pl.BlockSpec((None, bb, nh, dk, dv), lambda i: (0, i, 0, 0, 0)),
                  pl.BlockSpec(memory_space=pl.ANY)],
        out_specs=[pl.BlockSpec((bb, hv), lambda i: (rbb + i, col_y // hv)),
                   pl.BlockSpec((None, bb, nh, dk, dv), lambda i: (0, i, 0, 0, 0))],
        out_shape=[SDS(y_mix.shape, BF16), SDS((1, ms, nh, dk, dv), F32)],
        scratch_shapes=[pltpu.VMEM((ns, nh, dk, bb), F32)] * 3,
        input_output_aliases={7: 0},
        compiler_params=_cparams(1), name="gla_sample")(z, z, la, z, z, gn, st, y_mix)


def _cast_bf16(w3):
    _, k, n = w3.shape
    tk = _tile(k, 512, 16)

    def kern(w_ref, o_ref):
        o_ref[...] = w_ref[...].astype(BF16)

    return pl.pallas_call(
        kern, grid=(k // tk,),
        in_specs=[pl.BlockSpec((None, tk, n), lambda i: (0, i, 0))],
        out_specs=pl.BlockSpec((tk, n), lambda i: (i, 0)),
        out_shape=SDS((k, n), BF16),
        compiler_params=_cparams(1), name="cast_bf16")(w3)


def _route(hn, rw):
    logits = _dot(hn.astype(BF16), rw.astype(BF16))
    ne = float(logits.shape[1])
    lane = lax.broadcasted_iota(I32, logits.shape, 1).astype(F32)
    m1 = jnp.max(logits, axis=1, keepdims=True)
    i1 = jnp.min(jnp.where(logits == m1, lane, ne), axis=1, keepdims=True)
    sel1 = lane == i1
    rest = jnp.where(sel1, -jnp.inf, logits)
    m2 = jnp.max(rest, axis=1, keepdims=True)
    i2 = jnp.min(jnp.where(rest == m2, lane, ne), axis=1, keepdims=True)
    sel2 = lane == i2
    e2 = jnp.exp(m2 - m1)
    g1 = 1.0 / (1.0 + e2)
    g2 = e2 / (1.0 + e2)
    mh = jnp.where(sel1 | sel2, 1.0, 0.0)
    gd = jnp.where(sel1, g1, 0.0) + jnp.where(sel2, g2, 0.0)
    return mh, gd


def _out_proj(y, wb, res, g, tm, res_tail=None, router_w=None):
    m, k = y.shape
    d = wb.shape[1]
    n_m = m // tm
    split = res_tail is not None
    if split:
        mp, ms = res.shape[0], res_tail.shape[0]
        assert mp + ms == m and ms <= tm
        head = tm - ms
        n_rb = pl.cdiv(mp, tm)
    route = router_w is not None

    def kern(*refs):
        it = iter(refs)
        y_ref, w_ref, r_ref = next(it), next(it), next(it)
        rt_ref = next(it) if split else None
        g_ref = next(it)
        rw_ref = next(it) if route else None
        h_ref, n_ref = next(it), next(it)
        mh_ref, gd_ref = (next(it), next(it)) if route else (None, None)
        acc_ref = next(it)
        i = pl.program_id(0)

        def matmul():
            acc_ref[i & 1] = _dot(y_ref[...], w_ref[...])

        def epilogue():
            acc = acc_ref[(i - 1) & 1]
            if split:
                last = i == n_m
                tail = acc[head:] + jnp.where(last, rt_ref[...], r_ref[head:tm, :])
                h = tail if head == 0 else jnp.concatenate([acc[:head] + r_ref[0:head, :], tail], 0)
            else:
                h = acc + r_ref[...]
            h_ref[...] = h
            hn = _rms(h, g_ref[...])
            if route:
                n_ref[...] = hn
                mh, gd = _route(hn, rw_ref[...])
                mh_ref[...] = mh
                gd_ref[...] = gd
            else:
                n_ref[...] = hn.astype(BF16)

        @pl.when(i < n_m)
        def _():
            matmul()

        @pl.when(i > 0)
        def _():
            epilogue()

    def lag(i):
        return jnp.maximum(i - 1, 0)

    in_specs = [pl.BlockSpec((tm, k), lambda i: (jnp.minimum(i, n_m - 1), 0)),
                pl.BlockSpec((k, d), lambda i: (0, 0))]
    args = [y, wb]
    if split:
        in_specs += [pl.BlockSpec((tm, d), lambda i: (jnp.minimum(lag(i), n_rb - 1), 0)),
                     pl.BlockSpec((ms, d), lambda i: (0, 0))]
        args += [res, res_tail]
    else:
        in_specs += [pl.BlockSpec((tm, d), lambda i: (lag(i), 0))]
        args += [res]
    in_specs += [pl.BlockSpec((1, d), lambda i: (0, 0))]
    args += [g]
    out_specs = [pl.BlockSpec((tm, d), lambda i: (lag(i), 0))]
    out_shape = [SDS((m, d), F32)]
    if route:
        ne = router_w.shape[1]
        in_specs += [pl.BlockSpec((d, ne), lambda i: (0, 0))]
        args += [router_w]
        out_specs += [pl.BlockSpec((tm, d), lambda i: (lag(i), 0)),
                      pl.BlockSpec((tm, ne), lambda i: (lag(i), 0)),
                      pl.BlockSpec((tm, ne), lambda i: (lag(i), 0))]
        out_shape += [SDS((m, d), F32), SDS((m, ne), F32), SDS((m, ne), F32)]
    else:
        out_specs += [pl.BlockSpec((tm, d), lambda i: (lag(i), 0))]
        out_shape += [SDS((m, d), BF16)]
    return pl.pallas_call(
        kern, grid=(n_m + 1,), in_specs=in_specs, out_specs=out_specs, out_shape=out_shape,
        scratch_shapes=[pltpu.VMEM((2, tm, d), F32)],
        compiler_params=_cparams(1), name="out_proj_route" if route else "out_proj")(*args)


def _ffn(x, wg3, wu3, wd3, tm, tf):
    m, k = x.shape
    f = wg3.shape[2]
    d = wd3.shape[2]

    def kern(x_ref, wg_ref, wu_ref, wd_ref, o_ref):
        @pl.when(pl.program_id(1) == 0)
        def _():
            o_ref[...] = jnp.zeros_like(o_ref)

        xb = x_ref[...]
        gg = _dot(xb, wg_ref[...].astype(BF16))
        uu = _dot(xb, wu_ref[...].astype(BF16))
        a = (gg * _sigmoid(gg) * uu).astype(BF16)
        o_ref[...] += _dot(a, wd_ref[...].astype(BF16))

    return pl.pallas_call(
        kern, grid=(m // tm, f // tf),
        in_specs=[pl.BlockSpec((tm, k), lambda i, j: (i, 0)),
                  pl.BlockSpec((None, k, tf), lambda i, j: (0, 0, j)),
                  pl.BlockSpec((None, k, tf), lambda i, j: (0, 0, j)),
                  pl.BlockSpec((None, tf, d), lambda i, j: (0, j, 0))],
        out_specs=pl.BlockSpec((tm, d), lambda i, j: (i, 0)),
        out_shape=SDS((m, d), F32),
        compiler_params=_cparams(2), name="ffn")(x, wg3, wu3, wd3)


def _add_norm(a, b, g, tm):
    m, d = a.shape

    def kern(a_ref, b_ref, g_ref, h_ref, n_ref):
        h = a_ref[...] + b_ref[...]
        h_ref[...] = h
        n_ref[...] = _rms(h, g_ref[...]).astype(BF16)

    row = pl.BlockSpec((tm, d), lambda i: (i, 0))
    return pl.pallas_call(
        kern, grid=(m // tm,),
        in_specs=[row, row, pl.BlockSpec((1, d), lambda i: (0, 0))],
        out_specs=[row, row],
        out_shape=[SDS((m, d), F32), SDS((m, d), BF16)],
        compiler_params=_cparams(1), name="add_norm")(a, b, g)


def _in_proj1(x, w3, tm, tn):
    m, k = x.shape
    dc = w3.shape[2] // 3
    nb = dc // tn

    def kern(x_ref, wb_ref, wc_ref, wv_ref, gb_ref, cv_ref, sb_ref, sc_ref, sv_ref):
        @pl.when(pl.program_id(1) == 0)
        def _():
            sb_ref[...] = wb_ref[...].astype(BF16)
            sc_ref[...] = wc_ref[...].astype(BF16)
            sv_ref[...] = wv_ref[...].astype(BF16)

        xb = x_ref[...]
        gb_ref[...] = _dot(xb, sb_ref[...]).astype(BF16)
        cv_ref[...] = _dot(xb, sc_ref[...]) * _dot(xb, sv_ref[...])

    def wspec(g):
        return pl.BlockSpec((None, k, tn), lambda n, i: (0, 0, g * nb + n))

    return pl.pallas_call(
        kern, grid=(nb, m // tm),
        in_specs=[pl.BlockSpec((tm, k), lambda n, i: (i, 0)), wspec(0), wspec(1), wspec(2)],
        out_specs=[pl.BlockSpec((tm, tn), lambda n, i: (i, n))] * 2,
        out_shape=[SDS((m, dc), BF16), SDS((m, dc), F32)],
        scratch_shapes=[pltpu.VMEM((k, tn), BF16)] * 3,
        compiler_params=_cparams(2), name="in_proj1")(x, w3, w3, w3)


def _shortconv_prompt(gb, cv, cw, bsz, t):
    m, dc = cv.shape
    tc = _tile(dc, 512, LANE)
    nw = cw.shape[0]

    def kern(gb_ref, cv_ref, cw_ref, u_ref, buf_ref):
        x = cv_ref[...]
        row = lax.broadcasted_iota(I32, x.shape, 0)
        w = cw_ref[...]
        u = w[nw - 1:nw, :] * x
        for s in range(1, nw):
            u = u + w[nw - 1 - s:nw - s, :] * _shift_rows(x, s, row, 0.0)
        u_ref[...] = (gb_ref[...].astype(F32) * u).astype(BF16)
        buf_ref[...] = x[t - (nw - 1):t, :]

    blk = pl.BlockSpec((t, tc), lambda b_, c: (b_, c))
    return pl.pallas_call(
        kern, grid=(bsz, dc // tc),
        in_specs=[blk, blk, pl.BlockSpec((nw, tc), lambda b_, c: (0, c))],
        out_specs=[blk, pl.BlockSpec((None, nw - 1, tc), lambda b_, c: (b_, 0, c))],
        out_shape=[SDS((m, dc), BF16), SDS((bsz, nw - 1, dc), F32)],
        compiler_params=_cparams(2), name="shortconv_prompt")(gb, cv, cw)


def _shortconv_sample(gb, cv, u_all, st, cw, mp, ms):
    dc = cv.shape[1]
    tc = _tile(dc, 512, LANE)
    nw = cw.shape[0]
    rb = mp // ms

    def kern(gb_ref, cv_ref, u_in_ref, st_ref, cw_ref, u_ref, buf_ref):
        del u_in_ref
        x = cv_ref[...]
        w = cw_ref[...]
        u = w[nw - 1:nw, :] * x
        for s in range(nw - 1):
            u = u + w[s:s + 1, :] * st_ref[:, s, :]
        u_ref[...] = (gb_ref[...].astype(F32) * u).astype(BF16)
        for s in range(nw - 2):
            buf_ref[:, s, :] = st_ref[:, s + 1, :]
        buf_ref[:, nw - 2, :] = x

    blk = pl.BlockSpec((ms, tc), lambda c: (rb, c))
    stb = pl.BlockSpec((ms, nw - 1, tc), lambda c: (0, 0, c))
    return pl.pallas_call(
        kern, grid=(dc // tc,),
        in_specs=[blk, blk, pl.BlockSpec(memory_space=pl.ANY), stb,
                  pl.BlockSpec((nw, tc), lambda c: (0, c))],
        out_specs=[blk, stb],
        out_shape=[SDS(u_all.shape, BF16), SDS((ms, nw - 1, dc), F32)],
        input_output_aliases={2: 0},
        compiler_params=_cparams(1), name="shortconv_sample")(gb, cv, u_all, st, cw)


def _moe_rank(mh, tm):
    m, ne = mh.shape

    def kern(mh_ref, ex_ref, cnt_ref, carry_ref):
        @pl.when(pl.program_id(0) == 0)
        def _():
            carry_ref[...] = jnp.zeros_like(carry_ref)

        x = mh_ref[...]
        tri = (lax.broadcasted_iota(I32, (tm, tm), 0) > lax.broadcasted_iota(I32, (tm, tm), 1))
        ex = _dot(jnp.where(tri, 1.0, 0.0).astype(BF16), x.astype(BF16)) + carry_ref[...]
        ex_ref[...] = ex
        tot = ex[tm - 1:tm, :] + x[tm - 1:tm, :]
        carry_ref[...] = tot
        cnt_ref[...] = tot

    return pl.pallas_call(
        kern, grid=(m // tm,),
        in_specs=[pl.BlockSpec((tm, ne), lambda i: (i, 0))],
        out_specs=[pl.BlockSpec((tm, ne), lambda i: (i, 0)), pl.BlockSpec((1, ne), lambda i: (0, 0))],
        out_shape=[SDS((m, ne), F32), SDS((1, ne), F32)],
        scratch_shapes=[pltpu.VMEM((1, ne), F32)],
        compiler_params=_cparams(1), name="moe_rank")(mh)


def _moe_pos(mh, gd, ex, off, tm):
    m, ne = mh.shape

    def kern(mh_ref, gd_ref, ex_ref, off_ref, pos_ref, gate_ref):
        sel = mh_ref[...] > 0.5
        pd = ex_ref[...] + off_ref[...]
        big = jnp.float32(3e38)
        p_lo = jnp.min(jnp.where(sel, pd, big), axis=1, keepdims=True)
        p_hi = jnp.max(jnp.where(sel, pd, -big), axis=1, keepdims=True)
        gdv = gd_ref[...]
        g_lo = jnp.sum(jnp.where(sel & (pd == p_lo), gdv, 0.0), axis=1, keepdims=True)
        g_hi = jnp.sum(jnp.where(sel & (pd == p_hi), gdv, 0.0), axis=1, keepdims=True)
        pos_ref[:, 0:1] = p_lo.astype(I32)
        pos_ref[:, 1:2] = p_hi.astype(I32)
        gate_ref[:, 0:1] = g_lo
        gate_ref[:, 1:2] = g_hi

    blk = pl.BlockSpec((tm, ne), lambda i: (i, 0))
    two = pl.BlockSpec((tm, TOP_K), lambda i: (i, 0))
    return pl.pallas_call(
        kern, grid=(m // tm,),
        in_specs=[blk, blk, blk, pl.BlockSpec((1, ne), lambda i: (0, 0))],
        out_specs=[two, two],
        out_shape=[SDS((m, TOP_K), I32), SDS((m, TOP_K), F32)],
        compiler_params=_cparams(1), name="moe_pos")(mh, gd, ex, off)


def _moe_scatter(pos_flat, pad_rows, n_pad, x, n_rows, tm):
    m, w = x.shape

    def kern(pos_ref, pad_ref, npad_ref, x_ref, xs_ref, zero_ref, sem, zsem):
        i = pl.program_id(0)

        def row_copy(r, p):
            return pltpu.make_async_copy(x_ref.at[pl.ds(r, 1)], xs_ref.at[pl.ds(p, 1)], sem)

        def issue(r, c):
            t = i * tm + r
            row_copy(r, pos_ref[TOP_K * t]).start(priority=0)
            row_copy(r, pos_ref[TOP_K * t + 1]).start(priority=1)
            return c

        lax.fori_loop(0, tm, issue, 0, unroll=8)

        @pl.when(i == 0)
        def _():
            zero_ref[...] = jnp.zeros_like(zero_ref)
            npad = npad_ref[0]

            def zcopy(j):
                return pltpu.make_async_copy(zero_ref.at[pl.ds(0, 1)],
                                             xs_ref.at[pl.ds(pad_ref[j], 1)], zsem)

            def zissue(j, c):
                zcopy(j).start()
                return c

            def zwait(j, c):
                zcopy(j).wait()
                return c

            lax.fori_loop(0, npad, zissue, 0)
            lax.fori_loop(0, npad, zwait, 0)

        def drain(r, c):
            row_copy(0, 0).wait()
            row_copy(0, 0).wait()
            return c

        lax.fori_loop(0, tm, drain, 0, unroll=8)

    return pl.pallas_call(
        kern,
        grid_spec=pltpu.PrefetchScalarGridSpec(
            num_scalar_prefetch=3, grid=(m // tm,),
            in_specs=[pl.BlockSpec((tm, w), lambda i, *_: (i, 0))],
            out_specs=pl.BlockSpec(memory_space=pl.ANY),
            scratch_shapes=[pltpu.VMEM((8, w), F32), pltpu.SemaphoreType.DMA(()),
                            pltpu.SemaphoreType.DMA(())]),
        out_shape=SDS((n_rows, w), F32),
        compiler_params=_cparams(1), name="moe_scatter")(pos_flat, pad_rows, n_pad, x)


def _moe_ffn(st_e, st_row, st_nsub, xs, wg4, wu4, wd4, rs, sub, tf):
    n_rows, k = xs.shape
    f = wg4.shape[3]
    d = wd4.shape[3]
    ng = st_e.shape[0]
    nf = f // tf
    assert k == d and nf >= 2
    kc = _tile(k, 512, LANE)
    big, mid = 4 * sub, 2 * sub

    def kern(se_ref, sr_ref, sn_ref, xs_ref, wg_ref, wu_ref, wd_ref, ys_ref,
             x_ref, acc_ref, a_ref, wgb_ref, wub_ref, wdb_ref, sem_in, sem_out):
        g = pl.program_id(0)
        j = pl.program_id(1)
        nsub = sn_ref[g]
        row0 = sr_ref[g]
        nbig = lax.shift_right_logical(nsub, 2)
        has_mid = (nsub & 2) != 0
        has_small = (nsub & 1) != 0
        start_mid = nbig * big
        start_small = start_mid + jnp.where(has_mid, mid, 0)

        def rows(start, size):
            return pl.ds(pl.multiple_of(start, sub), size)

        def in_copy(s):
            return pltpu.make_async_copy(xs_ref.at[rows(row0 + s * sub, sub)],
                                         acc_ref.at[rows(s * sub, sub)], sem_in)

        def out_copy(start, size):
            return pltpu.make_async_copy(acc_ref.at[rows(start, size)],
                                         ys_ref.at[rows(row0 + start, size)], sem_out)

        def each_sub(fn):
            def body(s, c):
                fn(s)
                return c
            lax.fori_loop(0, nsub, body, 0)

        def cast_weights():
            for c in range(k // kc):
                ks = slice(c * kc, (c + 1) * kc)
                wgb_ref[ks, :] = wg_ref[ks, :].astype(BF16)
                wub_ref[ks, :] = wu_ref[ks, :].astype(BF16)
            wdb_ref[...] = wd_ref[...].astype(BF16)

        def up(start, size, slot):
            xb = x_ref[rows(start, size), :]
            gg = _dot(xb, wgb_ref[...])
            uu = _dot(xb, wub_ref[...])
            a_ref[slot, 0:size, :] = (gg * _sigmoid(gg) * uu).astype(BF16)

        def down(start, size, slot, first, last):
            dd = _dot(a_ref[slot, 0:size, :], wdb_ref[...])
            if first:
                acc_ref[rows(start, size), :] = dd
            else:
                acc_ref[rows(start, size), :] += dd
            if last:
                out_copy(start, size).start()

        def compute(first, last, cast):
            @pl.when(nbig > 0)
            def _():
                if cast:
                    cast_weights()
                up(0, big, 0)

                def body(s, c):
                    up(s * big, big, s & 1)
                    down((s - 1) * big, big, (s - 1) & 1, first, last)
                    return c

                lax.fori_loop(1, nbig, body, 0)
                down((nbig - 1) * big, big, (nbig - 1) & 1, first, last)

            if cast:
                @pl.when(nbig == 0)
                def _():
                    cast_weights()

            @pl.when(has_mid)
            def _():
                up(start_mid, mid, 0)
                down(start_mid, mid, 0, first, last)

            @pl.when(has_small)
            def _():
                up(start_small, sub, 0)
                down(start_small, sub, 0, first, last)

        @pl.when(nsub > 0)
        def _():
            @pl.when(j == 0)
            def _():
                each_sub(lambda s: in_copy(s).start())
                cast_weights()

                def to_bf16(s):
                    x_ref[rows(s * sub, sub), :] = acc_ref[rows(s * sub, sub), :].astype(BF16)

                each_sub(lambda s: in_copy(s).wait())
                each_sub(to_bf16)
                compute(True, False, False)

            @pl.when((j > 0) & (j < nf - 1))
            def _():
                compute(False, False, True)

            @pl.when(j == nf - 1)
            def _():
                compute(False, True, True)

                def wait_big(s, c):
                    out_copy(0, big).wait()
                    return c

                lax.fori_loop(0, nbig, wait_big, 0)

                @pl.when(has_mid)
                def _():
                    out_copy(0, mid).wait()

                @pl.when(has_small)
                def _():
                    out_copy(0, sub).wait()

    def widx(g, j, se, sr, sn):
        return (0, se[g], 0, jnp.where(sn[g] > 0, j, nf - 1))

    def didx(g, j, se, sr, sn):
        return (0, se[g], jnp.where(sn[g] > 0, j, nf - 1), 0)

    return pl.pallas_call(
        kern,
        grid_spec=pltpu.PrefetchScalarGridSpec(
            num_scalar_prefetch=3, grid=(ng, nf),
            in_specs=[pl.BlockSpec(memory_space=pl.ANY),
                      pl.BlockSpec((None, None, k, tf), widx),
                      pl.BlockSpec((None, None, k, tf), widx),
                      pl.BlockSpec((None, None, tf, d), didx)],
            out_specs=pl.BlockSpec(memory_space=pl.ANY),
            scratch_shapes=[pltpu.VMEM((rs, k), BF16), pltpu.VMEM((rs, d), F32),
                            pltpu.VMEM((2, 4 * sub, tf), BF16),
                            pltpu.VMEM((k, tf), BF16), pltpu.VMEM((k, tf), BF16),
                            pltpu.VMEM((tf, d), BF16),
                            pltpu.SemaphoreType.DMA(()), pltpu.SemaphoreType.DMA(())]),
        out_shape=SDS((n_rows, d), F32),
        compiler_params=_cparams(2), name="moe_ffn")(st_e, st_row, st_nsub, xs, wg4, wu4, wd4)


def _moe_combine(pos_flat, h, gates, ys, g, mp, ms, tp):
    m, d = h.shape
    n_p = mp // tp
    assert mp % tp == 0 and ms <= tp and ms % 8 == 0

    def kern(pos_ref, h_ref, gate_ref, ys_ref, g_ref, yp_ref, ysm_ref, a_ref, b_ref, sem):
        i = pl.program_id(0)

        def copies(tile, r):
            t = tile * tp + r
            slot = tile & 1
            return (pltpu.make_async_copy(ys_ref.at[pl.ds(pos_ref[TOP_K * t], 1)],
                                          a_ref.at[slot, pl.ds(r, 1)], sem.at[slot]),
                    pltpu.make_async_copy(ys_ref.at[pl.ds(pos_ref[TOP_K * t + 1], 1)],
                                          b_ref.at[slot, pl.ds(r, 1)], sem.at[slot]))

        def gather(tile):
            def issue(r, c):
                ca, cb = copies(tile, r)
                ca.start(priority=0)
                cb.start(priority=1)
                return c

            @pl.when(tile < n_p)
            def _():
                lax.fori_loop(0, tp, issue, 0, unroll=8)

            @pl.when(tile == n_p)
            def _():
                lax.fori_loop(0, ms, issue, 0, unroll=8)

        def combined(n):
            def drain(r, c):
                ca, cb = copies(i, 0)
                ca.wait()
                cb.wait()
                return c

            lax.fori_loop(0, n, drain, 0, unroll=8)
            slot = i & 1
            gt = gate_ref[0:n, :]
            hh = (h_ref[0:n, :] + gt[:, 0:1] * a_ref[slot, 0:n, :]
                  + gt[:, 1:2] * b_ref[slot, 0:n, :])
            return _rms(hh, g_ref[...])

        @pl.when(i == 0)
        def _():
            gather(i)

        gather(i + 1)

        @pl.when(i < n_p)
        def _():
            yp_ref[...] = combined(tp)

        @pl.when(i == n_p)
        def _():
            ysm_ref[...] = combined(ms)

    return pl.pallas_call(
        kern,
        grid_spec=pltpu.PrefetchScalarGridSpec(
            num_scalar_prefetch=1, grid=(n_p + 1,),
            in_specs=[pl.BlockSpec((tp, d), lambda i, *_: (i, 0)),
                      pl.BlockSpec((tp, TOP_K), lambda i, *_: (i, 0)),
                      pl.BlockSpec(memory_space=pl.ANY),
                      pl.BlockSpec((1, d), lambda i, *_: (0, 0))],
            out_specs=[pl.BlockSpec((tp, d), lambda i, *_: (jnp.minimum(i, n_p - 1), 0)),
                       pl.BlockSpec((ms, d), lambda i, *_: (0, 0))],
            scratch_shapes=[pltpu.VMEM((2, tp, d), F32), pltpu.VMEM((2, tp, d), F32),
                            pltpu.SemaphoreType.DMA((2,))]),
        out_shape=[SDS((mp, d), F32), SDS((ms, d), F32)],
        compiler_params=_cparams(1), name="moe_combine")(pos_flat, h, gates, ys, g)


def _moe_tables(cnt, sub, rs, ng):
    ne = cnt.shape[0]
    nsub_e = (cnt + sub - 1) // sub
    size_e = nsub_e * sub
    off = jnp.cumsum(size_e) - size_e
    spr = rs // sub
    nst_e = (nsub_e + spr - 1) // spr
    st_start = jnp.cumsum(nst_e) - nst_e
    n_act = jnp.sum(nst_e)
    gidx = jnp.arange(ng, dtype=I32)
    e_of = jnp.sum((gidx[:, None] >= st_start[None, :]).astype(I32), axis=1) - 1
    e_of = jnp.clip(e_of, 0, ne - 1)
    kth = gidx - st_start[e_of]
    active = gidx < n_act
    nsub = jnp.where(active, jnp.clip(nsub_e[e_of] - kth * spr, 0, spr), 0)
    row = off[e_of] + kth * rs
    last = jnp.maximum(n_act - 1, 0)
    st_e = jnp.where(active, e_of, e_of[last]).astype(I32)
    st_row = jnp.where(active, row, 0).astype(I32)
    n_pad_e = size_e - cnt
    pad_start = jnp.cumsum(n_pad_e) - n_pad_e
    n_pad = jnp.sum(n_pad_e)
    pidx = jnp.arange(ne * (sub - 1), dtype=I32)
    pe = jnp.clip(jnp.sum((pidx[:, None] >= pad_start[None, :]).astype(I32), axis=1) - 1, 0, ne - 1)
    pad_rows = jnp.where(pidx < n_pad, off[pe] + cnt[pe] + (pidx - pad_start[pe]), 0).astype(I32)
    return off, st_e, st_row, nsub.astype(I32), pad_rows, n_pad.astype(I32).reshape(1)


def kernel(x_prompt, x_sample, state_rg_conv, state_rg_h, state_gla, state_sc_conv, norm_mix_e, w_in_e, rg_conv_w, rg_conv_b, rg_w_a, rg_b_a, rg_w_x, rg_b_x, rg_lambda, gla_w_gate, gla_b_gate, gla_norm, w_out_e, norm_ffn_e, ffn_w_gate, ffn_w_up, ffn_w_down, norm_mix_o, w_in_o, sc_conv_w, w_out_o, norm_ffn_o, router_w, moe_w_gate, moe_w_up, moe_w_down, final_norm):
    bsz, t, d = x_prompt.shape
    ms = x_sample.shape[0]
    assert x_sample.shape[1] == 1 and w_in_e.shape[0] == 1 and w_in_o.shape[0] == 1
    mp = bsz * t
    m = mp + ms
    d_rnn = rg_lambda.shape[1]
    nh, dk, dv = state_gla.shape[2:]
    hk, hv = nh * dk, nh * dv
    n_main = 2 * d_rnn + 2 * hk + 2 * hv
    d_mix = d_rnn + hv
    ne = router_w.shape[2]
    xp = x_prompt.reshape(mp, d)
    xs = x_sample.reshape(ms, d)
    row = lambda v: v.reshape(1, -1)

    tp = _tile(mp, TP_TARGET, 16)
    tm = _tile(m, TM_TARGET, 16)
    tm_in = _tile(m, TM_IN_TARGET, 16)
    tm_out = _tile(m, TM_OUT_TARGET, 16)

    hn0 = _norm_in(xp, xs, norm_mix_e, tp)
    w_in_t = jnp.swapaxes(w_in_e, 1, 2)
    z0 = _in_proj0(hn0, w_in_t, n_main, tm_in, _tile(n_main, 1024, LANE))
    la = _gla_gate(hn0, w_in_t, n_main, gla_w_gate[0], gla_b_gate, tm)
    y_mix, rgc_p, rgh_p = _rglru_prompt(z0, rg_conv_w[0], rg_conv_b, rg_w_a[0], rg_b_a, rg_w_x[0],
                                        rg_b_x, rg_lambda, bsz, t, d_rnn, d_mix)
    y_mix, rgc_s, rgh_s = _rglru_sample(z0, y_mix, state_rg_conv[0], state_rg_h[0], rg_conv_w[0],
                                        rg_conv_b, rg_w_a[0], rg_b_a, rg_w_x[0], rg_b_x, rg_lambda,
                                        mp, ms, d_rnn)
    y_mix, gla_p = _gla_prompt(z0, la, y_mix, gla_norm, bsz, t, nh, dk, dv, 2 * d_rnn, d_rnn)
    y_mix, gla_s = _gla_sample(z0, la, y_mix, state_gla, gla_norm, mp, ms, nh, dk, dv,
                               2 * d_rnn, d_rnn)
    h1, hn1 = _out_proj(y_mix, _cast_bf16(w_out_e), xp, norm_ffn_e, tm_out, res_tail=xs)

    f0 = _ffn(hn1, ffn_w_gate, ffn_w_up, ffn_w_down, tm, _tile(ffn_w_gate.shape[2], 512, LANE))
    h2, hn2 = _add_norm(h1, f0, norm_mix_o, tm)

    gb, cv = _in_proj1(hn2, w_in_o, tm_in, _tile(w_in_o.shape[2] // 3, 256, LANE))
    u, sc_p = _shortconv_prompt(gb, cv, sc_conv_w[0], bsz, t)
    u, sc_s = _shortconv_sample(gb, cv, u, state_sc_conv[0], sc_conv_w[0], mp, ms)
    h3, hn3, mh, gd = _out_proj(u, _cast_bf16(w_out_o), h2, norm_ffn_o, tm_out,
                                router_w=router_w[0])

    sub = MOE_SUB
    rs = MOE_SPR * sub
    ng = (TOP_K * m) // rs + ne
    n_rows = TOP_K * m + ne * (sub - 1)
    n_rows = ((n_rows + sub - 1) // sub) * sub
    ex, cnt = _moe_rank(mh, tm)
    off, st_e, st_row, st_nsub, pad_rows, n_pad = _moe_tables(cnt[0].astype(I32), sub, rs, ng)
    pos, gates = _moe_pos(mh, gd, ex, off.astype(F32).reshape(1, ne), tm)
    pos_flat = pos.reshape(TOP_K * m)
    xsort = _moe_scatter(pos_flat, pad_rows, n_pad, hn3, n_rows, tm)
    ys = _moe_ffn(st_e, st_row, st_nsub, xsort, moe_w_gate, moe_w_up, moe_w_down, rs, sub,
                  _tile(moe_w_gate.shape[3], 256, LANE))
    y_p, y_s = _moe_combine(pos_flat, h3, gates, ys, row(final_norm), mp, ms, tp)

    return (y_p.reshape(bsz, t, d), y_s.reshape(ms, 1, d),
            rgc_p[None], rgc_s[None], rgh_p.reshape(1, bsz, d_rnn), rgh_s[None],
            gla_p[None], gla_s, sc_p[None], sc_s[None])
```

```python
import jax
import jax.numpy as jnp
from jax import lax
from jax.experimental import pallas as pl
from jax.experimental.pallas import tpu as pltpu

F32 = jnp.float32
BF16 = jnp.bfloat16
I32 = jnp.int32
SDS = jax.ShapeDtypeStruct

EPS = 1e-6
RG_C = 8.0
GLA_TAU = 16.0
GLA_CHUNK = 64
TOP_K = 2
LANE = 128
SUB8 = 8
VMEM_LIMIT = 56 * 1024 * 1024
ARB = "arbitrary"
TM_TARGET = 640
TM_IN_TARGET = 1664
TM_OUT_TARGET = 320
TP_TARGET = 512
GLA_TB_TARGET = 512
MOE_SUB = 128
MOE_SPR = 20


def _cparams(n_axes, vmem=VMEM_LIMIT):
    return pltpu.CompilerParams(dimension_semantics=(ARB,) * n_axes, vmem_limit_bytes=vmem)


def _tile(n, target, align):
    best = None
    for t in range(align, min(n, target) + 1, align):
        if n % t == 0:
            best = t
    assert best is not None, (n, target, align)
    return best


def _rms(xf, g):
    ms = jnp.mean(xf * xf, axis=-1, keepdims=True)
    return xf * lax.rsqrt(ms + EPS) * g


def _sigmoid(x):
    return 1.0 / (1.0 + jnp.exp(-x))


def _softplus(x):
    return jnp.maximum(x, 0.0) + jnp.log1p(jnp.exp(-jnp.abs(x)))


def _gelu_tanh(x):
    c = 0.7978845608028654
    return x * (0.5 * (1.0 + jnp.tanh(c * (x + 0.044715 * (x * x * x)))))


def _dot(a, b):
    return jnp.dot(a, b, preferred_element_type=F32)


def _shift_rows(x, s, row, fill):
    return jnp.where(row >= s, pltpu.roll(x, s, 0), fill)


def _norm_in(xp, xs, g, tp):
    mp, d = xp.shape
    ms = xs.shape[0]
    n_p = mp // tp
    assert mp % tp == 0 and ms <= tp and ms % 16 == 0

    def kern(xp_ref, xs_ref, g_ref, o_ref):
        i = pl.program_id(0)

        @pl.when(i < n_p)
        def _():
            o_ref[...] = _rms(xp_ref[...], g_ref[...]).astype(BF16)

        @pl.when(i == n_p)
        def _():
            o_ref[0:ms, :] = _rms(xs_ref[...], g_ref[...]).astype(BF16)

    return pl.pallas_call(
        kern, grid=(n_p + 1,),
        in_specs=[pl.BlockSpec((tp, d), lambda i: (jnp.minimum(i, n_p - 1), 0)),
                  pl.BlockSpec((ms, d), lambda i: (0, 0)),
                  pl.BlockSpec((1, d), lambda i: (0, 0))],
        out_specs=pl.BlockSpec((tp, d), lambda i: (i, 0)),
        out_shape=SDS((mp + ms, d), BF16),
        compiler_params=_cparams(1), name="norm_in")(xp, xs, g)


def _in_proj0(x, wt3, n_cols, tm, tn):
    m, k = x.shape

    def kern(x_ref, w_ref, o_ref, wb_ref):
        @pl.when(pl.program_id(1) == 0)
        def _():
            wb_ref[...] = w_ref[...].T.astype(BF16)

        o_ref[...] = _dot(x_ref[...], wb_ref[...])

    return pl.pallas_call(
        kern, grid=(n_cols // tn, m // tm),
        in_specs=[pl.BlockSpec((tm, k), lambda n, i: (i, 0)),
                  pl.BlockSpec((None, tn, k), lambda n, i: (0, n, 0))],
        out_specs=pl.BlockSpec((tm, tn), lambda n, i: (i, n)),
        out_shape=SDS((m, n_cols), F32),
        scratch_shapes=[pltpu.VMEM((k, tn), BF16)],
        compiler_params=_cparams(2), name="in_proj0")(x, wt3)


def _gla_gate(x, wt3, col0, w_gate, b_gate, tm):
    m, k = x.shape
    r, n = w_gate.shape
    assert col0 % r == 0

    def kern(x_ref, wl_ref, wg_ref, bg_ref, o_ref):
        lr = lax.dot_general(x_ref[...], wl_ref[...].astype(BF16), (((1,), (1,)), ((), ())),
                             preferred_element_type=F32)
        pre = _dot(lr.astype(BF16), wg_ref[...].astype(BF16)) + bg_ref[...]
        o_ref[...] = -_softplus(-pre) * (1.0 / GLA_TAU)

    return pl.pallas_call(
        kern, grid=(m // tm,),
        in_specs=[pl.BlockSpec((tm, k), lambda i: (i, 0)),
                  pl.BlockSpec((None, r, k), lambda i: (0, col0 // r, 0)),
                  pl.BlockSpec((r, n), lambda i: (0, 0)),
                  pl.BlockSpec((1, n), lambda i: (0, 0))],
        out_specs=pl.BlockSpec((tm, n), lambda i: (i, 0)),
        out_shape=SDS((m, n), F32),
        compiler_params=_cparams(1), name="gla_gate")(x, wt3, w_gate, b_gate)


def _rg_gates(xc, wa_ref, wx_ref, ba_ref, bx_ref, lam_ref):
    w2 = jnp.concatenate([wa_ref[...], wx_ref[...]], axis=1).astype(BF16)
    pre = _dot(xc.astype(BF16), w2)
    r = _sigmoid(pre[:, :LANE] + ba_ref[...])
    i = _sigmoid(pre[:, LANE:] + bx_ref[...])
    log_a = (-RG_C) * r * _softplus(-lam_ref[...])
    a = jnp.exp(log_a)
    mult = jnp.sqrt(jnp.tanh(-log_a) * (1.0 + a * a))
    return a, mult, i


def _rglru_prompt(z, cw, cb, wa, ba, wx, bx, lam, bsz, t, d_rnn, d_out):
    nh = d_rnn // LANE
    m = z.shape[0]

    def kern(xr_ref, gr_ref, cw_ref, cb_ref, wa_ref, ba_ref, wx_ref, bx_ref, lam_ref,
             y_ref, conv_ref, h_ref, a_s, b_s, c_s):
        xr = xr_ref[...]
        row = lax.broadcasted_iota(I32, (t, LANE), 0)
        w = cw_ref[...]
        nw = w.shape[0]
        xc = w[nw - 1:nw, :] * xr
        for s in range(1, nw):
            xc = xc + w[nw - 1 - s:nw - s, :] * _shift_rows(xr, s, row, 0.0)
        xc = xc + cb_ref[...]
        a, mult, i = _rg_gates(xc, wa_ref, wx_ref, ba_ref, bx_ref, lam_ref)
        mult = jnp.where(row == 0, 1.0, mult)
        b = mult * i * xc
        ng = t // SUB8
        a3, b3 = a.reshape(ng, SUB8, LANE), b.reshape(ng, SUB8, LANE)
        r8 = lax.broadcasted_iota(I32, (ng, SUB8, LANE), 1)
        s = 1
        while s < SUB8:
            b3 = a3 * jnp.where(r8 >= s, pltpu.roll(b3, s, 1), 0.0) + b3
            a3 = a3 * jnp.where(r8 >= s, pltpu.roll(a3, s, 1), 1.0)
            s *= 2
        a_s[...] = a3.reshape(t, LANE)
        b_s[...] = b3.reshape(t, LANE)
        ag = a_s[pl.ds(SUB8 - 1, ng, stride=SUB8), :]
        bg = b_s[pl.ds(SUB8 - 1, ng, stride=SUB8), :]
        rowg = lax.broadcasted_iota(I32, (ng, LANE), 0)
        s = 1
        while s < ng:
            bg = ag * _shift_rows(bg, s, rowg, 0.0) + bg
            ag = ag * _shift_rows(ag, s, rowg, 1.0)
            s *= 2
        carry = _shift_rows(bg, 1, rowg, 0.0)
        for k in range(SUB8):
            c_s[pl.ds(k, ng, stride=SUB8), :] = carry
        h = b_s[...] + a_s[...] * c_s[...]
        y_ref[...] = (h * _gelu_tanh(gr_ref[...])).astype(BF16)
        conv_ref[...] = xr[t - (nw - 1):t, :]
        h_ref[...] = h[t - 1:t, :]

    vec = pl.BlockSpec((1, LANE), lambda b_, h: (0, h))
    blk = pl.BlockSpec((None, LANE, LANE), lambda b_, h: (h, 0, 0))
    nw = cw.shape[0]
    return pl.pallas_call(
        kern, grid=(bsz, nh),
        in_specs=[pl.BlockSpec((t, LANE), lambda b_, h: (b_, h)),
                  pl.BlockSpec((t, LANE), lambda b_, h: (b_, nh + h)),
                  pl.BlockSpec((nw, LANE), lambda b_, h: (0, h)),
                  vec, blk, vec, blk, vec, vec],
        out_specs=[pl.BlockSpec((t, LANE), lambda b_, h: (b_, h)),
                   pl.BlockSpec((None, nw - 1, LANE), lambda b_, h: (b_, 0, h)),
                   pl.BlockSpec((None, 1, LANE), lambda b_, h: (b_, 0, h))],
        out_shape=[SDS((m, d_out), BF16), SDS((bsz, nw - 1, d_rnn), F32), SDS((bsz, 1, d_rnn), F32)],
        scratch_shapes=[pltpu.VMEM((t, LANE), F32)] * 3,
        compiler_params=_cparams(2), name="rglru_prompt")(z, z, cw, cb, wa, ba, wx, bx, lam)


def _rglru_sample(z, y_mix, st_conv, st_h, cw, cb, wa, ba, wx, bx, lam, mp, ms, d_rnn):
    nh = d_rnn // LANE
    nw = cw.shape[0]
    rb = mp // ms
    assert mp % ms == 0

    def kern(xr_ref, gr_ref, y_in_ref, sc_ref, sh_ref, cw_ref, cb_ref, wa_ref, ba_ref, wx_ref,
             bx_ref, lam_ref, y_ref, conv_ref, h_ref):
        del y_in_ref
        xr = xr_ref[...]
        w = cw_ref[...]
        xc = w[nw - 1:nw, :] * xr
        for s in range(nw - 1):
            xc = xc + w[s:s + 1, :] * sc_ref[:, s, :]
        xc = xc + cb_ref[...]
        a, mult, i = _rg_gates(xc, wa_ref, wx_ref, ba_ref, bx_ref, lam_ref)
        h = a * sh_ref[...] + mult * i * xc
        y_ref[...] = (h * _gelu_tanh(gr_ref[...])).astype(BF16)
        for s in range(nw - 2):
            conv_ref[:, s, :] = sc_ref[:, s + 1, :]
        conv_ref[:, nw - 2, :] = xr
        h_ref[...] = h

    vec = pl.BlockSpec((1, LANE), lambda h: (0, h))
    blk = pl.BlockSpec((None, LANE, LANE), lambda h: (h, 0, 0))
    return pl.pallas_call(
        kern, grid=(nh,),
        in_specs=[pl.BlockSpec((ms, LANE), lambda h: (rb, h)),
                  pl.BlockSpec((ms, LANE), lambda h: (rb, nh + h)),
                  pl.BlockSpec(memory_space=pl.ANY),
                  pl.BlockSpec((ms, nw - 1, LANE), lambda h: (0, 0, h)),
                  pl.BlockSpec((ms, LANE), lambda h: (0, h)),
                  pl.BlockSpec((nw, LANE), lambda h: (0, h)),
                  vec, blk, vec, blk, vec, vec],
        out_specs=[pl.BlockSpec((ms, LANE), lambda h: (rb, h)),
                   pl.BlockSpec((ms, nw - 1, LANE), lambda h: (0, 0, h)),
                   pl.BlockSpec((ms, LANE), lambda h: (0, h))],
        out_shape=[SDS(y_mix.shape, BF16), SDS((ms, nw - 1, d_rnn), F32), SDS((ms, d_rnn), F32)],
        input_output_aliases={2: 0},
        compiler_params=_cparams(1), name="rglru_sample")(
            z, z, y_mix, st_conv, st_h, cw, cb, wa, ba, wx, bx, lam)


def _gla_out(o, gn, g):
    return _rms(o, gn) * (g * _sigmoid(g))


def _gla_prompt(z, la, y_mix, gn, bsz, t, nh, dk, dv, col_q, col_y):
    tb = _tile(t, GLA_TB_TARGET, GLA_CHUNK)
    nt = t // tb
    nc = tb // GLA_CHUNK
    c = GLA_CHUNK
    hk, hv = nh * dk, nh * dv
    assert col_q % hk == 0 and (col_q + 2 * hk) % hv == 0 and col_y % hv == 0
    cq, ck = col_q // hk, col_q // hk + 1
    cv, cg = (col_q + 2 * hk) // hv, (col_q + 2 * hk) // hv + 1
    scale = dk ** -0.5

    def kern(q_ref, k_ref, v_ref, g_ref, la_ref, gn_ref, y_in_ref, y_ref, s_ref, st_ref):
        del y_in_ref
        tbi = pl.program_id(1)

        @pl.when(tbi == 0)
        def _():
            st_ref[...] = jnp.zeros_like(st_ref)

        row = lax.broadcasted_iota(I32, (c, dk), 0)
        causal = (lax.broadcasted_iota(I32, (c, c), 0) >= lax.broadcasted_iota(I32, (c, c), 1))

        def chunk(ci, carry):
            rows = pl.ds(pl.multiple_of(ci * c, c), c)
            for hd in range(nh):
                ks = slice(hd * dk, (hd + 1) * dk)
                vs = slice(hd * dv, (hd + 1) * dv)
                q = q_ref[rows, ks] * scale
                k = k_ref[rows, ks]
                v = v_ref[rows, vs].astype(BF16)
                bc = la_ref[rows, ks]
                s = 1
                while s < c:
                    bc = bc + _shift_rows(bc, s, row, 0.0)
                    s *= 2
                b_last = bc[c - 1:c, :]
                b_mid = bc[c // 2 - 1:c // 2, :]
                qe = (q * jnp.exp(bc)).astype(BF16)
                qm = (q * jnp.exp(bc - b_mid)).astype(BF16)
                km = (k * jnp.exp(b_mid - bc)).astype(BF16)
                kd = (k * jnp.exp(b_last - bc)).astype(BF16)
                st = st_ref[hd]
                o = lax.dot_general(qe, st.astype(BF16), (((1,), (1,)), ((), ())),
                                    preferred_element_type=F32)
                attn = lax.dot_general(qm, km, (((1,), (1,)), ((), ())),
                                       preferred_element_type=F32)
                attn = jnp.where(causal, attn, 0.0).astype(BF16)
                o = o + _dot(attn, v)
                st_ref[hd] = st * jnp.exp(b_last) + lax.dot_general(
                    v, kd, (((0,), (0,)), ((), ())), preferred_element_type=F32)
                y_ref[rows, vs] = _gla_out(o, gn_ref[...], g_ref[rows, vs]).astype(BF16)
            return carry

        lax.fori_loop(0, nc, chunk, 0)

        @pl.when(tbi == nt - 1)
        def _():
            for hd in range(nh):
                s_ref[hd] = st_ref[hd].T

    m = z.shape[0]
    return pl.pallas_call(
        kern, grid=(bsz, nt),
        in_specs=[pl.BlockSpec((tb, hk), lambda b_, i: (b_ * nt + i, cq)),
                  pl.BlockSpec((tb, hk), lambda b_, i: (b_ * nt + i, ck)),
                  pl.BlockSpec((tb, hv), lambda b_, i: (b_ * nt + i, cv)),
                  pl.BlockSpec((tb, hv), lambda b_, i: (b_ * nt + i, cg)),
                  pl.BlockSpec((tb, hk), lambda b_, i: (b_ * nt + i, 0)),
                  pl.BlockSpec((1, dv), lambda b_, i: (0, 0)),
                  pl.BlockSpec(memory_space=pl.ANY)],
        out_specs=[pl.BlockSpec((tb, hv), lambda b_, i: (b_ * nt + i, col_y // hv)),
                   pl.BlockSpec((None, nh, dk, dv), lambda b_, i: (b_, 0, 0, 0))],
        out_shape=[SDS(y_mix.shape, BF16), SDS((bsz, nh, dk, dv), F32)],
        scratch_shapes=[pltpu.VMEM((nh, dv, dk), F32)],
        input_output_aliases={6: 0},
        compiler_params=_cparams(2), name="gla_prompt")(z, z, z, z, la, gn, y_mix)


def _gla_sample(z, la, y_mix, st, gn, mp, ms, nh, dk, dv, col_q, col_y):
    bb = 16
    assert ms % bb == 0 and mp % ms == 0 and dk == LANE
    ns = ms // bb
    hk, hv = nh * dk, nh * dv
    cq, ck = col_q // hk, col_q // hk + 1
    cv, cg = (col_q + 2 * hk) // hv, (col_q + 2 * hk) // hv + 1
    scale = dk ** -0.5

    def kern(q_ref, k_ref, la_ref, v_ref, g_ref, gn_ref, st_ref, y_in_ref, y_ref, so_ref,
             qt_ref, kt_ref, at_ref):
        del y_in_ref
        i = pl.program_id(0)

        @pl.when(i == 0)
        def _():
            for hd in range(nh):
                ks = slice(hd * dk, (hd + 1) * dk)
                qt = (q_ref[:, ks] * scale).T
                kt = k_ref[:, ks].T
                at = jnp.exp(la_ref[:, ks]).T
                for s in range(ns):
                    qt_ref[s, hd] = qt[:, s * bb:(s + 1) * bb]
                    kt_ref[s, hd] = kt[:, s * bb:(s + 1) * bb]
                    at_ref[s, hd] = at[:, s * bb:(s + 1) * bb]

        for hd in range(nh):
            vs = slice(hd * dv, (hd + 1) * dv)
            qt = qt_ref[i, hd]
            kt = kt_ref[i, hd]
            at = at_ref[i, hd]
            outs = []
            for j in range(bb):
                v = v_ref[j:j + 1, vs]
                s_new = at[:, j:j + 1] * st_ref[j, hd] + kt[:, j:j + 1] * v
                so_ref[j, hd] = s_new
                outs.append(jnp.sum(qt[:, j:j + 1] * s_new, axis=0, keepdims=True))
            o = jnp.concatenate(outs, axis=0)
            y_ref[:, vs] = _gla_out(o, gn_ref[...], g_ref[:, vs]).astype(BF16)

    rb = mp // ms
    rbb = mp // bb
    return pl.pallas_call(
        kern, grid=(ns,),
        in_specs=[pl.BlockSpec((ms, hk), lambda i: (rb, cq)),
                  pl.BlockSpec((ms, hk), lambda i: (rb, ck)),
                  pl.BlockSpec((ms, hk), lambda i: (rb, 0)),
                  pl.BlockSpec((bb, hv), lambda i: (rbb + i, cv)),
                  pl.BlockSpec((bb, hv), lambda i: (rbb + i, cg)),
                  pl.BlockSpec((1, dv), lambda i: (0, 0)),
                  pl.BlockSpec((None, bb, nh, dk, dv), lambda i: (0, i, 0, 0, 0)),
                  pl.BlockSpec(memory_space=pl.ANY)],
        out_specs=[pl.BlockSpec((bb, hv), lambda i: (rbb + i, col_y // hv)),
                   pl.BlockSpec((None, bb, nh, dk, dv), lambda i: (0, i, 0, 0, 0))],
        out_shape=[SDS(y_mix.shape, BF16), SDS((1, ms, nh, dk, dv), F32)],
        scratch_shapes=[pltpu.VMEM((ns, nh, dk, bb), F32)] * 3,
        input_output_aliases={7: 0},
        compiler_params=_cparams(1), name="gla_sample")(z, z, la, z, z, gn, st, y_mix)


def _cast_bf16(w3):
    _, k, n = w3.shape
    tk = _tile(k, 512, 16)

    def kern(w_ref, o_ref):
        o_ref[...] = w_ref[...].astype(BF16)

    return pl.pallas_call(
        kern, grid=(k // tk,),
        in_specs=[pl.BlockSpec((None, tk, n), lambda i: (0, i, 0))],
        out_specs=pl.BlockSpec((tk, n), lambda i: (i, 0)),
        out_shape=SDS((k, n), BF16),
        compiler_params=_cparams(1), name="cast_bf16")(w3)


def _route(hn, rw):
    logits = _dot(hn.astype(BF16), rw.astype(BF16))
    ne = float(logits.shape[1])
    lane = lax.broadcasted_iota(I32, logits.shape, 1).astype(F32)
    m1 = jnp.max(logits, axis=1, keepdims=True)
    i1 = jnp.min(jnp.where(logits == m1, lane, ne), axis=1, keepdims=True)
    sel1 = lane == i1
    rest = jnp.where(sel1, -jnp.inf, logits)
    m2 = jnp.max(rest, axis=1, keepdims=True)
    i2 = jnp.min(jnp.where(rest == m2, lane, ne), axis=1, keepdims=True)
    sel2 = lane == i2
    e2 = jnp.exp(m2 - m1)
    g1 = 1.0 / (1.0 + e2)
    g2 = e2 / (1.0 + e2)
    mh = jnp.where(sel1 | sel2, 1.0, 0.0)
    gd = jnp.where(sel1, g1, 0.0) + jnp.where(sel2, g2, 0.0)
    return mh, gd


def _out_proj(y, wb, res, g, tm, res_tail=None, router_w=None):
    m, k = y.shape
    d = wb.shape[1]
    n_m = m // tm
    split = res_tail is not None
    if split:
        mp, ms = res.shape[0], res_tail.shape[0]
        assert mp + ms == m and ms <= tm
        head = tm - ms
        n_rb = pl.cdiv(mp, tm)
    route = router_w is not None

    def kern(*refs):
        it = iter(refs)
        y_ref, w_ref, r_ref = next(it), next(it), next(it)
        rt_ref = next(it) if split else None
        g_ref = next(it)
        rw_ref = next(it) if route else None
        h_ref, n_ref = next(it), next(it)
        mh_ref, gd_ref = (next(it), next(it)) if route else (None, None)
        acc_ref = next(it)
        i = pl.program_id(0)

        def matmul():
            acc_ref[i & 1] = _dot(y_ref[...], w_ref[...])

        def epilogue():
            acc = acc_ref[(i - 1) & 1]
            if split:
                last = i == n_m
                tail = acc[head:] + jnp.where(last, rt_ref[...], r_ref[head:tm, :])
                h = tail if head == 0 else jnp.concatenate([acc[:head] + r_ref[0:head, :], tail], 0)
            else:
                h = acc + r_ref[...]
            h_ref[...] = h
            hn = _rms(h, g_ref[...])
            if route:
                n_ref[...] = hn
                mh, gd = _route(hn, rw_ref[...])
                mh_ref[...] = mh
                gd_ref[...] = gd
            else:
                n_ref[...] = hn.astype(BF16)

        @pl.when(i < n_m)
        def _():
            matmul()

        @pl.when(i > 0)
        def _():
            epilogue()

    def lag(i):
        return jnp.maximum(i - 1, 0)

    in_specs = [pl.BlockSpec((tm, k), lambda i: (jnp.minimum(i, n_m - 1), 0)),
                pl.BlockSpec((k, d), lambda i: (0, 0))]
    args = [y, wb]
    if split:
        in_specs += [pl.BlockSpec((tm, d), lambda i: (jnp.minimum(lag(i), n_rb - 1), 0)),
                     pl.BlockSpec((ms, d), lambda i: (0, 0))]
        args += [res, res_tail]
    else:
        in_specs += [pl.BlockSpec((tm, d), lambda i: (lag(i), 0))]
        args += [res]
    in_specs += [pl.BlockSpec((1, d), lambda i: (0, 0))]
    args += [g]
    out_specs = [pl.BlockSpec((tm, d), lambda i: (lag(i), 0))]
    out_shape = [SDS((m, d), F32)]
    if route:
        ne = router_w.shape[1]
        in_specs += [pl.BlockSpec((d, ne), lambda i: (0, 0))]
        args += [router_w]
        out_specs += [pl.BlockSpec((tm, d), lambda i: (lag(i), 0)),
                      pl.BlockSpec((tm, ne), lambda i: (lag(i), 0)),
                      pl.BlockSpec((tm, ne), lambda i: (lag(i), 0))]
        out_shape += [SDS((m, d), F32), SDS((m, ne), F32), SDS((m, ne), F32)]
    else:
        out_specs += [pl.BlockSpec((tm, d), lambda i: (lag(i), 0))]
        out_shape += [SDS((m, d), BF16)]
    return pl.pallas_call(
        kern, grid=(n_m + 1,), in_specs=in_specs, out_specs=out_specs, out_shape=out_shape,
        scratch_shapes=[pltpu.VMEM((2, tm, d), F32)],
        compiler_params=_cparams(1), name="out_proj_route" if route else "out_proj")(*args)


def _ffn(x, wg3, wu3, wd3, tm, tf):
    m, k = x.shape
    f = wg3.shape[2]
    d = wd3.shape[2]

    def kern(x_ref, wg_ref, wu_ref, wd_ref, o_ref):
        @pl.when(pl.program_id(1) == 0)
        def _():
            o_ref[...] = jnp.zeros_like(o_ref)

        xb = x_ref[...]
        gg = _dot(xb, wg_ref[...].astype(BF16))
        uu = _dot(xb, wu_ref[...].astype(BF16))
        a = (gg * _sigmoid(gg) * uu).astype(BF16)
        o_ref[...] += _dot(a, wd_ref[...].astype(BF16))

    return pl.pallas_call(
        kern, grid=(m // tm, f // tf),
        in_specs=[pl.BlockSpec((tm, k), lambda i, j: (i, 0)),
                  pl.BlockSpec((None, k, tf), lambda i, j: (0, 0, j)),
                  pl.BlockSpec((None, k, tf), lambda i, j: (0, 0, j)),
                  pl.BlockSpec((None, tf, d), lambda i, j: (0, j, 0))],
        out_specs=pl.BlockSpec((tm, d), lambda i, j: (i, 0)),
        out_shape=SDS((m, d), F32),
        compiler_params=_cparams(2), name="ffn")(x, wg3, wu3, wd3)


def _add_norm(a, b, g, tm):
    m, d = a.shape

    def kern(a_ref, b_ref, g_ref, h_ref, n_ref):
        h = a_ref[...] + b_ref[...]
        h_ref[...] = h
        n_ref[...] = _rms(h, g_ref[...]).astype(BF16)

    row = pl.BlockSpec((tm, d), lambda i: (i, 0))
    return pl.pallas_call(
        kern, grid=(m // tm,),
        in_specs=[row, row, pl.BlockSpec((1, d), lambda i: (0, 0))],
        out_specs=[row, row],
        out_shape=[SDS((m, d), F32), SDS((m, d), BF16)],
        compiler_params=_cparams(1), name="add_norm")(a, b, g)


def _in_proj1(x, w3, tm, tn):
    m, k = x.shape
    dc = w3.shape[2] // 3
    nb = dc // tn

    def kern(x_ref, wb_ref, wc_ref, wv_ref, gb_ref, cv_ref, sb_ref, sc_ref, sv_ref):
        @pl.when(pl.program_id(1) == 0)
        def _():
            sb_ref[...] = wb_ref[...].astype(BF16)
            sc_ref[...] = wc_ref[...].astype(BF16)
            sv_ref[...] = wv_ref[...].astype(BF16)

        xb = x_ref[...]
        gb_ref[...] = _dot(xb, sb_ref[...]).astype(BF16)
        cv_ref[...] = _dot(xb, sc_ref[...]) * _dot(xb, sv_ref[...])

    def wspec(g):
        return pl.BlockSpec((None, k, tn), lambda n, i: (0, 0, g * nb + n))

    return pl.pallas_call(
        kern, grid=(nb, m // tm),
        in_specs=[pl.BlockSpec((tm, k), lambda n, i: (i, 0)), wspec(0), wspec(1), wspec(2)],
        out_specs=[pl.BlockSpec((tm, tn), lambda n, i: (i, n))] * 2,
        out_shape=[SDS((m, dc), BF16), SDS((m, dc), F32)],
        scratch_shapes=[pltpu.VMEM((k, tn), BF16)] * 3,
        compiler_params=_cparams(2), name="in_proj1")(x, w3, w3, w3)


def _shortconv_prompt(gb, cv, cw, bsz, t):
    m, dc = cv.shape
    tc = _tile(dc, 512, LANE)
    nw = cw.shape[0]

    def kern(gb_ref, cv_ref, cw_ref, u_ref, buf_ref):
        x = cv_ref[...]
        row = lax.broadcasted_iota(I32, x.shape, 0)
        w = cw_ref[...]
        u = w[nw - 1:nw, :] * x
        for s in range(1, nw):
            u = u + w[nw - 1 - s:nw - s, :] * _shift_rows(x, s, row, 0.0)
        u_ref[...] = (gb_ref[...].astype(F32) * u).astype(BF16)
        buf_ref[...] = x[t - (nw - 1):t, :]

    blk = pl.BlockSpec((t, tc), lambda b_, c: (b_, c))
    return pl.pallas_call(
        kern, grid=(bsz, dc // tc),
        in_specs=[blk, blk, pl.BlockSpec((nw, tc), lambda b_, c: (0, c))],
        out_specs=[blk, pl.BlockSpec((None, nw - 1, tc), lambda b_, c: (b_, 0, c))],
        out_shape=[SDS((m, dc), BF16), SDS((bsz, nw - 1, dc), F32)],
        compiler_params=_cparams(2), name="shortconv_prompt")(gb, cv, cw)


def _shortconv_sample(gb, cv, u_all, st, cw, mp, ms):
    dc = cv.shape[1]
    tc = _tile(dc, 512, LANE)
    nw = cw.shape[0]
    rb = mp // ms

    def kern(gb_ref, cv_ref, u_in_ref, st_ref, cw_ref, u_ref, buf_ref):
        del u_in_ref
        x = cv_ref[...]
        w = cw_ref[...]
        u = w[nw - 1:nw, :] * x
        for s in range(nw - 1):
            u = u + w[s:s + 1, :] * st_ref[:, s, :]
        u_ref[...] = (gb_ref[...].astype(F32) * u).astype(BF16)
        for s in range(nw - 2):
            buf_ref[:, s, :] = st_ref[:, s + 1, :]
        buf_ref[:, nw - 2, :] = x

    blk = pl.BlockSpec((ms, tc), lambda c: (rb, c))
    stb = pl.BlockSpec((ms, nw - 1, tc), lambda c: (0, 0, c))
    return pl.pallas_call(
        kern, grid=(dc // tc,),
        in_specs=[blk, blk, pl.BlockSpec(memory_space=pl.ANY), stb,
                  pl.BlockSpec((nw, tc), lambda c: (0, c))],
        out_specs=[blk, stb],
        out_shape=[SDS(u_all.shape, BF16), SDS((ms, nw - 1, dc), F32)],
        input_output_aliases={2: 0},
        compiler_params=_cparams(1), name="shortconv_sample")(gb, cv, u_all, st, cw)


def _moe_rank(mh, tm):
    m, ne = mh.shape

    def kern(mh_ref, ex_ref, cnt_ref, carry_ref):
        @pl.when(pl.program_id(0) == 0)
        def _():
            carry_ref[...] = jnp.zeros_like(carry_ref)

        x = mh_ref[...]
        tri = (lax.broadcasted_iota(I32, (tm, tm), 0) > lax.broadcasted_iota(I32, (tm, tm), 1))
        ex = _dot(jnp.where(tri, 1.0, 0.0).astype(BF16), x.astype(BF16)) + carry_ref[...]
        ex_ref[...] = ex
        tot = ex[tm - 1:tm, :] + x[tm - 1:tm, :]
        carry_ref[...] = tot
        cnt_ref[...] = tot

    return pl.pallas_call(
        kern, grid=(m // tm,),
        in_specs=[pl.BlockSpec((tm, ne), lambda i: (i, 0))],
        out_specs=[pl.BlockSpec((tm, ne), lambda i: (i, 0)), pl.BlockSpec((1, ne), lambda i: (0, 0))],
        out_shape=[SDS((m, ne), F32), SDS((1, ne), F32)],
        scratch_shapes=[pltpu.VMEM((1, ne), F32)],
        compiler_params=_cparams(1), name="moe_rank")(mh)


def _moe_pos(mh, gd, ex, off, tm):
    m, ne = mh.shape

    def kern(mh_ref, gd_ref, ex_ref, off_ref, pos_ref, gate_ref):
        sel = mh_ref[...] > 0.5
        pd = ex_ref[...] + off_ref[...]
        big = jnp.float32(3e38)
        p_lo = jnp.min(jnp.where(sel, pd, big), axis=1, keepdims=True)
        p_hi = jnp.max(jnp.where(sel, pd, -big), axis=1, keepdims=True)
        gdv = gd_ref[...]
        g_lo = jnp.sum(jnp.where(sel & (pd == p_lo), gdv, 0.0), axis=1, keepdims=True)
        g_hi = jnp.sum(jnp.where(sel & (pd == p_hi), gdv, 0.0), axis=1, keepdims=True)
        pos_ref[:, 0:1] = p_lo.astype(I32)
        pos_ref[:, 1:2] = p_hi.astype(I32)
        gate_ref[:, 0:1] = g_lo
        gate_ref[:, 1:2] = g_hi

    blk = pl.BlockSpec((tm, ne), lambda i: (i, 0))
    two = pl.BlockSpec((tm, TOP_K), lambda i: (i, 0))
    return pl.pallas_call(
        kern, grid=(m // tm,),
        in_specs=[blk, blk, blk, pl.BlockSpec((1, ne), lambda i: (0, 0))],
        out_specs=[two, two],
        out_shape=[SDS((m, TOP_K), I32), SDS((m, TOP_K), F32)],
        compiler_params=_cparams(1), name="moe_pos")(mh, gd, ex, off)


def _moe_scatter(pos_flat, tail_row, has_tail, x, n_rows, tm, sub):
    m, w = x.shape
    ne = tail_row.shape[0]

    def kern(pos_ref, tail_ref, flag_ref, x_ref, xs_ref, zero_ref, sem, zsem):
        i = pl.program_id(0)

        @pl.when(i == 0)
        def _():
            zero_ref[...] = jnp.zeros_like(zero_ref)

            def zcopy(e):
                dst = xs_ref.at[pl.ds(pl.multiple_of(tail_ref[e], sub), sub)]
                return pltpu.make_async_copy(zero_ref, dst, zsem)

            for e in range(ne):
                @pl.when(flag_ref[e] > 0)
                def _():
                    zcopy(e).start()

            for e in range(ne):
                @pl.when(flag_ref[e] > 0)
                def _():
                    zcopy(e).wait()

        def row_copy(r, p):
            return pltpu.make_async_copy(x_ref.at[pl.ds(r, 1)], xs_ref.at[pl.ds(p, 1)], sem)

        def issue(r, c):
            t = i * tm + r
            row_copy(r, pos_ref[TOP_K * t]).start(priority=0)
            row_copy(r, pos_ref[TOP_K * t + 1]).start(priority=1)
            return c

        lax.fori_loop(0, tm, issue, 0, unroll=8)

        def drain(r, c):
            row_copy(0, 0).wait()
            row_copy(0, 0).wait()
            return c

        lax.fori_loop(0, tm, drain, 0, unroll=8)

    return pl.pallas_call(
        kern,
        grid_spec=pltpu.PrefetchScalarGridSpec(
            num_scalar_prefetch=3, grid=(m // tm,),
            in_specs=[pl.BlockSpec((tm, w), lambda i, *_: (i, 0))],
            out_specs=pl.BlockSpec(memory_space=pl.ANY),
            scratch_shapes=[pltpu.VMEM((sub, w), F32), pltpu.SemaphoreType.DMA(()),
                            pltpu.SemaphoreType.DMA(())]),
        out_shape=SDS((n_rows, w), F32),
        compiler_params=_cparams(1), name="moe_scatter")(pos_flat, tail_row, has_tail, x)


def _moe_ffn(st_e, st_row, st_nsub, xs, wg4, wu4, wd4, rs, sub, tf):
    n_rows, k = xs.shape
    f = wg4.shape[3]
    d = wd4.shape[3]
    ng = st_e.shape[0]
    nf = f // tf
    assert k == d and nf >= 2
    kc = _tile(k, 512, LANE)
    big, mid = 4 * sub, 2 * sub

    def kern(se_ref, sr_ref, sn_ref, xs_ref, wg_ref, wu_ref, wd_ref, ys_ref,
             x_ref, acc_ref, a_ref, wgb_ref, wub_ref, wdb_ref, sem_in, sem_out):
        g = pl.program_id(0)
        j = pl.program_id(1)
        nsub = sn_ref[g]
        row0 = sr_ref[g]
        nbig = lax.shift_right_logical(nsub, 2)
        has_mid = (nsub & 2) != 0
        has_small = (nsub & 1) != 0
        start_mid = nbig * big
        start_small = start_mid + jnp.where(has_mid, mid, 0)

        def rows(start, size):
            return pl.ds(pl.multiple_of(start, sub), size)

        def in_copy(s):
            return pltpu.make_async_copy(xs_ref.at[rows(row0 + s * sub, sub)],
                                         acc_ref.at[rows(s * sub, sub)], sem_in)

        def out_copy(start, size):
            return pltpu.make_async_copy(acc_ref.at[rows(start, size)],
                                         ys_ref.at[rows(row0 + start, size)], sem_out)

        def each_sub(fn):
            def body(s, c):
                fn(s)
                return c
            lax.fori_loop(0, nsub, body, 0)

        def cast_weights():
            for c in range(k // kc):
                ks = slice(c * kc, (c + 1) * kc)
                wgb_ref[ks, :] = wg_ref[ks, :].astype(BF16)
                wub_ref[ks, :] = wu_ref[ks, :].astype(BF16)
            wdb_ref[...] = wd_ref[...].astype(BF16)

        def up(start, size, slot):
            xb = x_ref[rows(start, size), :]
            gg = _dot(xb, wgb_ref[...])
            uu = _dot(xb, wub_ref[...])
            a_ref[slot, 0:size, :] = (gg * _sigmoid(gg) * uu).astype(BF16)

        def down(start, size, slot, first, last):
            dd = _dot(a_ref[slot, 0:size, :], wdb_ref[...])
            if first:
                acc_ref[rows(start, size), :] = dd
            else:
                acc_ref[rows(start, size), :] += dd
            if last:
                out_copy(start, size).start()

        def compute(first, last, cast):
            @pl.when(nbig > 0)
            def _():
                if cast:
                    cast_weights()
                up(0, big, 0)

                def body(s, c):
                    up(s * big, big, s & 1)
                    down((s - 1) * big, big, (s - 1) & 1, first, last)
                    return c

                lax.fori_loop(1, nbig, body, 0)
                down((nbig - 1) * big, big, (nbig - 1) & 1, first, last)

            if cast:
                @pl.when(nbig == 0)
                def _():
                    cast_weights()

            @pl.when(has_mid)
            def _():
                up(start_mid, mid, 0)
                down(start_mid, mid, 0, first, last)

            @pl.when(has_small)
            def _():
                up(start_small, sub, 0)
                down(start_small, sub, 0, first, last)

        @pl.when(nsub > 0)
        def _():
            @pl.when(j == 0)
            def _():
                each_sub(lambda s: in_copy(s).start())
                cast_weights()

                def to_bf16(s):
                    x_ref[rows(s * sub, sub), :] = acc_ref[rows(s * sub, sub), :].astype(BF16)

                each_sub(lambda s: in_copy(s).wait())
                each_sub(to_bf16)
                compute(True, False, False)

            @pl.when((j > 0) & (j < nf - 1))
            def _():
                compute(False, False, True)

            @pl.when(j == nf - 1)
            def _():
                compute(False, True, True)

                def wait_big(s, c):
                    out_copy(0, big).wait()
                    return c

                lax.fori_loop(0, nbig, wait_big, 0)

                @pl.when(has_mid)
                def _():
                    out_copy(0, mid).wait()

                @pl.when(has_small)
                def _():
                    out_copy(0, sub).wait()

    def widx(g, j, se, sr, sn):
        return (0, se[g], 0, jnp.where(sn[g] > 0, j, nf - 1))

    def didx(g, j, se, sr, sn):
        return (0, se[g], jnp.where(sn[g] > 0, j, nf - 1), 0)

    return pl.pallas_call(
        kern,
        grid_spec=pltpu.PrefetchScalarGridSpec(
            num_scalar_prefetch=3, grid=(ng, nf),
            in_specs=[pl.BlockSpec(memory_space=pl.ANY),
                      pl.BlockSpec((None, None, k, tf), widx),
                      pl.BlockSpec((None, None, k, tf), widx),
                      pl.BlockSpec((None, None, tf, d), didx)],
            out_specs=pl.BlockSpec(memory_space=pl.ANY),
            scratch_shapes=[pltpu.VMEM((rs, k), BF16), pltpu.VMEM((rs, d), F32),
                            pltpu.VMEM((2, 4 * sub, tf), BF16),
                            pltpu.VMEM((k, tf), BF16), pltpu.VMEM((k, tf), BF16),
                            pltpu.VMEM((tf, d), BF16),
                            pltpu.SemaphoreType.DMA(()), pltpu.SemaphoreType.DMA(())]),
        out_shape=SDS((n_rows, d), F32),
        compiler_params=_cparams(2), name="moe_ffn")(st_e, st_row, st_nsub, xs, wg4, wu4, wd4)


def _moe_combine(pos_flat, h, gates, ys, g, mp, ms, tp):
    m, d = h.shape
    n_p = mp // tp
    assert mp % tp == 0 and ms <= tp and ms % 8 == 0

    def kern(pos_ref, h_ref, gate_ref, ys_ref, g_ref, yp_ref, ysm_ref, a_ref, b_ref, sem):
        i = pl.program_id(0)

        def copies(tile, r):
            t = tile * tp + r
            slot = tile & 1
            return (pltpu.make_async_copy(ys_ref.at[pl.ds(pos_ref[TOP_K * t], 1)],
                                          a_ref.at[slot, pl.ds(r, 1)], sem.at[slot]),
                    pltpu.make_async_copy(ys_ref.at[pl.ds(pos_ref[TOP_K * t + 1], 1)],
                                          b_ref.at[slot, pl.ds(r, 1)], sem.at[slot]))

        def gather(tile):
            def issue(r, c):
                ca, cb = copies(tile, r)
                ca.start(priority=0)
                cb.start(priority=1)
                return c

            @pl.when(tile < n_p)
            def _():
                lax.fori_loop(0, tp, issue, 0, unroll=8)

            @pl.when(tile == n_p)
            def _():
                lax.fori_loop(0, ms, issue, 0, unroll=8)

        def combined(n):
            def drain(r, c):
                ca, cb = copies(i, 0)
                ca.wait()
                cb.wait()
                return c

            lax.fori_loop(0, n, drain, 0, unroll=8)
            slot = i & 1
            gt = gate_ref[0:n, :]
            hh = (h_ref[0:n, :] + gt[:, 0:1] * a_ref[slot, 0:n, :]
                  + gt[:, 1:2] * b_ref[slot, 0:n, :])
            return _rms(hh, g_ref[...])

        @pl.when(i == 0)
        def _():
            gather(i)

        gather(i + 1)

        @pl.when(i < n_p)
        def _():
            yp_ref[...] = combined(tp)

        @pl.when(i == n_p)
        def _():
            ysm_ref[...] = combined(ms)

    return pl.pallas_call(
        kern,
        grid_spec=pltpu.PrefetchScalarGridSpec(
            num_scalar_prefetch=1, grid=(n_p + 1,),
            in_specs=[pl.BlockSpec((tp, d), lambda i, *_: (i, 0)),
                      pl.BlockSpec((tp, TOP_K), lambda i, *_: (i, 0)),
                      pl.BlockSpec(memory_space=pl.ANY),
                      pl.BlockSpec((1, d), lambda i, *_: (0, 0))],
            out_specs=[pl.BlockSpec((tp, d), lambda i, *_: (jnp.minimum(i, n_p - 1), 0)),
                       pl.BlockSpec((ms, d), lambda i, *_: (0, 0))],
            scratch_shapes=[pltpu.VMEM((2, tp, d), F32), pltpu.VMEM((2, tp, d), F32),
                            pltpu.SemaphoreType.DMA((2,))]),
        out_shape=[SDS((mp, d), F32), SDS((ms, d), F32)],
        compiler_params=_cparams(1), name="moe_combine")(pos_flat, h, gates, ys, g)


def _moe_tables(cnt, sub, rs, ng):
    ne = cnt.shape[0]
    nsub_e = (cnt + sub - 1) // sub
    size_e = nsub_e * sub
    off = jnp.cumsum(size_e) - size_e
    spr = rs // sub
    nst_e = (nsub_e + spr - 1) // spr
    st_start = jnp.cumsum(nst_e) - nst_e
    n_act = jnp.sum(nst_e)
    gidx = jnp.arange(ng, dtype=I32)
    e_of = jnp.sum((gidx[:, None] >= st_start[None, :]).astype(I32), axis=1) - 1
    e_of = jnp.clip(e_of, 0, ne - 1)
    kth = gidx - st_start[e_of]
    active = gidx < n_act
    nsub = jnp.where(active, jnp.clip(nsub_e[e_of] - kth * spr, 0, spr), 0)
    row = off[e_of] + kth * rs
    last = jnp.maximum(n_act - 1, 0)
    st_e = jnp.where(active, e_of, e_of[last]).astype(I32)
    st_row = jnp.where(active, row, 0).astype(I32)
    tail_row = jnp.where(size_e > 0, off + size_e - sub, 0).astype(I32)
    has_tail = (size_e > 0).astype(I32)
    return off, st_e, st_row, nsub.astype(I32), tail_row, has_tail


def kernel(x_prompt, x_sample, state_rg_conv, state_rg_h, state_gla, state_sc_conv, norm_mix_e, w_in_e, rg_conv_w, rg_conv_b, rg_w_a, rg_b_a, rg_w_x, rg_b_x, rg_lambda, gla_w_gate, gla_b_gate, gla_norm, w_out_e, norm_ffn_e, ffn_w_gate, ffn_w_up, ffn_w_down, norm_mix_o, w_in_o, sc_conv_w, w_out_o, norm_ffn_o, router_w, moe_w_gate, moe_w_up, moe_w_down, final_norm):
    bsz, t, d = x_prompt.shape
    ms = x_sample.shape[0]
    assert x_sample.shape[1] == 1 and w_in_e.shape[0] == 1 and w_in_o.shape[0] == 1
    mp = bsz * t
    m = mp + ms
    d_rnn = rg_lambda.shape[1]
    nh, dk, dv = state_gla.shape[2:]
    hk, hv = nh * dk, nh * dv
    n_main = 2 * d_rnn + 2 * hk + 2 * hv
    d_mix = d_rnn + hv
    ne = router_w.shape[2]
    xp = x_prompt.reshape(mp, d)
    xs = x_sample.reshape(ms, d)
    row = lambda v: v.reshape(1, -1)

    tp = _tile(mp, TP_TARGET, 16)
    tm = _tile(m, TM_TARGET, 16)
    tm_in = _tile(m, TM_IN_TARGET, 16)
    tm_out = _tile(m, TM_OUT_TARGET, 16)

    hn0 = _norm_in(xp, xs, norm_mix_e, tp)
    w_in_t = jnp.swapaxes(w_in_e, 1, 2)
    z0 = _in_proj0(hn0, w_in_t, n_main, tm_in, _tile(n_main, 1024, LANE))
    la = _gla_gate(hn0, w_in_t, n_main, gla_w_gate[0], gla_b_gate, tm)
    y_mix, rgc_p, rgh_p = _rglru_prompt(z0, rg_conv_w[0], rg_conv_b, rg_w_a[0], rg_b_a, rg_w_x[0],
                                        rg_b_x, rg_lambda, bsz, t, d_rnn, d_mix)
    y_mix, rgc_s, rgh_s = _rglru_sample(z0, y_mix, state_rg_conv[0], state_rg_h[0], rg_conv_w[0],
                                        rg_conv_b, rg_w_a[0], rg_b_a, rg_w_x[0], rg_b_x, rg_lambda,
                                        mp, ms, d_rnn)
    y_mix, gla_p = _gla_prompt(z0, la, y_mix, gla_norm, bsz, t, nh, dk, dv, 2 * d_rnn, d_rnn)
    y_mix, gla_s = _gla_sample(z0, la, y_mix, state_gla, gla_norm, mp, ms, nh, dk, dv,
                               2 * d_rnn, d_rnn)
    h1, hn1 = _out_proj(y_mix, _cast_bf16(w_out_e), xp, norm_ffn_e, tm_out, res_tail=xs)

    f0 = _ffn(hn1, ffn_w_gate, ffn_w_up, ffn_w_down, tm, _tile(ffn_w_gate.shape[2], 512, LANE))
    h2, hn2 = _add_norm(h1, f0, norm_mix_o, tm)

    gb, cv = _in_proj1(hn2, w_in_o, tm_in, _tile(w_in_o.shape[2] // 3, 256, LANE))
    u, sc_p = _shortconv_prompt(gb, cv, sc_conv_w[0], bsz, t)
    u, sc_s = _shortconv_sample(gb, cv, u, state_sc_conv[0], sc_conv_w[0], mp, ms)
    h3, hn3, mh, gd = _out_proj(u, _cast_bf16(w_out_o), h2, norm_ffn_o, tm_out,
                                router_w=router_w[0])

    sub = MOE_SUB
    rs = MOE_SPR * sub
    ng = (TOP_K * m) // rs + ne
    n_rows = TOP_K * m + ne * (sub - 1)
    n_rows = ((n_rows + sub - 1) // sub) * sub
    ex, cnt = _moe_rank(mh, tm)
    off, st_e, st_row, st_nsub, tail_row, has_tail = _moe_tables(cnt[0].astype(I32), sub, rs, ng)
    pos, gates = _moe_pos(mh, gd, ex, off.astype(F32).reshape(1, ne), tm)
    pos_flat = pos.reshape(TOP_K * m)
    xsort = _moe_scatter(pos_flat, tail_row, has_tail, hn3, n_rows, tm, sub)
    ys = _moe_ffn(st_e, st_row, st_nsub, xsort, moe_w_gate, moe_w_up, moe_w_down, rs, sub,
                  _tile(moe_w_gate.shape[3], 256, LANE))
    y_p, y_s = _moe_combine(pos_flat, h3, gates, ys, row(final_norm), mp, ms, tp)

    return (y_p.reshape(bsz, t, d), y_s.reshape(ms, 1, d),
            rgc_p[None], rgc_s[None], rgh_p.reshape(1, bsz, d_rnn), rgh_s[None],
            gla_p[None], gla_s, sc_p[None], sc_s[None])
```

```python
import jax
import jax.numpy as jnp
from jax import lax
from jax.experimental import pallas as pl
from jax.experimental.pallas import tpu as pltpu

F32 = jnp.float32
BF16 = jnp.bfloat16
I32 = jnp.int32
SDS = jax.ShapeDtypeStruct

EPS = 1e-6
RG_C = 8.0
GLA_TAU = 16.0
GLA_CHUNK = 64
TOP_K = 2
LANE = 128
SUB8 = 8
VMEM_LIMIT = 56 * 1024 * 1024
ARB = "arbitrary"
TM_TARGET = 640
TM_IN_TARGET = 1664
TM_OUT_TARGET = 320
TP_TARGET = 512
GLA_TB_TARGET = 512
MOE_SUB = 128
MOE_SPR = 20


def _cparams(n_axes, vmem=VMEM_LIMIT):
    return pltpu.CompilerParams(dimension_semantics=(ARB,) * n_axes, vmem_limit_bytes=vmem)


def _tile(n, target, align):
    best = None
    for t in range(align, min(n, target) + 1, align):
        if n % t == 0:
            best = t
    assert best is not None, (n, target, align)
    return best


def _rms(xf, g):
    ms = jnp.mean(xf * xf, axis=-1, keepdims=True)
    return xf * lax.rsqrt(ms + EPS) * g


def _sigmoid(x):
    return 1.0 / (1.0 + jnp.exp(-x))


def _softplus(x):
    return jnp.maximum(x, 0.0) + jnp.log1p(jnp.exp(-jnp.abs(x)))


def _gelu_tanh(x):
    c = 0.7978845608028654
    return x * (0.5 * (1.0 + jnp.tanh(c * (x + 0.044715 * (x * x * x)))))


def _dot(a, b):
    return jnp.dot(a, b, preferred_element_type=F32)


def _shift_rows(x, s, row, fill):
    return jnp.where(row >= s, pltpu.roll(x, s, 0), fill)


def _norm_in(xp, xs, g, tp):
    mp, d = xp.shape
    ms = xs.shape[0]
    n_p = mp // tp
    assert mp % tp == 0 and ms <= tp and ms % 16 == 0

    def kern(xp_ref, xs_ref, g_ref, o_ref):
        i = pl.program_id(0)

        @pl.when(i < n_p)
        def _():
            o_ref[...] = _rms(xp_ref[...], g_ref[...]).astype(BF16)

        @pl.when(i == n_p)
        def _():
            o_ref[0:ms, :] = _rms(xs_ref[...], g_ref[...]).astype(BF16)

    return pl.pallas_call(
        kern, grid=(n_p + 1,),
        in_specs=[pl.BlockSpec((tp, d), lambda i: (jnp.minimum(i, n_p - 1), 0)),
                  pl.BlockSpec((ms, d), lambda i: (0, 0)),
                  pl.BlockSpec((1, d), lambda i: (0, 0))],
        out_specs=pl.BlockSpec((tp, d), lambda i: (i, 0)),
        out_shape=SDS((mp + ms, d), BF16),
        compiler_params=_cparams(1), name="norm_in")(xp, xs, g)


def _in_proj0(x, wt3, n_cols, tm, tn):
    m, k = x.shape

    def kern(x_ref, w_ref, o_ref, wb_ref):
        @pl.when(pl.program_id(1) == 0)
        def _():
            wb_ref[...] = w_ref[...].T.astype(BF16)

        o_ref[...] = _dot(x_ref[...], wb_ref[...])

    return pl.pallas_call(
        kern, grid=(n_cols // tn, m // tm),
        in_specs=[pl.BlockSpec((tm, k), lambda n, i: (i, 0)),
                  pl.BlockSpec((None, tn, k), lambda n, i: (0, n, 0))],
        out_specs=pl.BlockSpec((tm, tn), lambda n, i: (i, n)),
        out_shape=SDS((m, n_cols), F32),
        scratch_shapes=[pltpu.VMEM((k, tn), BF16)],
        compiler_params=_cparams(2), name="in_proj0")(x, wt3)


def _gla_gate(x, wt3, col0, w_gate, b_gate, tm):
    m, k = x.shape
    r, n = w_gate.shape
    assert col0 % r == 0

    def kern(x_ref, wl_ref, wg_ref, bg_ref, o_ref):
        lr = lax.dot_general(x_ref[...], wl_ref[...].astype(BF16), (((1,), (1,)), ((), ())),
                             preferred_element_type=F32)
        pre = _dot(lr.astype(BF16), wg_ref[...].astype(BF16)) + bg_ref[...]
        o_ref[...] = -_softplus(-pre) * (1.0 / GLA_TAU)

    return pl.pallas_call(
        kern, grid=(m // tm,),
        in_specs=[pl.BlockSpec((tm, k), lambda i: (i, 0)),
                  pl.BlockSpec((None, r, k), lambda i: (0, col0 // r, 0)),
                  pl.BlockSpec((r, n), lambda i: (0, 0)),
                  pl.BlockSpec((1, n), lambda i: (0, 0))],
        out_specs=pl.BlockSpec((tm, n), lambda i: (i, 0)),
        out_shape=SDS((m, n), F32),
        compiler_params=_cparams(1), name="gla_gate")(x, wt3, w_gate, b_gate)


def _rg_gates(xc, wa_ref, wx_ref, ba_ref, bx_ref, lam_ref):
    w2 = jnp.concatenate([wa_ref[...], wx_ref[...]], axis=1).astype(BF16)
    pre = _dot(xc.astype(BF16), w2)
    r = _sigmoid(pre[:, :LANE] + ba_ref[...])
    i = _sigmoid(pre[:, LANE:] + bx_ref[...])
    log_a = (-RG_C) * r * _softplus(-lam_ref[...])
    a = jnp.exp(log_a)
    mult = jnp.sqrt(jnp.tanh(-log_a) * (1.0 + a * a))
    return a, mult, i


def _rglru_prompt(z, cw, cb, wa, ba, wx, bx, lam, bsz, t, d_rnn, d_out):
    nh = d_rnn // LANE
    m = z.shape[0]

    def kern(xr_ref, gr_ref, cw_ref, cb_ref, wa_ref, ba_ref, wx_ref, bx_ref, lam_ref,
             y_ref, conv_ref, h_ref, a_s, b_s, c_s):
        xr = xr_ref[...]
        row = lax.broadcasted_iota(I32, (t, LANE), 0)
        w = cw_ref[...]
        nw = w.shape[0]
        xc = w[nw - 1:nw, :] * xr
        for s in range(1, nw):
            xc = xc + w[nw - 1 - s:nw - s, :] * _shift_rows(xr, s, row, 0.0)
        xc = xc + cb_ref[...]
        a, mult, i = _rg_gates(xc, wa_ref, wx_ref, ba_ref, bx_ref, lam_ref)
        mult = jnp.where(row == 0, 1.0, mult)
        b = mult * i * xc
        ng = t // SUB8
        a3, b3 = a.reshape(ng, SUB8, LANE), b.reshape(ng, SUB8, LANE)
        r8 = lax.broadcasted_iota(I32, (ng, SUB8, LANE), 1)
        s = 1
        while s < SUB8:
            b3 = a3 * jnp.where(r8 >= s, pltpu.roll(b3, s, 1), 0.0) + b3
            a3 = a3 * jnp.where(r8 >= s, pltpu.roll(a3, s, 1), 1.0)
            s *= 2
        a_s[...] = a3.reshape(t, LANE)
        b_s[...] = b3.reshape(t, LANE)
        ag = a_s[pl.ds(SUB8 - 1, ng, stride=SUB8), :]
        bg = b_s[pl.ds(SUB8 - 1, ng, stride=SUB8), :]
        rowg = lax.broadcasted_iota(I32, (ng, LANE), 0)
        s = 1
        while s < ng:
            bg = ag * _shift_rows(bg, s, rowg, 0.0) + bg
            ag = ag * _shift_rows(ag, s, rowg, 1.0)
            s *= 2
        carry = _shift_rows(bg, 1, rowg, 0.0)
        for k in range(SUB8):
            c_s[pl.ds(k, ng, stride=SUB8), :] = carry
        h = b_s[...] + a_s[...] * c_s[...]
        y_ref[...] = (h * _gelu_tanh(gr_ref[...])).astype(BF16)
        conv_ref[...] = xr[t - (nw - 1):t, :]
        h_ref[...] = h[t - 1:t, :]

    vec = pl.BlockSpec((1, LANE), lambda b_, h: (0, h))
    blk = pl.BlockSpec((None, LANE, LANE), lambda b_, h: (h, 0, 0))
    nw = cw.shape[0]
    return pl.pallas_call(
        kern, grid=(bsz, nh),
        in_specs=[pl.BlockSpec((t, LANE), lambda b_, h: (b_, h)),
                  pl.BlockSpec((t, LANE), lambda b_, h: (b_, nh + h)),
                  pl.BlockSpec((nw, LANE), lambda b_, h: (0, h)),
                  vec, blk, vec, blk, vec, vec],
        out_specs=[pl.BlockSpec((t, LANE), lambda b_, h: (b_, h)),
                   pl.BlockSpec((None, nw - 1, LANE), lambda b_, h: (b_, 0, h)),
                   pl.BlockSpec((None, 1, LANE), lambda b_, h: (b_, 0, h))],
        out_shape=[SDS((m, d_out), BF16), SDS((bsz, nw - 1, d_rnn), F32), SDS((bsz, 1, d_rnn), F32)],
        scratch_shapes=[pltpu.VMEM((t, LANE), F32)] * 3,
        compiler_params=_cparams(2), name="rglru_prompt")(z, z, cw, cb, wa, ba, wx, bx, lam)


def _rglru_sample(z, y_mix, st_conv, st_h, cw, cb, wa, ba, wx, bx, lam, mp, ms, d_rnn):
    nh = d_rnn // LANE
    nw = cw.shape[0]
    rb = mp // ms
    assert mp % ms == 0

    def kern(xr_ref, gr_ref, y_in_ref, sc_ref, sh_ref, cw_ref, cb_ref, wa_ref, ba_ref, wx_ref,
             bx_ref, lam_ref, y_ref, conv_ref, h_ref):
        del y_in_ref
        xr = xr_ref[...]
        w = cw_ref[...]
        xc = w[nw - 1:nw, :] * xr
        for s in range(nw - 1):
            xc = xc + w[s:s + 1, :] * sc_ref[:, s, :]
        xc = xc + cb_ref[...]
        a, mult, i = _rg_gates(xc, wa_ref, wx_ref, ba_ref, bx_ref, lam_ref)
        h = a * sh_ref[...] + mult * i * xc
        y_ref[...] = (h * _gelu_tanh(gr_ref[...])).astype(BF16)
        for s in range(nw - 2):
            conv_ref[:, s, :] = sc_ref[:, s + 1, :]
        conv_ref[:, nw - 2, :] = xr
        h_ref[...] = h

    vec = pl.BlockSpec((1, LANE), lambda h: (0, h))
    blk = pl.BlockSpec((None, LANE, LANE), lambda h: (h, 0, 0))
    return pl.pallas_call(
        kern, grid=(nh,),
        in_specs=[pl.BlockSpec((ms, LANE), lambda h: (rb, h)),
                  pl.BlockSpec((ms, LANE), lambda h: (rb, nh + h)),
                  pl.BlockSpec(memory_space=pl.ANY),
                  pl.BlockSpec((ms, nw - 1, LANE), lambda h: (0, 0, h)),
                  pl.BlockSpec((ms, LANE), lambda h: (0, h)),
                  pl.BlockSpec((nw, LANE), lambda h: (0, h)),
                  vec, blk, vec, blk, vec, vec],
        out_specs=[pl.BlockSpec((ms, LANE), lambda h: (rb, h)),
                   pl.BlockSpec((ms, nw - 1, LANE), lambda h: (0, 0, h)),
                   pl.BlockSpec((ms, LANE), lambda h: (0, h))],
        out_shape=[SDS(y_mix.shape, BF16), SDS((ms, nw - 1, d_rnn), F32), SDS((ms, d_rnn), F32)],
        input_output_aliases={2: 0},
        compiler_params=_cparams(1), name="rglru_sample")(
            z, z, y_mix, st_conv, st_h, cw, cb, wa, ba, wx, bx, lam)


def _gla_out(o, gn, g):
    return _rms(o, gn) * (g * _sigmoid(g))


def _gla_prompt(z, la, y_mix, gn, bsz, t, nh, dk, dv, col_q, col_y):
    tb = _tile(t, GLA_TB_TARGET, GLA_CHUNK)
    nt = t // tb
    nc = tb // GLA_CHUNK
    c = GLA_CHUNK
    hk, hv = nh * dk, nh * dv
    assert col_q % hk == 0 and (col_q + 2 * hk) % hv == 0 and col_y % hv == 0
    cq, ck = col_q // hk, col_q // hk + 1
    cv, cg = (col_q + 2 * hk) // hv, (col_q + 2 * hk) // hv + 1
    scale = dk ** -0.5

    def kern(q_ref, k_ref, v_ref, g_ref, la_ref, gn_ref, y_in_ref, y_ref, s_ref, st_ref):
        del y_in_ref
        tbi = pl.program_id(1)

        @pl.when(tbi == 0)
        def _():
            st_ref[...] = jnp.zeros_like(st_ref)

        row = lax.broadcasted_iota(I32, (c, dk), 0)
        causal = (lax.broadcasted_iota(I32, (c, c), 0) >= lax.broadcasted_iota(I32, (c, c), 1))

        def chunk(ci, carry):
            rows = pl.ds(pl.multiple_of(ci * c, c), c)
            for hd in range(nh):
                ks = slice(hd * dk, (hd + 1) * dk)
                vs = slice(hd * dv, (hd + 1) * dv)
                q = q_ref[rows, ks] * scale
                k = k_ref[rows, ks]
                v = v_ref[rows, vs].astype(BF16)
                bc = la_ref[rows, ks]
                s = 1
                while s < c:
                    bc = bc + _shift_rows(bc, s, row, 0.0)
                    s *= 2
                b_last = bc[c - 1:c, :]
                b_mid = bc[c // 2 - 1:c // 2, :]
                qe = (q * jnp.exp(bc)).astype(BF16)
                qm = (q * jnp.exp(bc - b_mid)).astype(BF16)
                km = (k * jnp.exp(b_mid - bc)).astype(BF16)
                kd = (k * jnp.exp(b_last - bc)).astype(BF16)
                st = st_ref[hd]
                o = lax.dot_general(qe, st.astype(BF16), (((1,), (1,)), ((), ())),
                                    preferred_element_type=F32)
                attn = lax.dot_general(qm, km, (((1,), (1,)), ((), ())),
                                       preferred_element_type=F32)
                attn = jnp.where(causal, attn, 0.0).astype(BF16)
                o = o + _dot(attn, v)
                st_ref[hd] = st * jnp.exp(b_last) + lax.dot_general(
                    v, kd, (((0,), (0,)), ((), ())), preferred_element_type=F32)
                y_ref[rows, vs] = _gla_out(o, gn_ref[...], g_ref[rows, vs]).astype(BF16)
            return carry

        lax.fori_loop(0, nc, chunk, 0)

        @pl.when(tbi == nt - 1)
        def _():
            for hd in range(nh):
                s_ref[hd] = st_ref[hd].T

    m = z.shape[0]
    return pl.pallas_call(
        kern, grid=(bsz, nt),
        in_specs=[pl.BlockSpec((tb, hk), lambda b_, i: (b_ * nt + i, cq)),
                  pl.BlockSpec((tb, hk), lambda b_, i: (b_ * nt + i, ck)),
                  pl.BlockSpec((tb, hv), lambda b_, i: (b_ * nt + i, cv)),
                  pl.BlockSpec((tb, hv), lambda b_, i: (b_ * nt + i, cg)),
                  pl.BlockSpec((tb, hk), lambda b_, i: (b_ * nt + i, 0)),
                  pl.BlockSpec((1, dv), lambda b_, i: (0, 0)),
                  pl.BlockSpec(memory_space=pl.ANY)],
        out_specs=[pl.BlockSpec((tb, hv), lambda b_, i: (b_ * nt + i, col_y // hv)),
                   pl.BlockSpec((None, nh, dk, dv), lambda b_, i: (b_, 0, 0, 0))],
        out_shape=[SDS(y_mix.shape, BF16), SDS((bsz, nh, dk, dv), F32)],
        scratch_shapes=[pltpu.VMEM((nh, dv, dk), F32)],
        input_output_aliases={6: 0},
        compiler_params=_cparams(2), name="gla_prompt")(z, z, z, z, la, gn, y_mix)


def _gla_sample(z, la, y_mix, st, gn, mp, ms, nh, dk, dv, col_q, col_y):
    bb = 16
    assert ms % bb == 0 and mp % ms == 0 and dk == LANE
    ns = ms // bb
    hk, hv = nh * dk, nh * dv
    cq, ck = col_q // hk, col_q // hk + 1
    cv, cg = (col_q + 2 * hk) // hv, (col_q + 2 * hk) // hv + 1
    scale = dk ** -0.5

    def kern(q_ref, k_ref, la_ref, v_ref, g_ref, gn_ref, st_ref, y_in_ref, y_ref, so_ref,
             qt_ref, kt_ref, at_ref):
        del y_in_ref
        i = pl.program_id(0)

        @pl.when(i == 0)
        def _():
            for hd in range(nh):
                ks = slice(hd * dk, (hd + 1) * dk)
                qt = (q_ref[:, ks] * scale).T
                kt = k_ref[:, ks].T
                at = jnp.exp(la_ref[:, ks]).T
                for s in range(ns):
                    qt_ref[s, hd] = qt[:, s * bb:(s + 1) * bb]
                    kt_ref[s, hd] = kt[:, s * bb:(s + 1) * bb]
                    at_ref[s, hd] = at[:, s * bb:(s + 1) * bb]

        for hd in range(nh):
            vs = slice(hd * dv, (hd + 1) * dv)
            qt = qt_ref[i, hd]
            kt = kt_ref[i, hd]
            at = at_ref[i, hd]
            outs = []
            for j in range(bb):
                v = v_ref[j:j + 1, vs]
                s_new = at[:, j:j + 1] * st_ref[j, hd] + kt[:, j:j + 1] * v
                so_ref[j, hd] = s_new
                outs.append(jnp.sum(qt[:, j:j + 1] * s_new, axis=0, keepdims=True))
            o = jnp.concatenate(outs, axis=0)
            y_ref[:, vs] = _gla_out(o, gn_ref[...], g_ref[:, vs]).astype(BF16)

    rb = mp // ms
    rbb = mp // bb
    return pl.pallas_call(
        kern, grid=(ns,),
        in_specs=[pl.BlockSpec((ms, hk), lambda i: (rb, cq)),
                  pl.BlockSpec((ms, hk), lambda i: (rb, ck)),
                  pl.BlockSpec((ms, hk), lambda i: (rb, 0)),
                  pl.BlockSpec((bb, hv), lambda i: (rbb + i, cv)),
                  pl.BlockSpec((bb, hv), lambda i: (rbb + i, cg)),
                  pl.BlockSpec((1, dv), lambda i: (0, 0)),
                  pl.BlockSpec((None, bb, nh, dk, dv), lambda i: (0, i, 0, 0, 0)),
                  pl.BlockSpec(memory_space=pl.ANY)],
        out_specs=[pl.BlockSpec((bb, hv), lambda i: (rbb + i, col_y // hv)),
                   pl.BlockSpec((None, bb, nh, dk, dv), lambda i: (0, i, 0, 0, 0))],
        out_shape=[SDS(y_mix.shape, BF16), SDS((1, ms, nh, dk, dv), F32)],
        scratch_shapes=[pltpu.VMEM((ns, nh, dk, bb), F32)] * 3,
        input_output_aliases={7: 0},
        compiler_params=_cparams(1), name="gla_sample")(z, z, la, z, z, gn, st, y_mix)


def _cast_bf16(w3):
    _, k, n = w3.shape
    tk = _tile(k, 512, 16)

    def kern(w_ref, o_ref):
        o_ref[...] = w_ref[...].astype(BF16)

    return pl.pallas_call(
        kern, grid=(k // tk,),
        in_specs=[pl.BlockSpec((None, tk, n), lambda i: (0, i, 0))],
        out_specs=pl.BlockSpec((tk, n), lambda i: (i, 0)),
        out_shape=SDS((k, n), BF16),
        compiler_params=_cparams(1), name="cast_bf16")(w3)


def _route(hn, rw):
    logits = _dot(hn.astype(BF16), rw.astype(BF16))
    ne = float(logits.shape[1])
    lane = lax.broadcasted_iota(I32, logits.shape, 1).astype(F32)
    m1 = jnp.max(logits, axis=1, keepdims=True)
    i1 = jnp.min(jnp.where(logits == m1, lane, ne), axis=1, keepdims=True)
    sel1 = lane == i1
    rest = jnp.where(sel1, -jnp.inf, logits)
    m2 = jnp.max(rest, axis=1, keepdims=True)
    i2 = jnp.min(jnp.where(rest == m2, lane, ne), axis=1, keepdims=True)
    sel2 = lane == i2
    e2 = jnp.exp(m2 - m1)
    g1 = 1.0 / (1.0 + e2)
    g2 = e2 / (1.0 + e2)
    mh = jnp.where(sel1 | sel2, 1.0, 0.0)
    gd = jnp.where(sel1, g1, 0.0) + jnp.where(sel2, g2, 0.0)
    return mh, gd


def _out_proj(y, wb, res, g, tm, res_tail=None, router_w=None):
    m, k = y.shape
    d = wb.shape[1]
    n_m = m // tm
    split = res_tail is not None
    if split:
        mp, ms = res.shape[0], res_tail.shape[0]
        assert mp + ms == m and ms <= tm
        head = tm - ms
        n_rb = pl.cdiv(mp, tm)
    route = router_w is not None

    def kern(*refs):
        it = iter(refs)
        y_ref, w_ref, r_ref = next(it), next(it), next(it)
        rt_ref = next(it) if split else None
        g_ref = next(it)
        rw_ref = next(it) if route else None
        h_ref, n_ref = next(it), next(it)
        mh_ref, gd_ref = (next(it), next(it)) if route else (None, None)
        acc_ref = next(it)
        i = pl.program_id(0)

        def matmul():
            acc_ref[i & 1] = _dot(y_ref[...], w_ref[...])

        def epilogue():
            acc = acc_ref[(i - 1) & 1]
            if split:
                last = i == n_m
                tail = acc[head:] + jnp.where(last, rt_ref[...], r_ref[head:tm, :])
                h = tail if head == 0 else jnp.concatenate([acc[:head] + r_ref[0:head, :], tail], 0)
            else:
                h = acc + r_ref[...]
            h_ref[...] = h
            hn = _rms(h, g_ref[...])
            if route:
                n_ref[...] = hn
                mh, gd = _route(hn, rw_ref[...])
                mh_ref[...] = mh
                gd_ref[...] = gd
            else:
                n_ref[...] = hn.astype(BF16)

        @pl.when(i < n_m)
        def _():
            matmul()

        @pl.when(i > 0)
        def _():
            epilogue()

    def lag(i):
        return jnp.maximum(i - 1, 0)

    in_specs = [pl.BlockSpec((tm, k), lambda i: (jnp.minimum(i, n_m - 1), 0)),
                pl.BlockSpec((k, d), lambda i: (0, 0))]
    args = [y, wb]
    if split:
        in_specs += [pl.BlockSpec((tm, d), lambda i: (jnp.minimum(lag(i), n_rb - 1), 0)),
                     pl.BlockSpec((ms, d), lambda i: (0, 0))]
        args += [res, res_tail]
    else:
        in_specs += [pl.BlockSpec((tm, d), lambda i: (lag(i), 0))]
        args += [res]
    in_specs += [pl.BlockSpec((1, d), lambda i: (0, 0))]
    args += [g]
    out_specs = [pl.BlockSpec((tm, d), lambda i: (lag(i), 0))]
    out_shape = [SDS((m, d), F32)]
    if route:
        ne = router_w.shape[1]
        in_specs += [pl.BlockSpec((d, ne), lambda i: (0, 0))]
        args += [router_w]
        out_specs += [pl.BlockSpec((tm, d), lambda i: (lag(i), 0)),
                      pl.BlockSpec((tm, ne), lambda i: (lag(i), 0)),
                      pl.BlockSpec((tm, ne), lambda i: (lag(i), 0))]
        out_shape += [SDS((m, d), F32), SDS((m, ne), F32), SDS((m, ne), F32)]
    else:
        out_specs += [pl.BlockSpec((tm, d), lambda i: (lag(i), 0))]
        out_shape += [SDS((m, d), BF16)]
    return pl.pallas_call(
        kern, grid=(n_m + 1,), in_specs=in_specs, out_specs=out_specs, out_shape=out_shape,
        scratch_shapes=[pltpu.VMEM((2, tm, d), F32)],
        compiler_params=_cparams(1), name="out_proj_route" if route else "out_proj")(*args)


def _ffn(x, wg3, wu3, wd3, tm, tf):
    m, k = x.shape
    f = wg3.shape[2]
    d = wd3.shape[2]

    def kern(x_ref, wg_ref, wu_ref, wd_ref, o_ref):
        @pl.when(pl.program_id(1) == 0)
        def _():
            o_ref[...] = jnp.zeros_like(o_ref)

        xb = x_ref[...]
        gg = _dot(xb, wg_ref[...].astype(BF16))
        uu = _dot(xb, wu_ref[...].astype(BF16))
        a = (gg * _sigmoid(gg) * uu).astype(BF16)
        o_ref[...] += _dot(a, wd_ref[...].astype(BF16))

    return pl.pallas_call(
        kern, grid=(m // tm, f // tf),
        in_specs=[pl.BlockSpec((tm, k), lambda i, j: (i, 0)),
                  pl.BlockSpec((None, k, tf), lambda i, j: (0, 0, j)),
                  pl.BlockSpec((None, k, tf), lambda i, j: (0, 0, j)),
                  pl.BlockSpec((None, tf, d), lambda i, j: (0, j, 0))],
        out_specs=pl.BlockSpec((tm, d), lambda i, j: (i, 0)),
        out_shape=SDS((m, d), F32),
        compiler_params=_cparams(2), name="ffn")(x, wg3, wu3, wd3)


def _ffn_up(x, wg3, wu3, tm, tn):
    m, k = x.shape
    f = wg3.shape[2]

    def kern(x_ref, wg_ref, wu_ref, a_ref, sg_ref, su_ref):
        @pl.when(pl.program_id(1) == 0)
        def _():
            sg_ref[...] = wg_ref[...].astype(BF16)
            su_ref[...] = wu_ref[...].astype(BF16)

        xb = x_ref[...]
        gg = _dot(xb, sg_ref[...])
        uu = _dot(xb, su_ref[...])
        a_ref[...] = (gg * _sigmoid(gg) * uu).astype(BF16)

    wspec = pl.BlockSpec((None, k, tn), lambda n, i: (0, 0, n))
    return pl.pallas_call(
        kern, grid=(f // tn, m // tm),
        in_specs=[pl.BlockSpec((tm, k), lambda n, i: (i, 0)), wspec, wspec],
        out_specs=pl.BlockSpec((tm, tn), lambda n, i: (i, n)),
        out_shape=SDS((m, f), BF16),
        scratch_shapes=[pltpu.VMEM((k, tn), BF16)] * 2,
        compiler_params=_cparams(2), name="ffn_up")(x, wg3, wu3)


def _ffn_down(a, wd3, h, tm, tn):
    m, f = a.shape
    d = wd3.shape[2]

    def kern(a_ref, w_ref, h_ref, o_ref, s_ref):
        @pl.when(pl.program_id(1) == 0)
        def _():
            s_ref[...] = w_ref[...].astype(BF16)

        o_ref[...] = h_ref[...] + _dot(a_ref[...], s_ref[...])

    blk = pl.BlockSpec((tm, tn), lambda n, i: (i, n))
    return pl.pallas_call(
        kern, grid=(d // tn, m // tm),
        in_specs=[pl.BlockSpec((tm, f), lambda n, i: (i, 0)),
                  pl.BlockSpec((None, f, tn), lambda n, i: (0, 0, n)), blk],
        out_specs=blk,
        out_shape=SDS((m, d), F32),
        scratch_shapes=[pltpu.VMEM((f, tn), BF16)],
        compiler_params=_cparams(2), name="ffn_down")(a, wd3, h)


def _norm(h, g, tm):
    m, d = h.shape

    def kern(h_ref, g_ref, n_ref):
        n_ref[...] = _rms(h_ref[...], g_ref[...]).astype(BF16)

    row = pl.BlockSpec((tm, d), lambda i: (i, 0))
    return pl.pallas_call(
        kern, grid=(m // tm,),
        in_specs=[row, pl.BlockSpec((1, d), lambda i: (0, 0))],
        out_specs=row,
        out_shape=SDS((m, d), BF16),
        compiler_params=_cparams(1), name="norm")(h, g)


def _add_norm(a, b, g, tm):
    m, d = a.shape

    def kern(a_ref, b_ref, g_ref, h_ref, n_ref):
        h = a_ref[...] + b_ref[...]
        h_ref[...] = h
        n_ref[...] = _rms(h, g_ref[...]).astype(BF16)

    row = pl.BlockSpec((tm, d), lambda i: (i, 0))
    return pl.pallas_call(
        kern, grid=(m // tm,),
        in_specs=[row, row, pl.BlockSpec((1, d), lambda i: (0, 0))],
        out_specs=[row, row],
        out_shape=[SDS((m, d), F32), SDS((m, d), BF16)],
        compiler_params=_cparams(1), name="add_norm")(a, b, g)


def _in_proj1(x, w3, tm, tn):
    m, k = x.shape
    dc = w3.shape[2] // 3
    nb = dc // tn

    def kern(x_ref, wb_ref, wc_ref, wv_ref, gb_ref, cv_ref, sb_ref, sc_ref, sv_ref):
        @pl.when(pl.program_id(1) == 0)
        def _():
            sb_ref[...] = wb_ref[...].astype(BF16)
            sc_ref[...] = wc_ref[...].astype(BF16)
            sv_ref[...] = wv_ref[...].astype(BF16)

        xb = x_ref[...]
        gb_ref[...] = _dot(xb, sb_ref[...]).astype(BF16)
        cv_ref[...] = _dot(xb, sc_ref[...]) * _dot(xb, sv_ref[...])

    def wspec(g):
        return pl.BlockSpec((None, k, tn), lambda n, i: (0, 0, g * nb + n))

    return pl.pallas_call(
        kern, grid=(nb, m // tm),
        in_specs=[pl.BlockSpec((tm, k), lambda n, i: (i, 0)), wspec(0), wspec(1), wspec(2)],
        out_specs=[pl.BlockSpec((tm, tn), lambda n, i: (i, n))] * 2,
        out_shape=[SDS((m, dc), BF16), SDS((m, dc), F32)],
        scratch_shapes=[pltpu.VMEM((k, tn), BF16)] * 3,
        compiler_params=_cparams(2), name="in_proj1")(x, w3, w3, w3)


def _shortconv_prompt(gb, cv, cw, bsz, t):
    m, dc = cv.shape
    tc = _tile(dc, 512, LANE)
    nw = cw.shape[0]

    def kern(gb_ref, cv_ref, cw_ref, u_ref, buf_ref):
        x = cv_ref[...]
        row = lax.broadcasted_iota(I32, x.shape, 0)
        w = cw_ref[...]
        u = w[nw - 1:nw, :] * x
        for s in range(1, nw):
            u = u + w[nw - 1 - s:nw - s, :] * _shift_rows(x, s, row, 0.0)
        u_ref[...] = (gb_ref[...].astype(F32) * u).astype(BF16)
        buf_ref[...] = x[t - (nw - 1):t, :]

    blk = pl.BlockSpec((t, tc), lambda b_, c: (b_, c))
    return pl.pallas_call(
        kern, grid=(bsz, dc // tc),
        in_specs=[blk, blk, pl.BlockSpec((nw, tc), lambda b_, c: (0, c))],
        out_specs=[blk, pl.BlockSpec((None, nw - 1, tc), lambda b_, c: (b_, 0, c))],
        out_shape=[SDS((m, dc), BF16), SDS((bsz, nw - 1, dc), F32)],
        compiler_params=_cparams(2), name="shortconv_prompt")(gb, cv, cw)


def _shortconv_sample(gb, cv, u_all, st, cw, mp, ms):
    dc = cv.shape[1]
    tc = _tile(dc, 512, LANE)
    nw = cw.shape[0]
    rb = mp // ms

    def kern(gb_ref, cv_ref, u_in_ref, st_ref, cw_ref, u_ref, buf_ref):
        del u_in_ref
        x = cv_ref[...]
        w = cw_ref[...]
        u = w[nw - 1:nw, :] * x
        for s in range(nw - 1):
            u = u + w[s:s + 1, :] * st_ref[:, s, :]
        u_ref[...] = (gb_ref[...].astype(F32) * u).astype(BF16)
        for s in range(nw - 2):
            buf_ref[:, s, :] = st_ref[:, s + 1, :]
        buf_ref[:, nw - 2, :] = x

    blk = pl.BlockSpec((ms, tc), lambda c: (rb, c))
    stb = pl.BlockSpec((ms, nw - 1, tc), lambda c: (0, 0, c))
    return pl.pallas_call(
        kern, grid=(dc // tc,),
        in_specs=[blk, blk, pl.BlockSpec(memory_space=pl.ANY), stb,
                  pl.BlockSpec((nw, tc), lambda c: (0, c))],
        out_specs=[blk, stb],
        out_shape=[SDS(u_all.shape, BF16), SDS((ms, nw - 1, dc), F32)],
        input_output_aliases={2: 0},
        compiler_params=_cparams(1), name="shortconv_sample")(gb, cv, u_all, st, cw)


def _moe_rank(mh, tm):
    m, ne = mh.shape

    def kern(mh_ref, ex_ref, cnt_ref, carry_ref):
        @pl.when(pl.program_id(0) == 0)
        def _():
            carry_ref[...] = jnp.zeros_like(carry_ref)

        x = mh_ref[...]
        tri = (lax.broadcasted_iota(I32, (tm, tm), 0) > lax.broadcasted_iota(I32, (tm, tm), 1))
        ex = _dot(jnp.where(tri, 1.0, 0.0).astype(BF16), x.astype(BF16)) + carry_ref[...]
        ex_ref[...] = ex
        tot = ex[tm - 1:tm, :] + x[tm - 1:tm, :]
        carry_ref[...] = tot
        cnt_ref[...] = tot

    return pl.pallas_call(
        kern, grid=(m // tm,),
        in_specs=[pl.BlockSpec((tm, ne), lambda i: (i, 0))],
        out_specs=[pl.BlockSpec((tm, ne), lambda i: (i, 0)), pl.BlockSpec((1, ne), lambda i: (0, 0))],
        out_shape=[SDS((m, ne), F32), SDS((1, ne), F32)],
        scratch_shapes=[pltpu.VMEM((1, ne), F32)],
        compiler_params=_cparams(1), name="moe_rank")(mh)


def _moe_pos(mh, gd, ex, off, tm):
    m, ne = mh.shape

    def kern(mh_ref, gd_ref, ex_ref, off_ref, pos_ref, gate_ref):
        sel = mh_ref[...] > 0.5
        pd = ex_ref[...] + off_ref[...]
        big = jnp.float32(3e38)
        p_lo = jnp.min(jnp.where(sel, pd, big), axis=1, keepdims=True)
        p_hi = jnp.max(jnp.where(sel, pd, -big), axis=1, keepdims=True)
        gdv = gd_ref[...]
        g_lo = jnp.sum(jnp.where(sel & (pd == p_lo), gdv, 0.0), axis=1, keepdims=True)
        g_hi = jnp.sum(jnp.where(sel & (pd == p_hi), gdv, 0.0), axis=1, keepdims=True)
        pos_ref[:, 0:1] = p_lo.astype(I32)
        pos_ref[:, 1:2] = p_hi.astype(I32)
        gate_ref[:, 0:1] = g_lo
        gate_ref[:, 1:2] = g_hi

    blk = pl.BlockSpec((tm, ne), lambda i: (i, 0))
    two = pl.BlockSpec((tm, TOP_K), lambda i: (i, 0))
    return pl.pallas_call(
        kern, grid=(m // tm,),
        in_specs=[blk, blk, blk, pl.BlockSpec((1, ne), lambda i: (0, 0))],
        out_specs=[two, two],
        out_shape=[SDS((m, TOP_K), I32), SDS((m, TOP_K), F32)],
        compiler_params=_cparams(1), name="moe_pos")(mh, gd, ex, off)


def _moe_scatter(pos_flat, tail_row, has_tail, x, n_rows, tm, sub):
    m, w = x.shape
    ne = tail_row.shape[0]

    def kern(pos_ref, tail_ref, flag_ref, x_ref, xs_ref, zero_ref, sem, zsem):
        i = pl.program_id(0)

        @pl.when(i == 0)
        def _():
            zero_ref[...] = jnp.zeros_like(zero_ref)

            def zcopy(e):
                dst = xs_ref.at[pl.ds(pl.multiple_of(tail_ref[e], sub), sub)]
                return pltpu.make_async_copy(zero_ref, dst, zsem)

            for e in range(ne):
                @pl.when(flag_ref[e] > 0)
                def _():
                    zcopy(e).start()

            for e in range(ne):
                @pl.when(flag_ref[e] > 0)
                def _():
                    zcopy(e).wait()

        def row_copy(r, p):
            return pltpu.make_async_copy(x_ref.at[pl.ds(r, 1)], xs_ref.at[pl.ds(p, 1)], sem)

        def issue(r, c):
            t = i * tm + r
            row_copy(r, pos_ref[TOP_K * t]).start(priority=0)
            row_copy(r, pos_ref[TOP_K * t + 1]).start(priority=1)
            return c

        lax.fori_loop(0, tm, issue, 0, unroll=8)

        def drain(r, c):
            row_copy(0, 0).wait()
            row_copy(0, 0).wait()
            return c

        lax.fori_loop(0, tm, drain, 0, unroll=8)

    return pl.pallas_call(
        kern,
        grid_spec=pltpu.PrefetchScalarGridSpec(
            num_scalar_prefetch=3, grid=(m // tm,),
            in_specs=[pl.BlockSpec((tm, w), lambda i, *_: (i, 0))],
            out_specs=pl.BlockSpec(memory_space=pl.ANY),
            scratch_shapes=[pltpu.VMEM((sub, w), F32), pltpu.SemaphoreType.DMA(()),
                            pltpu.SemaphoreType.DMA(())]),
        out_shape=SDS((n_rows, w), F32),
        compiler_params=_cparams(1), name="moe_scatter")(pos_flat, tail_row, has_tail, x)


def _moe_ffn(st_e, st_row, st_nsub, xs, wg4, wu4, wd4, rs, sub, tf):
    n_rows, k = xs.shape
    f = wg4.shape[3]
    d = wd4.shape[3]
    ng = st_e.shape[0]
    nf = f // tf
    assert k == d and nf >= 2
    kc = _tile(k, 512, LANE)
    big, mid = 4 * sub, 2 * sub

    def kern(se_ref, sr_ref, sn_ref, xs_ref, wg_ref, wu_ref, wd_ref, ys_ref,
             x_ref, acc_ref, a_ref, wgb_ref, wub_ref, wdb_ref, sem_in, sem_out):
        g = pl.program_id(0)
        j = pl.program_id(1)
        nsub = sn_ref[g]
        row0 = sr_ref[g]
        nbig = lax.shift_right_logical(nsub, 2)
        has_mid = (nsub & 2) != 0
        has_small = (nsub & 1) != 0
        start_mid = nbig * big
        start_small = start_mid + jnp.where(has_mid, mid, 0)

        def rows(start, size):
            return pl.ds(pl.multiple_of(start, sub), size)

        def in_copy(s):
            return pltpu.make_async_copy(xs_ref.at[rows(row0 + s * sub, sub)],
                                         acc_ref.at[rows(s * sub, sub)], sem_in)

        def out_copy(start, size):
            return pltpu.make_async_copy(acc_ref.at[rows(start, size)],
                                         ys_ref.at[rows(row0 + start, size)], sem_out)

        def each_sub(fn):
            def body(s, c):
                fn(s)
                return c
            lax.fori_loop(0, nsub, body, 0)

        def cast_weights():
            for c in range(k // kc):
                ks = slice(c * kc, (c + 1) * kc)
                wgb_ref[ks, :] = wg_ref[ks, :].astype(BF16)
                wub_ref[ks, :] = wu_ref[ks, :].astype(BF16)
            wdb_ref[...] = wd_ref[...].astype(BF16)

        def up(start, size, slot):
            xb = x_ref[rows(start, size), :]
            gg = _dot(xb, wgb_ref[...])
            uu = _dot(xb, wub_ref[...])
            a_ref[slot, 0:size, :] = (gg * _sigmoid(gg) * uu).astype(BF16)

        def down(start, size, slot, first, last):
            dd = _dot(a_ref[slot, 0:size, :], wdb_ref[...])
            if first:
                acc_ref[rows(start, size), :] = dd
            else:
                acc_ref[rows(start, size), :] += dd
            if last:
                out_copy(start, size).start()

        def compute(first, last, cast):
            @pl.when(nbig > 0)
            def _():
                if cast:
                    cast_weights()
                up(0, big, 0)

                def body(s, c):
                    up(s * big, big, s & 1)
                    down((s - 1) * big, big, (s - 1) & 1, first, last)
                    return c

                lax.fori_loop(1, nbig, body, 0)
                down((nbig - 1) * big, big, (nbig - 1) & 1, first, last)

            if cast:
                @pl.when(nbig == 0)
                def _():
                    cast_weights()

            @pl.when(has_mid)
            def _():
                up(start_mid, mid, 0)
                down(start_mid, mid, 0, first, last)

            @pl.when(has_small)
            def _():
                up(start_small, sub, 0)
                down(start_small, sub, 0, first, last)

        @pl.when(nsub > 0)
        def _():
            @pl.when(j == 0)
            def _():
                each_sub(lambda s: in_copy(s).start())
                cast_weights()

                def to_bf16(s):
                    x_ref[rows(s * sub, sub), :] = acc_ref[rows(s * sub, sub), :].astype(BF16)

                each_sub(lambda s: in_copy(s).wait())
                each_sub(to_bf16)
                compute(True, False, False)

            @pl.when((j > 0) & (j < nf - 1))
            def _():
                compute(False, False, True)

            @pl.when(j == nf - 1)
            def _():
                compute(False, True, True)

                def wait_big(s, c):
                    out_copy(0, big).wait()
                    return c

                lax.fori_loop(0, nbig, wait_big, 0)

                @pl.when(has_mid)
                def _():
                    out_copy(0, mid).wait()

                @pl.when(has_small)
                def _():
                    out_copy(0, sub).wait()

    def widx(g, j, se, sr, sn):
        return (0, se[g], 0, jnp.where(sn[g] > 0, j, nf - 1))

    def didx(g, j, se, sr, sn):
        return (0, se[g], jnp.where(sn[g] > 0, j, nf - 1), 0)

    return pl.pallas_call(
        kern,
        grid_spec=pltpu.PrefetchScalarGridSpec(
            num_scalar_prefetch=3, grid=(ng, nf),
            in_specs=[pl.BlockSpec(memory_space=pl.ANY),
                      pl.BlockSpec((None, None, k, tf), widx),
                      pl.BlockSpec((None, None, k, tf), widx),
                      pl.BlockSpec((None, None, tf, d), didx)],
            out_specs=pl.BlockSpec(memory_space=pl.ANY),
            scratch_shapes=[pltpu.VMEM((rs, k), BF16), pltpu.VMEM((rs, d), F32),
                            pltpu.VMEM((2, 4 * sub, tf), BF16),
                            pltpu.VMEM((k, tf), BF16), pltpu.VMEM((k, tf), BF16),
                            pltpu.VMEM((tf, d), BF16),
                            pltpu.SemaphoreType.DMA(()), pltpu.SemaphoreType.DMA(())]),
        out_shape=SDS((n_rows, d), F32),
        compiler_params=_cparams(2), name="moe_ffn")(st_e, st_row, st_nsub, xs, wg4, wu4, wd4)


def _moe_combine(pos_flat, h, gates, ys, g, mp, ms, tp):
    m, d = h.shape
    n_p = mp // tp
    assert mp % tp == 0 and ms <= tp and ms % 8 == 0

    def kern(pos_ref, h_ref, gate_ref, ys_ref, g_ref, yp_ref, ysm_ref, a_ref, b_ref, sem):
        i = pl.program_id(0)

        def copies(tile, r):
            t = tile * tp + r
            slot = tile & 1
            return (pltpu.make_async_copy(ys_ref.at[pl.ds(pos_ref[TOP_K * t], 1)],
                                          a_ref.at[slot, pl.ds(r, 1)], sem.at[slot]),
                    pltpu.make_async_copy(ys_ref.at[pl.ds(pos_ref[TOP_K * t + 1], 1)],
                                          b_ref.at[slot, pl.ds(r, 1)], sem.at[slot]))

        def gather(tile):
            def issue(r, c):
                ca, cb = copies(tile, r)
                ca.start(priority=0)
                cb.start(priority=1)
                return c

            @pl.when(tile < n_p)
            def _():
                lax.fori_loop(0, tp, issue, 0, unroll=8)

            @pl.when(tile == n_p)
            def _():
                lax.fori_loop(0, ms, issue, 0, unroll=8)

        def combined(n):
            def drain(r, c):
                ca, cb = copies(i, 0)
                ca.wait()
                cb.wait()
                return c

            lax.fori_loop(0, n, drain, 0, unroll=8)
            slot = i & 1
            gt = gate_ref[0:n, :]
            hh = (h_ref[0:n, :] + gt[:, 0:1] * a_ref[slot, 0:n, :]
                  + gt[:, 1:2] * b_ref[slot, 0:n, :])
            return _rms(hh, g_ref[...])

        @pl.when(i == 0)
        def _():
            gather(i)

        gather(i + 1)

        @pl.when(i < n_p)
        def _():
            yp_ref[...] = combined(tp)

        @pl.when(i == n_p)
        def _():
            ysm_ref[...] = combined(ms)

    return pl.pallas_call(
        kern,
        grid_spec=pltpu.PrefetchScalarGridSpec(
            num_scalar_prefetch=1, grid=(n_p + 1,),
            in_specs=[pl.BlockSpec((tp, d), lambda i, *_: (i, 0)),
                      pl.BlockSpec((tp, TOP_K), lambda i, *_: (i, 0)),
                      pl.BlockSpec(memory_space=pl.ANY),
                      pl.BlockSpec((1, d), lambda i, *_: (0, 0))],
            out_specs=[pl.BlockSpec((tp, d), lambda i, *_: (jnp.minimum(i, n_p - 1), 0)),
                       pl.BlockSpec((ms, d), lambda i, *_: (0, 0))],
            scratch_shapes=[pltpu.VMEM((2, tp, d), F32), pltpu.VMEM((2, tp, d), F32),
                            pltpu.SemaphoreType.DMA((2,))]),
        out_shape=[SDS((mp, d), F32), SDS((ms, d), F32)],
        compiler_params=_cparams(1), name="moe_combine")(pos_flat, h, gates, ys, g)


def _moe_tables(cnt, sub, rs, ng):
    ne = cnt.shape[0]
    nsub_e = (cnt + sub - 1) // sub
    size_e = nsub_e * sub
    off = jnp.cumsum(size_e) - size_e
    spr = rs // sub
    nst_e = (nsub_e + spr - 1) // spr
    st_start = jnp.cumsum(nst_e) - nst_e
    n_act = jnp.sum(nst_e)
    gidx = jnp.arange(ng, dtype=I32)
    e_of = jnp.sum((gidx[:, None] >= st_start[None, :]).astype(I32), axis=1) - 1
    e_of = jnp.clip(e_of, 0, ne - 1)
    kth = gidx - st_start[e_of]
    active = gidx < n_act
    nsub = jnp.where(active, jnp.clip(nsub_e[e_of] - kth * spr, 0, spr), 0)
    row = off[e_of] + kth * rs
    last = jnp.maximum(n_act - 1, 0)
    st_e = jnp.where(active, e_of, e_of[last]).astype(I32)
    st_row = jnp.where(active, row, 0).astype(I32)
    tail_row = jnp.where(size_e > 0, off + size_e - sub, 0).astype(I32)
    has_tail = (size_e > 0).astype(I32)
    return off, st_e, st_row, nsub.astype(I32), tail_row, has_tail


def kernel(x_prompt, x_sample, state_rg_conv, state_rg_h, state_gla, state_sc_conv, norm_mix_e, w_in_e, rg_conv_w, rg_conv_b, rg_w_a, rg_b_a, rg_w_x, rg_b_x, rg_lambda, gla_w_gate, gla_b_gate, gla_norm, w_out_e, norm_ffn_e, ffn_w_gate, ffn_w_up, ffn_w_down, norm_mix_o, w_in_o, sc_conv_w, w_out_o, norm_ffn_o, router_w, moe_w_gate, moe_w_up, moe_w_down, final_norm):
    bsz, t, d = x_prompt.shape
    ms = x_sample.shape[0]
    assert x_sample.shape[1] == 1 and w_in_e.shape[0] == 1 and w_in_o.shape[0] == 1
    mp = bsz * t
    m = mp + ms
    d_rnn = rg_lambda.shape[1]
    nh, dk, dv = state_gla.shape[2:]
    hk, hv = nh * dk, nh * dv
    n_main = 2 * d_rnn + 2 * hk + 2 * hv
    d_mix = d_rnn + hv
    ne = router_w.shape[2]
    xp = x_prompt.reshape(mp, d)
    xs = x_sample.reshape(ms, d)
    row = lambda v: v.reshape(1, -1)

    tp = _tile(mp, TP_TARGET, 16)
    tm = _tile(m, TM_TARGET, 16)
    tm_in = _tile(m, TM_IN_TARGET, 16)
    tm_out = _tile(m, TM_OUT_TARGET, 16)

    hn0 = _norm_in(xp, xs, norm_mix_e, tp)
    w_in_t = jnp.swapaxes(w_in_e, 1, 2)
    z0 = _in_proj0(hn0, w_in_t, n_main, tm_in, _tile(n_main, 1024, LANE))
    la = _gla_gate(hn0, w_in_t, n_main, gla_w_gate[0], gla_b_gate, tm)
    y_mix, rgc_p, rgh_p = _rglru_prompt(z0, rg_conv_w[0], rg_conv_b, rg_w_a[0], rg_b_a, rg_w_x[0],
                                        rg_b_x, rg_lambda, bsz, t, d_rnn, d_mix)
    y_mix, rgc_s, rgh_s = _rglru_sample(z0, y_mix, state_rg_conv[0], state_rg_h[0], rg_conv_w[0],
                                        rg_conv_b, rg_w_a[0], rg_b_a, rg_w_x[0], rg_b_x, rg_lambda,
                                        mp, ms, d_rnn)
    y_mix, gla_p = _gla_prompt(z0, la, y_mix, gla_norm, bsz, t, nh, dk, dv, 2 * d_rnn, d_rnn)
    y_mix, gla_s = _gla_sample(z0, la, y_mix, state_gla, gla_norm, mp, ms, nh, dk, dv,
                               2 * d_rnn, d_rnn)
    h1, hn1 = _out_proj(y_mix, _cast_bf16(w_out_e), xp, norm_ffn_e, tm_out, res_tail=xs)

    act = _ffn_up(hn1, ffn_w_gate, ffn_w_up, tm_in, _tile(ffn_w_gate.shape[2], 512, LANE))
    h2 = _ffn_down(act, ffn_w_down, h1, tm, _tile(d, 512, LANE))
    hn2 = _norm(h2, norm_mix_o, tm_in)

    gb, cv = _in_proj1(hn2, w_in_o, tm_in, _tile(w_in_o.shape[2] // 3, 256, LANE))
    u, sc_p = _shortconv_prompt(gb, cv, sc_conv_w[0], bsz, t)
    u, sc_s = _shortconv_sample(gb, cv, u, state_sc_conv[0], sc_conv_w[0], mp, ms)
    h3, hn3, mh, gd = _out_proj(u, _cast_bf16(w_out_o), h2, norm_ffn_o, tm_out,
                                router_w=router_w[0])

    sub = MOE_SUB
    rs = MOE_SPR * sub
    ng = (TOP_K * m) // rs + ne
    n_rows = TOP_K * m + ne * (sub - 1)
    n_rows = ((n_rows + sub - 1) // sub) * sub
    ex, cnt = _moe_rank(mh, tm)
    off, st_e, st_row, st_nsub, tail_row, has_tail = _moe_tables(cnt[0].astype(I32), sub, rs, ng)
    pos, gates = _moe_pos(mh, gd, ex, off.astype(F32).reshape(1, ne), tm)
    pos_flat = pos.reshape(TOP_K * m)
    xsort = _moe_scatter(pos_flat, tail_row, has_tail, hn3, n_rows, tm, sub)
    ys = _moe_ffn(st_e, st_row, st_nsub, xsort, moe_w_gate, moe_w_up, moe_w_down, rs, sub,
                  _tile(moe_w_gate.shape[3], 256, LANE))
    y_p, y_s = _moe_combine(pos_flat, h3, gates, ys, row(final_norm), mp, ms, tp)

    return (y_p.reshape(bsz, t, d), y_s.reshape(ms, 1, d),
            rgc_p[None], rgc_s[None], rgh_p.reshape(1, bsz, d_rnn), rgh_s[None],
            gla_p[None], gla_s, sc_p[None], sc_s[None])
```
